```python
import jax, jax.numpy as jnp
from jax import lax
import numpy as np

D_MODEL = 1024
BATCH = 8
SEQ = 4096
DEPTH = 4

N_MIXERS = 3
GRID_W = 64
D_FF = 2816
NORM_EPS = 1e-6
HG_HEADS = 8
HG_DK = D_MODEL // HG_HEADS
HG_DV = D_MODEL // HG_HEADS
HG_CHUNK = 16
NA_HEADS = 16
NA_HEAD_DIM = D_MODEL // NA_HEADS
NA_WIN_R = 8
NA_WIN_C = 16
MLA_HEADS = 16
MLA_Q_LORA = 768
MLA_KV_LORA = 256
MLA_NOPE = 64
MLA_ROPE = 32
MLA_V = 64
ROPE_THETA = 10000.0
Q_BLOCK = 128
N_A = (DEPTH + 2) // 3
N_B = (DEPTH + 1) // 3
N_C = DEPTH // 3
NEG_INF = -1e30

kernel_name = 'hybrid_hgrn2_natten_mla_macaron_encoder'


def _rms_norm(x, g):
    xf = x.astype(jnp.float32)
    y = xf * lax.rsqrt(jnp.mean(xf * xf, axis=-1, keepdims=True) + NORM_EPS)
    return (y * g.astype(jnp.float32)).astype(x.dtype)


def _swiglu(x, w_gu, w_down):
    g, u = jnp.split(x @ w_gu, 2, axis=-1)
    return (jax.nn.silu(g) * u) @ w_down


def _rope(x, pos):
    half = x.shape[-1] // 2
    inv_freq = ROPE_THETA ** (-jnp.arange(half, dtype=jnp.float32) / half)
    ang = pos.astype(jnp.float32)[:, None] * inv_freq[None, :]
    cos = jnp.cos(ang)[:, None, :]
    sin = jnp.sin(ang)[:, None, :]
    xf = x.astype(jnp.float32)
    x1, x2 = xf[..., :half], xf[..., half:]
    return jnp.concatenate([x1 * cos - x2 * sin, x2 * cos + x1 * sin], axis=-1).astype(x.dtype)


def _gla_chunk_scan(q, k, v, logf):
    H, T, dk = q.shape
    dv = v.shape[-1]
    n = T // HG_CHUNK
    q = q.reshape(H, n, HG_CHUNK, dk)
    k = k.reshape(H, n, HG_CHUNK, dk)
    v = v.reshape(H, n, HG_CHUNK, dv)
    b = jnp.cumsum(logf.reshape(H, n, HG_CHUNK, dk), axis=2)
    tri = np.tril(np.ones((HG_CHUNK, HG_CHUNK), dtype=bool))[:, :, None]
    diff = b[:, :, :, None, :] - b[:, :, None, :, :]
    decay = jnp.where(tri, jnp.exp(jnp.where(tri, diff, 0.0)), 0.0)
    attn = jnp.einsum('hntd,hntsd,hnsd->hnts', q, decay, k)
    o_intra = jnp.einsum('hnts,hnsv->hntv', attn, v)
    b_last = b[:, :, -1, :]
    q_inter = q * jnp.exp(b)
    k_inter = k * jnp.exp(b_last[:, :, None, :] - b)
    chunk_kv = jnp.einsum('hncd,hncv->hndv', k_inter, v)

    def step(state, inp):
        q_c, dl, kv_c = inp
        o_c = jnp.einsum('hcd,hdv->hcv', q_c, state)
        state = jnp.exp(dl)[..., None] * state + kv_c
        return state, o_c

    s0 = jnp.zeros((H, dk, dv), q.dtype)
    _, o_inter = lax.scan(step, s0, (q_inter.transpose(1, 0, 2, 3), b_last.transpose(1, 0, 2), chunk_kv.transpose(1, 0, 2, 3)))
    o = o_intra + o_inter.transpose(1, 0, 2, 3)
    return o.reshape(H, T, dv)


def _hgrn2_mixer(h, w_in, g_norm, w_out, lb):
    B, T, _ = h.shape
    q, zf, zb, i, g = jnp.split(h @ w_in, 5, axis=-1)

    def to_heads(t, d):
        return t.reshape(B, T, HG_HEADS, d).transpose(0, 2, 1, 3).astype(jnp.float32)

    lbh = lb.astype(jnp.float32).reshape(HG_HEADS, 1, HG_DK)
    log_lb = jnp.log(lbh)
    log_1m_lb = jnp.log1p(-lbh)

    def forget(z):
        logf = jnp.logaddexp(log_lb, log_1m_lb + jax.nn.log_sigmoid(to_heads(z, HG_DK)))
        return 1.0 - jnp.exp(logf), logf

    k_f, logf_f = forget(zf)
    k_b, logf_b = forget(zb)
    qh = to_heads(q, HG_DK) * (HG_DK ** -0.5)
    vh = to_heads(i, HG_DV)

    def per_seq(a):
        q_s, kf_s, lf_s, kb_s, lbw_s, v_s = a
        fwd = _gla_chunk_scan(q_s, kf_s, v_s, lf_s)
        bwd = _gla_chunk_scan(q_s[:, ::-1], kb_s[:, ::-1], v_s[:, ::-1], lbw_s[:, ::-1])[:, ::-1]
        return fwd + bwd

    o = lax.map(per_seq, (qh, k_f, logf_f, k_b, logf_b, vh))
    o = o.transpose(0, 2, 1, 3)
    gh = g.reshape(B, T, HG_HEADS, HG_DV).astype(jnp.float32)
    o = _rms_norm(o, g_norm) * jax.nn.silu(gh)
    return o.reshape(B, T, D_MODEL).astype(h.dtype) @ w_out


def _na_mixer(h, w_in, q_norm, k_norm, rpb, w_out):
    B, T, _ = h.shape
    rows = T // GRID_W
    win_r = min(NA_WIN_R, rows)
    q, k, v = jnp.split(h @ w_in, 3, axis=-1)
    shp = (B, rows, GRID_W, NA_HEADS, NA_HEAD_DIM)
    q = _rms_norm(q.reshape(shp), q_norm)
    k = _rms_norm(k.reshape(shp), k_norm)
    v = v.reshape(shp)
    cols = np.arange(GRID_W)
    col_start = np.clip(cols - NA_WIN_C // 2, 0, GRID_W - NA_WIN_C)
    col_mask = (cols[None, :] >= col_start[:, None]) & (cols[None, :] < col_start[:, None] + NA_WIN_C)
    col_off = np.clip(cols[None, :] - cols[:, None] + NA_WIN_C - 1, 0, 2 * NA_WIN_C - 2)
    rpb_cols = rpb.astype(jnp.float32)[:, :, col_off]
    scale = NA_HEAD_DIM ** -0.5

    def attend_row(r):
        r0 = jnp.clip(r - win_r // 2, 0, rows - win_r)
        q_r = lax.dynamic_index_in_dim(q, r, axis=1, keepdims=False)
        k_blk = lax.dynamic_slice_in_dim(k, r0, win_r, axis=1)
        v_blk = lax.dynamic_slice_in_dim(v, r0, win_r, axis=1)
        s = jnp.einsum('bqhd,brkhd->bhqrk', q_r, k_blk).astype(jnp.float32) * scale
        row_off = r0 + jnp.arange(win_r) - r + NA_WIN_R - 1
        bias = jnp.take(rpb_cols, row_off, axis=1).transpose(0, 2, 1, 3)
        s = jnp.where(col_mask[:, None, :], s + bias, NEG_INF)
        p = jax.nn.softmax(s.reshape(B, NA_HEADS, GRID_W, win_r * GRID_W), axis=-1)
        p = p.reshape(B, NA_HEADS, GRID_W, win_r, GRID_W).astype(v.dtype)
        return jnp.einsum('bhqrk,brkhd->bqhd', p, v_blk)

    o = lax.map(attend_row, jnp.arange(rows))
    o = o.transpose(1, 0, 2, 3, 4).reshape(B, T, D_MODEL)
    return o @ w_out


def _dense_attention_blocks(q, k, v):
    B, T, H, dqk = q.shape
    n_blk = T // Q_BLOCK
    scale = dqk ** -0.5
    q_blocks = q.reshape(B, n_blk, Q_BLOCK, H, dqk).transpose(1, 0, 2, 3, 4)

    def attend(q_b):
        s = jnp.einsum('bqhd,bkhd->bhqk', q_b, k).astype(jnp.float32) * scale
        p = jax.nn.softmax(s, axis=-1).astype(v.dtype)
        return jnp.einsum('bhqk,bkhd->bqhd', p, v)

    o = lax.map(attend, q_blocks)
    return o.transpose(1, 0, 2, 3, 4).reshape(B, T, H * v.shape[-1])


def _mla_mixer(h, w_in, q_a_norm, w_uq, kv_a_norm, w_ukv, q_norm, k_norm, w_out):
    B, T, _ = h.shape
    c = h @ w_in
    c_q = c[..., :MLA_Q_LORA]
    c_kv = c[..., MLA_Q_LORA:MLA_Q_LORA + MLA_KV_LORA]
    k_rope = c[..., MLA_Q_LORA + MLA_KV_LORA:]
    q = (_rms_norm(c_q, q_a_norm) @ w_uq).reshape(B, T, MLA_HEADS, MLA_NOPE + MLA_ROPE)
    kv = (_rms_norm(c_kv, kv_a_norm) @ w_ukv).reshape(B, T, MLA_HEADS, MLA_NOPE + MLA_V)
    k_rope = jnp.broadcast_to(k_rope[:, :, None, :], (B, T, MLA_HEADS, MLA_ROPE))
    k = jnp.concatenate([kv[..., :MLA_NOPE], k_rope], axis=-1)
    v = kv[..., MLA_NOPE:]
    q = _rms_norm(q, q_norm)
    k = _rms_norm(k, k_norm)
    pos = jnp.arange(T)
    q = jnp.concatenate([q[..., :MLA_NOPE], _rope(q[..., MLA_NOPE:], pos)], axis=-1)
    k = jnp.concatenate([k[..., :MLA_NOPE], _rope(k[..., MLA_NOPE:], pos)], axis=-1)
    return _dense_attention_blocks(q, k, v) @ w_out


def setup_inputs(seed: int = 0) -> dict:
    key = jax.random.key(seed)
    ks = iter(jax.random.split(key, 32))

    def nrm(shape, scale):
        return jax.random.normal(next(ks), shape, jnp.float32) * scale

    def gain(shape):
        return 1.0 + nrm(shape, 0.02)

    D = D_MODEL
    qk_dim = MLA_NOPE + MLA_ROPE
    return {
        'x': nrm((BATCH, SEQ, D), 1.0),
        'ffn1_norm': gain((DEPTH, D)),
        'ffn1_w_gu': nrm((DEPTH, D, 2 * D_FF), D ** -0.5),
        'ffn1_w_down': nrm((DEPTH, D_FF, D), D_FF ** -0.5),
        'mix_norm': gain((DEPTH, D)),
        'ffn2_norm': gain((DEPTH, D)),
        'ffn2_w_gu': nrm((DEPTH, D, 2 * D_FF), D ** -0.5),
        'ffn2_w_down': nrm((DEPTH, D_FF, D), D_FF ** -0.5),
        'hg_lb_logits': nrm((DEPTH, HG_HEADS * HG_DK), 0.5),
        'hg_w_in': nrm((N_A, D, 5 * D), D ** -0.5),
        'hg_g_norm': gain((N_A, HG_DV)),
        'hg_w_out': nrm((N_A, D, D), D ** -0.5),
        'na_w_in': nrm((N_B, D, 3 * D), D ** -0.5),
        'na_q_norm': gain((N_B, NA_HEAD_DIM)),
        'na_k_norm': gain((N_B, NA_HEAD_DIM)),
        'na_rpb': nrm((N_B, NA_HEADS, 2 * NA_WIN_R - 1, 2 * NA_WIN_C - 1), 0.2),
        'na_w_out': nrm((N_B, D, D), D ** -0.5),
        'mla_w_in': nrm((N_C, D, MLA_Q_LORA + MLA_KV_LORA + MLA_ROPE), D ** -0.5),
        'mla_q_a_norm': gain((N_C, MLA_Q_LORA)),
        'mla_w_uq': nrm((N_C, MLA_Q_LORA, MLA_HEADS * qk_dim), MLA_Q_LORA ** -0.5),
        'mla_kv_a_norm': gain((N_C, MLA_KV_LORA)),
        'mla_w_ukv': nrm((N_C, MLA_KV_LORA, MLA_HEADS * (MLA_NOPE + MLA_V)), MLA_KV_LORA ** -0.5),
        'mla_q_norm': gain((N_C, qk_dim)),
        'mla_k_norm': gain((N_C, qk_dim)),
        'mla_w_out': nrm((N_C, MLA_HEADS * MLA_V, D), (MLA_HEADS * MLA_V) ** -0.5),
    }


def reference(x, ffn1_norm, ffn1_w_gu, ffn1_w_down, mix_norm, ffn2_norm, ffn2_w_gu, ffn2_w_down,
              hg_lb_logits, hg_w_in, hg_g_norm, hg_w_out,
              na_w_in, na_q_norm, na_k_norm, na_rpb, na_w_out,
              mla_w_in, mla_q_a_norm, mla_w_uq, mla_kv_a_norm, mla_w_ukv, mla_q_norm, mla_k_norm, mla_w_out):
    gam = jnp.cumsum(jax.nn.softmax(hg_lb_logits.astype(jnp.float32), axis=0), axis=0)
    lb_all = gam - gam[0:1]
    ia = ib = ic = 0
    for layer in range(DEPTH):
        x = x + 0.5 * _swiglu(_rms_norm(x, ffn1_norm[layer]), ffn1_w_gu[layer], ffn1_w_down[layer])
        h = _rms_norm(x, mix_norm[layer])
        kind = layer % N_MIXERS
        if kind == 0:
            y = _hgrn2_mixer(h, hg_w_in[ia], hg_g_norm[ia], hg_w_out[ia], lb_all[layer])
            ia += 1
        elif kind == 1:
            y = _na_mixer(h, na_w_in[ib], na_q_norm[ib], na_k_norm[ib], na_rpb[ib], na_w_out[ib])
            ib += 1
        else:
            y = _mla_mixer(h, mla_w_in[ic], mla_q_a_norm[ic], mla_w_uq[ic], mla_kv_a_norm[ic], mla_w_ukv[ic],
                           mla_q_norm[ic], mla_k_norm[ic], mla_w_out[ic])
            ic += 1
        x = x + y
        x = x + 0.5 * _swiglu(_rms_norm(x, ffn2_norm[layer]), ffn2_w_gu[layer], ffn2_w_down[layer])
    return x
```

```python
import functools

import numpy as np
import jax
import jax.numpy as jnp
from jax import lax
from jax.experimental import pallas as pl
from jax.experimental.pallas import tpu as pltpu

D_MODEL = 1024
DEPTH = 4
N_MIXERS = 3
GRID_W = 64
D_FF = 2816
NORM_EPS = 1e-6
HG_HEADS = 8
HG_DK = 128
HG_DV = 128
NA_HEADS = 16
NA_HEAD_DIM = 64
NA_WIN_R = 8
NA_WIN_C = 16
MLA_HEADS = 16
MLA_Q_LORA = 768
MLA_KV_LORA = 256
MLA_NOPE = 64
MLA_ROPE = 32
MLA_V = 64
MLA_QK = MLA_NOPE + MLA_ROPE
ROPE_THETA = 10000.0
NEG_INF = -1e30
LOG2E = 1.4426950408889634

LANES = 128
VMEM_LIMIT = 56 * 1024 * 1024
TOKEN_TILE = 512
HG_CHUNK = 128
HG_LEVELS = (64, 32, 16, 8, 4, 2, 1)
MLA_Q_TILE = 256
F32 = jnp.float32
BF16 = jnp.bfloat16


def _params(*sem):
    return pltpu.CompilerParams(dimension_semantics=sem, vmem_limit_bytes=VMEM_LIMIT)


def _resident(shape):
    nd = len(shape)
    return pl.BlockSpec(shape, lambda *_: (0,) * nd, pipeline_mode=pl.Buffered(1))


def _rms(x, g):
    ms = jnp.mean(x * x, axis=-1, keepdims=True)
    return x * lax.rsqrt(ms + NORM_EPS) * g


def _silu(x):
    return x * (1.0 / (1.0 + jnp.exp(-x)))


def _dot(a, b):
    return jnp.dot(a, b, preferred_element_type=F32)


def _dot_nt(a, b):
    return lax.dot_general(a, b, (((1,), (1,)), ((), ())), preferred_element_type=F32)


def _dot_tn(a, b):
    return lax.dot_general(a, b, (((0,), (0,)), ((), ())), preferred_element_type=F32)


def _split3(x):
    hi = x.astype(BF16)
    r1 = x - hi.astype(F32)
    mid = r1.astype(BF16)
    lo = (r1 - mid.astype(F32)).astype(BF16)
    return hi, mid, lo


FF_CHUNKS = ((0, 1536), (1536, 2816))


def _ffn_body(x, g, wgu_ref, wd_ref):
    h = _rms(x, g).astype(BF16)
    acc = None
    for s, e in FF_CHUNKS:
        gate = _dot(h, wgu_ref[:, s:e])
        up = _dot(h, wgu_ref[:, D_FF + s:D_FF + e])
        a = (_silu(gate) * up).astype(BF16)
        d = _dot(a, wd_ref[s:e, :])
        acc = d if acc is None else acc + d
    return x + 0.5 * acc


def _ffn_kernel(x_ref, g_ref, wgu_ref, wd_ref, o_ref):
    o_ref[...] = _ffn_body(x_ref[...], g_ref[...], wgu_ref, wd_ref)


def _ffn(x, g, wgu, wd):
    m = x.shape[0]
    tm = min(TOKEN_TILE, m)
    return pl.pallas_call(
        _ffn_kernel,
        grid=(m // tm,),
        in_specs=[pl.BlockSpec((tm, D_MODEL), lambda i: (i, 0)),
                  _resident((1, D_MODEL)),
                  _resident(wgu.shape),
                  _resident(wd.shape)],
        out_specs=pl.BlockSpec((tm, D_MODEL), lambda i: (i, 0)),
        out_shape=jax.ShapeDtypeStruct((m, D_MODEL), F32),
        compiler_params=_params("parallel"),
        name="ffn",
    )(x, g, wgu, wd)


def _norm_proj_kernel(splits, x_ref, g_ref, w_ref, *o_refs):
    h = _rms(x_ref[...], g_ref[...]).astype(BF16)
    for (s, e), o_ref in zip(splits, o_refs):
        o_ref[...] = _dot(h, w_ref[:, s:e]).astype(o_ref.dtype)


def _norm_proj(x, g, w, widths, dtypes):
    m = x.shape[0]
    tm = min(TOKEN_TILE, m)
    offs = np.concatenate([[0], np.cumsum(widths)])
    splits = tuple((int(offs[i]), int(offs[i + 1])) for i in range(len(widths)))
    return pl.pallas_call(
        functools.partial(_norm_proj_kernel, splits),
        grid=(m // tm,),
        in_specs=[pl.BlockSpec((tm, D_MODEL), lambda i: (i, 0)),
                  _resident((1, D_MODEL)),
                  _resident(w.shape)],
        out_specs=[pl.BlockSpec((tm, n), lambda i: (i, 0)) for n in widths],
        out_shape=[jax.ShapeDtypeStruct((m, n), dt) for n, dt in zip(widths, dtypes)],
        compiler_params=_params("parallel"),
        name="norm_proj",
    )(x, g, w)


def _out_proj_kernel(a_ref, x_ref, w_ref, o_ref):
    o_ref[...] = x_ref[...] + _dot(a_ref[...], w_ref[...])


def _out_proj(a, x, w):
    m, k = a.shape
    tm = min(TOKEN_TILE, m)
    return pl.pallas_call(
        _out_proj_kernel,
        grid=(m // tm,),
        in_specs=[pl.BlockSpec((tm, k), lambda i: (i, 0)),
                  pl.BlockSpec((tm, D_MODEL), lambda i: (i, 0)),
                  _resident(w.shape)],
        out_specs=pl.BlockSpec((tm, D_MODEL), lambda i: (i, 0)),
        out_shape=jax.ShapeDtypeStruct((m, D_MODEL), F32),
        compiler_params=_params("parallel"),
        name="out_proj",
    )(a, x, w)


def _hgrn_tables():
    c = HG_CHUNK
    t = np.arange(c)[:, None]
    r = np.arange(c)[None, :]
    blocks = [(r <= t), (r > t)]
    pair_masks = []
    for hb in HG_LEVELS:
        bs = (t // (2 * hb)) * (2 * hb)
        upper = (t - bs) >= hb
        m_up = upper & (r >= bs + hb) & (r <= t)
        m_lo = (~upper) & (r > t) & (r <= bs + hb - 1)
        blocks.append(m_up | m_lo)
        s = r
        s_bs = (s // (2 * hb)) * (2 * hb)
        pair_masks.append(upper & (s_bs == bs) & ((s - s_bs) < hb))
    fwd = np.concatenate(blocks, axis=0).astype(np.float32)
    fwd_blocks = fwd.reshape(len(blocks), c, c)
    bwd = fwd_blocks[:, ::-1, ::-1].reshape(-1, c)
    pm_f = np.stack(pair_masks).astype(np.float32)
    pm_b = pm_f[:, ::-1, ::-1]
    return (np.stack([fwd, bwd]), np.stack([pm_f, pm_b]))


def _hgrn_chunk(direction, q32, v, z, llb, l1m, sums_ref, pair_ref, st):
    c = HG_CHUNK
    ls = jnp.minimum(z, 0.0) - jnp.log1p(jnp.exp(-jnp.abs(z)))
    y = l1m + ls
    logf = jnp.maximum(llb, y) + jnp.log1p(jnp.exp(-jnp.abs(llb - y)))
    k = 1.0 - jnp.exp(logf)
    hi, mid, lo = _split3(logf)
    sums = sums_ref[direction]
    ex = jnp.exp(_dot(sums, hi) + _dot(sums, mid) + _dot(sums, lo))
    e_in = ex[0:c]
    e_out = ex[c:2 * c]
    v32 = v.astype(F32)

    row = lax.broadcasted_iota(jnp.int32, (c, HG_DK), 0)
    att = None
    for li, hb in enumerate(HG_LEVELS):
        later = (row % (2 * hb)) >= hb
        q_rows = later if direction == 0 else jnp.logical_not(later)
        w = (jnp.where(q_rows, q32, k) * ex[(2 + li) * c:(3 + li) * c]).astype(BF16)
        a = _dot_nt(w, w) * pair_ref[direction, li]
        att = a if att is None else att + a
    diag = _dot((q32 * k).astype(BF16), jnp.ones((HG_DK, HG_DV), BF16))
    o = _dot(att.astype(BF16), v) + diag * v32
    o = o + _dot_nt((q32 * e_in).astype(BF16), st.astype(BF16))
    last = c - 1 if direction == 0 else 0
    total = e_in[last:last + 1, :]
    st_new = st * total + _dot_tn(v, (k * e_out).astype(BF16))
    return o, st_new


def _hgrn_kernel(q_ref, zf_ref, zb_ref, v_ref, llb_ref, l1m_ref, sums_ref, pair_ref,
                 o_ref, ob_ref):
    t = q_ref.shape[1]
    c = HG_CHUNK
    n = t // c
    llb = llb_ref[...]
    l1m = l1m_ref[...]
    scale = HG_DK ** -0.5

    def body(i, carry):
        st_f, st_b = carry
        rf = pl.multiple_of(i * c, c)
        rb = pl.multiple_of((n - 1 - i) * c, c)
        o_f, st_f = _hgrn_chunk(0, q_ref[0, pl.ds(rf, c), :].astype(F32) * scale,
                                v_ref[0, pl.ds(rf, c), :], zf_ref[0, pl.ds(rf, c), :],
                                llb, l1m, sums_ref, pair_ref, st_f)
        o_b, st_b = _hgrn_chunk(1, q_ref[0, pl.ds(rb, c), :].astype(F32) * scale,
                                v_ref[0, pl.ds(rb, c), :], zb_ref[0, pl.ds(rb, c), :],
                                llb, l1m, sums_ref, pair_ref, st_b)
        o_ref[0, pl.ds(rf, c), :] = o_f
        ob_ref[pl.ds(rb, c), :] = o_b
        return st_f, st_b

    zero = jnp.zeros((HG_DV, HG_DK), F32)
    lax.fori_loop(0, n, body, (zero, zero))
    o_ref[0] = o_ref[0] + ob_ref[...]


def _hgrn_scan(q, zf, zb, v, llb, l1m):
    b, t, _ = q.shape
    sums, pairs = _hgrn_tables()
    sums = jnp.asarray(sums, BF16)
    pairs = jnp.asarray(pairs, F32)
    seq = lambda: pl.BlockSpec((1, t, HG_DK), lambda i, j: (i, 0, j))
    vec = lambda: pl.BlockSpec((1, HG_DK), lambda i, j: (0, j))
    return pl.pallas_call(
        _hgrn_kernel,
        grid=(b, HG_HEADS),
        in_specs=[seq(), seq(), seq(), seq(), vec(), vec(),
                  _resident(sums.shape), _resident(pairs.shape)],
        out_specs=seq(),
        out_shape=jax.ShapeDtypeStruct((b, t, D_MODEL), F32),
        scratch_shapes=[pltpu.VMEM((t, HG_DV), F32)],
        compiler_params=_params("parallel", "parallel"),
        name="hgrn_scan",
    )(q, zf, zb, v, llb, l1m, sums, pairs)


def _hgrn_out_kernel(o_ref, gate_ref, x_ref, gn_ref, w_ref, y_ref):
    gn = gn_ref[...]
    parts = []
    for h in range(HG_HEADS):
        sl = slice(h * HG_DV, (h + 1) * HG_DV)
        parts.append(_rms(o_ref[:, sl], gn) * _silu(gate_ref[:, sl].astype(F32)))
    a = jnp.concatenate(parts, axis=-1).astype(BF16)
    y_ref[...] = x_ref[...] + _dot(a, w_ref[...])


def _hgrn_out(o, gate, x, gn, w):
    m = x.shape[0]
    tm = min(TOKEN_TILE, m)
    tile = lambda: pl.BlockSpec((tm, D_MODEL), lambda i: (i, 0))
    return pl.pallas_call(
        _hgrn_out_kernel,
        grid=(m // tm,),
        in_specs=[tile(), tile(), tile(), _resident((1, HG_DV)), _resident(w.shape)],
        out_specs=tile(),
        out_shape=jax.ShapeDtypeStruct((m, D_MODEL), F32),
        compiler_params=_params("parallel"),
        name="hgrn_out",
    )(o, gate, x, gn, w)


def _hgrn_mixer(x, mix_g, w_in, g_norm, w_out, lb):
    b, t, _ = x.shape
    m = b * t
    x2 = x.reshape(m, D_MODEL)
    q, zf, zb, v, gate = _norm_proj(x2, mix_g, w_in, (D_MODEL,) * 5, (BF16, F32, F32, BF16, F32))
    lb = lb.astype(F32).reshape(1, D_MODEL)
    r3 = lambda a: a.reshape(b, t, D_MODEL)
    o = _hgrn_scan(r3(q), r3(zf), r3(zb), r3(v), jnp.log(lb), jnp.log1p(-lb))
    y = _hgrn_out(o.reshape(m, D_MODEL), gate, x2, g_norm.reshape(1, HG_DV), w_out)
    return y.reshape(b, t, D_MODEL)


def _na_proj_kernel(x_ref, g_ref, w_ref, qg_ref, kg_ref, bd_ref, q_ref, k_ref, v_ref):
    h = _rms(x_ref[...], g_ref[...]).astype(BF16)
    bd = bd_ref[...]

    def head_norm(y, gain, mult):
        parts = []
        for j in range(D_MODEL // LANES):
            ys = y[:, j * LANES:(j + 1) * LANES]
            sq = ys * ys
            hi = sq.astype(BF16)
            lo = (sq - hi.astype(F32)).astype(BF16)
            ss = _dot(hi, bd) + _dot(lo, bd)
            parts.append(ys * lax.rsqrt(ss * (1.0 / NA_HEAD_DIM) + NORM_EPS) * (gain * mult))
        return jnp.concatenate(parts, axis=-1)

    q = _dot(h, w_ref[:, 0:D_MODEL])
    q_ref[...] = head_norm(q, qg_ref[...], NA_HEAD_DIM ** -0.5 * LOG2E).astype(BF16)
    k = _dot(h, w_ref[:, D_MODEL:2 * D_MODEL])
    k_ref[...] = head_norm(k, kg_ref[...], 1.0).astype(BF16)
    v_ref[...] = _dot(h, w_ref[:, 2 * D_MODEL:3 * D_MODEL]).astype(BF16)


def _na_proj(x, g, w, qg, kg):
    m = x.shape[0]
    tm = min(TOKEN_TILE, m)
    blk = np.arange(LANES) // NA_HEAD_DIM
    bd = jnp.asarray(blk[:, None] == blk[None, :], BF16)
    tile = lambda: pl.BlockSpec((tm, D_MODEL), lambda i: (i, 0))
    return pl.pallas_call(
        _na_proj_kernel,
        grid=(m // tm,),
        in_specs=[tile(), _resident((1, D_MODEL)), _resident(w.shape),
                  _resident((1, LANES)), _resident((1, LANES)), _resident((LANES, LANES))],
        out_specs=[tile(), tile(), tile()],
        out_shape=[jax.ShapeDtypeStruct((m, D_MODEL), BF16)] * 3,
        compiler_params=_params("parallel"),
        name="na_proj",
    )(x, g, w, qg, kg, bd)


def _na_attn_kernel(rows, q_ref, k_ref, v_ref, bias_ref, cm_ref, o_ref):
    win = NA_WIN_R * GRID_W
    lane = lax.broadcasted_iota(jnp.int32, (GRID_W, LANES), 1)
    first = lane < NA_HEAD_DIM
    head_sel = (jnp.where(first[0:1], 1.0, 0.0).astype(BF16), jnp.where(first[0:1], 0.0, 1.0).astype(BF16))
    valid = cm_ref[...] > 0.0

    def body(r, _):
        r0 = jnp.clip(r - NA_WIN_R // 2, 0, rows - NA_WIN_R)
        dr = r - r0
        qs = pl.multiple_of(r * GRID_W, GRID_W)
        ks = pl.multiple_of(r0 * GRID_W, GRID_W)
        q = q_ref[0, pl.ds(qs, GRID_W), :]
        kb = k_ref[0, pl.ds(ks, win), :]
        vb = v_ref[0, pl.ds(ks, win), :]
        outs = []
        for hh in range(2):
            s = _dot_nt(q * head_sel[hh], kb) + bias_ref[hh, dr]
            s = jnp.where(valid, s, NEG_INF)
            mx = jnp.max(s, axis=-1, keepdims=True)
            p = jnp.exp2(s - mx)
            den = jnp.sum(p, axis=-1, keepdims=True)
            outs.append(_dot(p.astype(BF16), vb) * (1.0 / den))
        o_ref[0, pl.ds(qs, GRID_W), :] = jnp.where(first, outs[0], outs[1]).astype(o_ref.dtype)
        return 0

    lax.fori_loop(0, rows, body, 0)


def _na_attn(q, k, v, bias, cm):
    b, t, _ = q.shape
    rows = t // GRID_W
    seq = lambda: pl.BlockSpec((1, t, LANES), lambda i, j: (i, 0, j))
    return pl.pallas_call(
        functools.partial(_na_attn_kernel, rows),
        grid=(b, D_MODEL // LANES),
        in_specs=[seq(), seq(), seq(),
                  pl.BlockSpec((2, NA_WIN_R, GRID_W, NA_WIN_R * GRID_W), lambda i, j: (j, 0, 0, 0)),
                  _resident(cm.shape)],
        out_specs=seq(),
        out_shape=jax.ShapeDtypeStruct((b, t, D_MODEL), BF16),
        compiler_params=_params("parallel", "parallel"),
        name="na_attn",
    )(q, k, v, bias, cm)


def _na_tables(rpb, rows):
    cols = np.arange(GRID_W)
    col_start = np.clip(cols - NA_WIN_C // 2, 0, GRID_W - NA_WIN_C)
    col_mask = (cols[None, :] >= col_start[:, None]) & (cols[None, :] < col_start[:, None] + NA_WIN_C)
    col_off = np.clip(cols[None, :] - cols[:, None] + NA_WIN_C - 1, 0, 2 * NA_WIN_C - 2)
    win_r = min(NA_WIN_R, rows)
    dr = np.arange(win_r)[:, None]
    j = np.arange(win_r)[None, :]
    row_off = j - dr + NA_WIN_R - 1
    tab = rpb.astype(F32)[:, row_off][:, :, :, col_off]
    tab = tab.transpose(0, 1, 3, 2, 4).reshape(NA_HEADS, win_r, GRID_W, win_r * GRID_W) * LOG2E
    cm = np.tile(col_mask, (1, win_r)).astype(np.float32)
    return tab, jnp.asarray(cm)


def _na_mixer(x, mix_g, w_in, q_norm, k_norm, rpb, w_out):
    b, t, _ = x.shape
    m = b * t
    x2 = x.reshape(m, D_MODEL)
    tile2 = lambda g: jnp.tile(g.astype(F32), LANES // NA_HEAD_DIM).reshape(1, LANES)
    q, k, v = _na_proj(x2, mix_g, w_in, tile2(q_norm), tile2(k_norm))
    bias, cm = _na_tables(rpb, t // GRID_W)
    r3 = lambda a: a.reshape(b, t, D_MODEL)
    o = _na_attn(r3(q), r3(k), r3(v), bias, cm)
    return _out_proj(o.reshape(m, D_MODEL), x2, w_out).reshape(b, t, D_MODEL)


_MLA_PERM = np.concatenate([
    MLA_NOPE + np.arange(16),
    np.arange(48),
    MLA_NOPE + 16 + np.arange(16),
    48 + np.arange(16),
    -np.ones(32, np.int64),
]).astype(np.int64)


def _mla_prep_kernel(x_ref, g_ref, win_ref, qa_ref, kva_ref, wuq_ref, wuk_ref, wuv_ref,
                     qn_ref, kn_ref, cos_ref, sin_ref, q_ref, k_ref, v_ref):
    h = _rms(x_ref[...], g_ref[...]).astype(BF16)
    c = _dot(h, win_ref[...])
    cq = _rms(c[:, :MLA_Q_LORA], qa_ref[...]).astype(BF16)
    ckv = _rms(c[:, MLA_Q_LORA:MLA_Q_LORA + MLA_KV_LORA], kva_ref[...]).astype(BF16)
    k_rope = c[:, MLA_Q_LORA + MLA_KV_LORA:]
    cos = cos_ref[...]
    sin = sin_ref[...]

    def finish(y, gain, mult):
        ms = jnp.sum(y * y, axis=-1, keepdims=True) * (1.0 / MLA_QK)
        y = y * lax.rsqrt(ms + NORM_EPS) * gain
        y = y * cos + pltpu.roll(y, LANES // 2, 1) * sin
        return (y * mult).astype(BF16)

    scale = MLA_QK ** -0.5 * LOG2E
    for hd in range(MLA_HEADS):
        sl = slice(hd * LANES, (hd + 1) * LANES)
        q_ref[:, sl] = finish(_dot(cq, wuq_ref[:, sl]), qn_ref[...], scale)
        k_ref[:, sl] = finish(_dot(ckv, wuk_ref[:, sl]) + k_rope, kn_ref[...], 1.0)
    v_ref[...] = _dot(ckv, wuv_ref[...]).astype(BF16)


def _mla_prep(x, g, win, qa, kva, wuq, wuk, wuv, qn, kn, cos, sin, t):
    m = x.shape[0]
    tm = min(TOKEN_TILE, t)
    per_seq = t // tm
    wide = MLA_HEADS * LANES
    return pl.pallas_call(
        _mla_prep_kernel,
        grid=(m // tm,),
        in_specs=[pl.BlockSpec((tm, D_MODEL), lambda i: (i, 0)),
                  _resident((1, D_MODEL)), _resident(win.shape),
                  _resident((1, MLA_Q_LORA)), _resident((1, MLA_KV_LORA)),
                  _resident(wuq.shape), _resident(wuk.shape), _resident(wuv.shape),
                  _resident((1, LANES)), _resident((1, LANES)),
                  pl.BlockSpec((tm, LANES), lambda i: (i % per_seq, 0)),
                  pl.BlockSpec((tm, LANES), lambda i: (i % per_seq, 0))],
        out_specs=[pl.BlockSpec((tm, wide), lambda i: (i, 0)),
                   pl.BlockSpec((tm, wide), lambda i: (i, 0)),
                   pl.BlockSpec((tm, D_MODEL), lambda i: (i, 0))],
        out_shape=[jax.ShapeDtypeStruct((m, wide), BF16),
                   jax.ShapeDtypeStruct((m, wide), BF16),
                   jax.ShapeDtypeStruct((m, D_MODEL), BF16)],
        compiler_params=_params("parallel"),
        name="mla_prep",
    )(x, g, win, qa, kva, wuq, wuk, wuv, qn, kn, cos, sin)


def _mla_attn_kernel(q_ref, k_ref, v_ref, o_ref):
    lane = lax.broadcasted_iota(jnp.int32, (q_ref.shape[1], LANES), 1)
    first = lane < MLA_V
    v = v_ref[0]
    outs = []
    for hh in range(2):
        sl = slice(hh * LANES, (hh + 1) * LANES)
        s = _dot_nt(q_ref[0, :, sl], k_ref[0, :, sl])
        mx = jnp.max(s, axis=-1, keepdims=True)
        p = jnp.exp2(s - mx)
        den = jnp.sum(p, axis=-1, keepdims=True)
        outs.append(_dot(p.astype(BF16), v) * (1.0 / den))
    o_ref[0] = jnp.where(first, outs[0], outs[1]).astype(o_ref.dtype)


def _mla_attn(q, k, v):
    b, t, _ = q.shape
    tq = min(MLA_Q_TILE, t)
    return pl.pallas_call(
        _mla_attn_kernel,
        grid=(b, MLA_HEADS // 2, t // tq),
        in_specs=[pl.BlockSpec((1, tq, 2 * LANES), lambda i, j, l: (i, l, j)),
                  pl.BlockSpec((1, t, 2 * LANES), lambda i, j, l: (i, 0, j)),
                  pl.BlockSpec((1, t, LANES), lambda i, j, l: (i, 0, j))],
        out_specs=pl.BlockSpec((1, tq, LANES), lambda i, j, l: (i, l, j)),
        out_shape=jax.ShapeDtypeStruct((b, t, D_MODEL), BF16),
        compiler_params=_params("parallel", "parallel", "arbitrary"),
        name="mla_attn",
    )(q, k, v)


def _mla_weights(w_in, w_uq, w_ukv, q_norm, k_norm, t):
    valid = _MLA_PERM >= 0
    src = np.where(valid, _MLA_PERM, 0)
    is_rope = valid & (_MLA_PERM >= MLA_NOPE)
    is_nope = valid & (_MLA_PERM < MLA_NOPE)

    def place(w, per_head, lanes_ok, base=0):
        wh = w.reshape(w.shape[0], MLA_HEADS, per_head)[:, :, base + np.where(lanes_ok, src, 0)]
        return jnp.where(lanes_ok[None, None, :], wh, 0.0).reshape(w.shape[0], MLA_HEADS * LANES)

    wuq = place(w_uq, MLA_QK, valid)
    wuk = place(w_ukv, MLA_NOPE + MLA_V, is_nope)
    wuv = w_ukv.reshape(MLA_KV_LORA, MLA_HEADS, MLA_NOPE + MLA_V)[:, :, MLA_NOPE:].reshape(
        MLA_KV_LORA, MLA_HEADS * MLA_V)
    rope_cols = w_in[:, MLA_Q_LORA + MLA_KV_LORA:]
    rope_placed = jnp.where(is_rope[None, :], rope_cols[:, np.where(is_rope, src - MLA_NOPE, 0)], 0.0)
    win = jnp.concatenate([w_in[:, :MLA_Q_LORA + MLA_KV_LORA], rope_placed], axis=1)
    gain = lambda g: jnp.where(valid, g.astype(F32)[src], 0.0).reshape(1, LANES)

    half = MLA_ROPE // 2
    inv_freq = ROPE_THETA ** (-jnp.arange(half, dtype=F32) / half)
    ang = jnp.arange(t).astype(F32)[:, None] * inv_freq[None, :]
    cos = jnp.ones((t, LANES), F32)
    cos = cos.at[:, 0:half].set(jnp.cos(ang)).at[:, 64:64 + half].set(jnp.cos(ang))
    sin = jnp.zeros((t, LANES), F32)
    sin = sin.at[:, 0:half].set(-jnp.sin(ang)).at[:, 64:64 + half].set(jnp.sin(ang))
    return (win.astype(BF16), wuq.astype(BF16), wuk.astype(BF16), wuv.astype(BF16),
            gain(q_norm), gain(k_norm), cos, sin)


def _mla_mixer(x, mix_g, w_in, q_a_norm, w_uq, kv_a_norm, w_ukv, q_norm, k_norm, w_out):
    b, t, _ = x.shape
    m = b * t
    x2 = x.reshape(m, D_MODEL)
    win, wuq, wuk, wuv, qn, kn, cos, sin = _mla_weights(w_in, w_uq, w_ukv, q_norm, k_norm, t)
    q, k, v = _mla_prep(x2, mix_g, win, q_a_norm.astype(F32).reshape(1, -1),
                        kv_a_norm.astype(F32).reshape(1, -1), wuq, wuk, wuv, qn, kn, cos, sin, t)
    o = _mla_attn(q.reshape(b, t, -1), k.reshape(b, t, -1), v.reshape(b, t, -1))
    return _out_proj(o.reshape(m, D_MODEL), x2, w_out).reshape(b, t, D_MODEL)


def kernel(x, ffn1_norm, ffn1_w_gu, ffn1_w_down, mix_norm, ffn2_norm, ffn2_w_gu, ffn2_w_down,
           hg_lb_logits, hg_w_in, hg_g_norm, hg_w_out,
           na_w_in, na_q_norm, na_k_norm, na_rpb, na_w_out,
           mla_w_in, mla_q_a_norm, mla_w_uq, mla_kv_a_norm, mla_w_ukv, mla_q_norm, mla_k_norm, mla_w_out):
    b, t, d = x.shape
    m = b * t
    gam = jnp.cumsum(jax.nn.softmax(hg_lb_logits.astype(F32), axis=0), axis=0)
    lb_all = gam - gam[0:1]
    row = lambda g: g.astype(F32).reshape(1, -1)
    bf = lambda w: w.astype(BF16)
    ia = ib = ic = 0
    for layer in range(DEPTH):
        x = _ffn(x.reshape(m, d), row(ffn1_norm[layer]), bf(ffn1_w_gu[layer]),
                 bf(ffn1_w_down[layer])).reshape(b, t, d)
        g = row(mix_norm[layer])
        kind = layer % N_MIXERS
        if kind == 0:
            x = _hgrn_mixer(x, g, bf(hg_w_in[ia]), hg_g_norm[ia].astype(F32), bf(hg_w_out[ia]), lb_all[layer])
            ia += 1
        elif kind == 1:
            x = _na_mixer(x, g, bf(na_w_in[ib]), na_q_norm[ib], na_k_norm[ib], na_rpb[ib], bf(na_w_out[ib]))
            ib += 1
        else:
            x = _mla_mixer(x, g, mla_w_in[ic], mla_q_a_norm[ic], mla_w_uq[ic], mla_kv_a_norm[ic],
                           mla_w_ukv[ic], mla_q_norm[ic], mla_k_norm[ic], bf(mla_w_out[ic]))
            ic += 1
        x = _ffn(x.reshape(m, d), row(ffn2_norm[layer]), bf(ffn2_w_gu[layer]),
                 bf(ffn2_w_down[layer])).reshape(b, t, d)
    return x
```

```python
import functools

import numpy as np
import jax
import jax.numpy as jnp
from jax import lax
from jax.experimental import pallas as pl
from jax.experimental.pallas import tpu as pltpu

D_MODEL = 1024
DEPTH = 4
N_MIXERS = 3
GRID_W = 64
D_FF = 2816
NORM_EPS = 1e-6
HG_HEADS = 8
HG_DK = 128
HG_DV = 128
NA_HEADS = 16
NA_HEAD_DIM = 64
NA_WIN_R = 8
NA_WIN_C = 16
MLA_HEADS = 16
MLA_Q_LORA = 768
MLA_KV_LORA = 256
MLA_NOPE = 64
MLA_ROPE = 32
MLA_V = 64
MLA_QK = MLA_NOPE + MLA_ROPE
ROPE_THETA = 10000.0
NEG_INF = -1e30
LOG2E = 1.4426950408889634

LANES = 128
VMEM_LIMIT = 56 * 1024 * 1024
TOKEN_TILE = 512
HG_CHUNK = 128
HG_LEVELS = (64, 32, 16, 8, 4, 2, 1)
HG_SUBLANES = 8
HG_HEADS_PER_STEP = 2
MLA_Q_TILE = 256
NA_GROUP = 4
NA_UNION = NA_WIN_R + NA_GROUP
F32 = jnp.float32
BF16 = jnp.bfloat16


def _params(*sem):
    return pltpu.CompilerParams(dimension_semantics=sem, vmem_limit_bytes=VMEM_LIMIT)


def _resident(shape):
    nd = len(shape)
    return pl.BlockSpec(shape, lambda *_: (0,) * nd, pipeline_mode=pl.Buffered(1))


def _rms(x, g):
    ms = jnp.mean(x * x, axis=-1, keepdims=True)
    return x * lax.rsqrt(ms + NORM_EPS) * g


def _silu(x):
    return x * (1.0 / (1.0 + jnp.exp(-x)))


def _dot(a, b):
    return jnp.dot(a, b, preferred_element_type=F32)


def _dot_nt(a, b):
    return lax.dot_general(a, b, (((1,), (1,)), ((), ())), preferred_element_type=F32)


def _dot_tn(a, b):
    return lax.dot_general(a, b, (((0,), (0,)), ((), ())), preferred_element_type=F32)


FF_CHUNKS = ((0, 1536), (1536, 2816))


def _ffn_body(x, g, wgu_ref, wd_ref):
    h = _rms(x, g).astype(BF16)
    acc = None
    for s, e in FF_CHUNKS:
        gate = _dot(h, wgu_ref[:, s:e])
        up = _dot(h, wgu_ref[:, D_FF + s:D_FF + e])
        a = (_silu(gate) * up).astype(BF16)
        d = _dot(a, wd_ref[s:e, :])
        acc = d if acc is None else acc + d
    return x + 0.5 * acc


def _ffn_kernel(x_ref, g_ref, wgu_ref, wd_ref, o_ref):
    o_ref[...] = _ffn_body(x_ref[...], g_ref[...], wgu_ref, wd_ref)


def _ffn(x, g, wgu, wd):
    m = x.shape[0]
    tm = min(TOKEN_TILE, m)
    return pl.pallas_call(
        _ffn_kernel,
        grid=(m // tm,),
        in_specs=[pl.BlockSpec((tm, D_MODEL), lambda i: (i, 0)),
                  _resident((1, D_MODEL)),
                  _resident(wgu.shape),
                  _resident(wd.shape)],
        out_specs=pl.BlockSpec((tm, D_MODEL), lambda i: (i, 0)),
        out_shape=jax.ShapeDtypeStruct((m, D_MODEL), F32),
        compiler_params=_params("parallel"),
        name="ffn",
    )(x, g, wgu, wd)


def _out_proj_kernel(a_ref, x_ref, w_ref, o_ref):
    o_ref[...] = x_ref[...] + _dot(a_ref[...], w_ref[...])


def _out_proj(a, x, w):
    m, k = a.shape
    tm = min(TOKEN_TILE, m)
    return pl.pallas_call(
        _out_proj_kernel,
        grid=(m // tm,),
        in_specs=[pl.BlockSpec((tm, k), lambda i: (i, 0)),
                  pl.BlockSpec((tm, D_MODEL), lambda i: (i, 0)),
                  _resident(w.shape)],
        out_specs=pl.BlockSpec((tm, D_MODEL), lambda i: (i, 0)),
        out_shape=jax.ShapeDtypeStruct((m, D_MODEL), F32),
        compiler_params=_params("parallel"),
        name="out_proj",
    )(a, x, w)


def _hgrn_q_row(direction, hb, t):
    return ((t % (2 * hb)) >= hb) != (direction == 1)


def _hgrn_ref_row(direction, hb, t):
    return (t // (2 * hb)) * (2 * hb) + (hb - 1 if direction == 0 else hb)


def _hgrn_tables():
    c = HG_CHUNK
    t = np.arange(c)[:, None]
    s = np.arange(c)[None, :]
    tri = np.stack([s <= t, s >= t]).astype(np.float32)
    wide, full = [], []
    for d in range(2):
        wide_d, full_d = [], []
        for hb in HG_LEVELS:
            same = (t // (2 * hb)) == (s // (2 * hb))
            own = (_hgrn_q_row(d, hb, t) & same & ~_hgrn_q_row(d, hb, s)).astype(np.float32)
            if hb >= HG_SUBLANES:
                wide_d.append(own[_hgrn_q_row(d, hb, np.arange(c))])
            else:
                full_d.append(own)
        wide.append(np.stack(wide_d))
        full.append(np.stack(full_d))
    return tri, np.stack(wide), np.stack(full)


def _hgrn_proj_kernel(x_ref, g_ref, w_ref, llb_ref, l1m_ref, q_ref, lf_ref, lb_ref, v_ref, gate_ref):
    h = _rms(x_ref[...], g_ref[...]).astype(BF16)
    proj = lambda j: _dot(h, w_ref[:, j * D_MODEL:(j + 1) * D_MODEL])
    llb = llb_ref[...]
    l1m = l1m_ref[...]

    def log2_decay(z):
        ls = jnp.minimum(z, 0.0) - jnp.log(1.0 + jnp.exp2(jnp.abs(z) * -LOG2E))
        y = l1m + ls
        return (jnp.maximum(llb, y) + jnp.log(1.0 + jnp.exp2(jnp.abs(llb - y) * -LOG2E))) * LOG2E

    q_ref[...] = (proj(0) * HG_DK ** -0.5).astype(BF16)
    lf_ref[...] = log2_decay(proj(1))
    lb_ref[...] = log2_decay(proj(2))
    v_ref[...] = proj(3).astype(BF16)
    gate_ref[...] = proj(4)


def _hgrn_proj(x, g, w, llb, l1m):
    m = x.shape[0]
    tm = min(TOKEN_TILE, m)
    tile = lambda: pl.BlockSpec((tm, D_MODEL), lambda i: (i, 0))
    return pl.pallas_call(
        _hgrn_proj_kernel,
        grid=(m // tm,),
        in_specs=[tile(), _resident((1, D_MODEL)), _resident(w.shape),
                  _resident((1, D_MODEL)), _resident((1, D_MODEL))],
        out_specs=[tile()] * 5,
        out_shape=[jax.ShapeDtypeStruct((m, D_MODEL), dt) for dt in (BF16, F32, F32, BF16, F32)],
        compiler_params=_params("parallel"),
        name="hgrn_proj",
    )(x, g, w, llb, l1m)


def _hgrn_chunk(direction, q32, v, logf2, tri_ref, wide_ref, full_ref, cum_ref, st_ref):
    c = HG_CHUNK
    nv = c // HG_SUBLANES
    rows = lambda a, i: a[i * HG_SUBLANES:(i + 1) * HG_SUBLANES]
    f = jnp.exp2(logf2)
    k = 1.0 - f
    hi = logf2.astype(BF16)
    lo = (logf2 - hi.astype(F32)).astype(BF16)
    tri = tri_ref[direction]
    cum = _dot(tri, hi) + _dot(tri, lo)
    cum_ref[direction] = cum
    ref_row = lambda r: cum_ref[direction, r:r + 1, :]

    heads = HG_HEADS_PER_STEP
    hsl = [slice(h * HG_DK, (h + 1) * HG_DK) for h in range(heads)]
    att = [[None] * nv for _ in range(heads)]

    def add_rows(h, i, piece):
        att[h][i] = piece if att[h][i] is None else att[h][i] + piece

    sub = lax.broadcasted_iota(jnp.int32, (c, q32.shape[1]), 0)
    sub8 = lax.broadcasted_iota(jnp.int32, (HG_SUBLANES, q32.shape[1]), 0)
    n_wide = 0
    for li, hb in enumerate(HG_LEVELS):
        if hb >= HG_SUBLANES:
            parts, q_idx = [], []
            for i in range(nv):
                t0 = i * HG_SUBLANES
                ref = ref_row(int(_hgrn_ref_row(direction, hb, t0)))
                if bool(_hgrn_q_row(direction, hb, t0)):
                    parts.append(rows(q32, i) * jnp.exp2(rows(cum, i) - ref))
                    q_idx.append(i)
                else:
                    parts.append(rows(k, i) * jnp.exp2(ref - rows(cum, i)))
            w = jnp.concatenate(parts, axis=0).astype(BF16)
            wq = jnp.concatenate([parts[i] for i in q_idx], axis=0).astype(BF16)
            for h in range(heads):
                a = _dot_nt(wq[:, hsl[h]], w[:, hsl[h]]) * wide_ref[direction, n_wide]
                for j, i in enumerate(q_idx):
                    add_rows(h, i, rows(a, j))
            n_wide += 1
        else:
            q_rows = _hgrn_q_row(direction, hb, sub)
            if hb == 1:
                w = jnp.where(q_rows, q32 * f, k)
            else:
                pieces = []
                for i in range(nv):
                    t0 = i * HG_SUBLANES
                    if 2 * hb == HG_SUBLANES:
                        pieces.append(jnp.broadcast_to(ref_row(int(_hgrn_ref_row(direction, hb, t0))),
                                                       (HG_SUBLANES, q32.shape[1])))
                    else:
                        lo_ref = ref_row(int(_hgrn_ref_row(direction, hb, t0)))
                        hi_ref = ref_row(int(_hgrn_ref_row(direction, hb, t0 + 2 * hb)))
                        pieces.append(jnp.where(sub8 < 2 * hb, lo_ref, hi_ref))
                d = cum - jnp.concatenate(pieces, axis=0)
                w = jnp.where(q_rows, q32, k) * jnp.exp2(jnp.where(q_rows, d, -d))
            w = w.astype(BF16)
            for h in range(heads):
                a = _dot_nt(w[:, hsl[h]], w[:, hsl[h]]) * full_ref[direction, li - n_wide]
                for i in range(nv):
                    add_rows(h, i, rows(a, i))

    last = c - 1 if direction == 0 else 0
    total = ref_row(last)
    qi = (q32 * jnp.exp2(cum)).astype(BF16)
    ki = (k * jnp.exp2(total - cum)).astype(BF16)
    diag_in = (q32 * k).astype(BF16)
    ones = jnp.ones((HG_DK, HG_DV), BF16)
    decay = jnp.exp2(total)
    outs = []
    for h in range(heads):
        vh = v[:, hsl[h]]
        st = st_ref[direction, h]
        o = _dot(jnp.concatenate(att[h], axis=0).astype(BF16), vh)
        o = o + _dot(diag_in[:, hsl[h]], ones) * vh.astype(F32)
        o = o + _dot_nt(qi[:, hsl[h]], st.astype(BF16))
        st_ref[direction, h] = st * decay[:, hsl[h]] + _dot_tn(vh, ki[:, hsl[h]])
        outs.append(o)
    return jnp.concatenate(outs, axis=-1)


def _hgrn_kernel(q_ref, lf_ref, lb_ref, v_ref, tri_ref, wide_ref, full_ref,
                 o_ref, ob_ref, cum_ref, st_ref):
    t = q_ref.shape[1]
    c = HG_CHUNK
    n = t // c
    st_ref[...] = jnp.zeros_like(st_ref)

    def body(i, carry):
        rf = pl.multiple_of(i * c, c)
        rb = pl.multiple_of((n - 1 - i) * c, c)
        o_ref[0, pl.ds(rf, c), :] = _hgrn_chunk(
            0, q_ref[0, pl.ds(rf, c), :].astype(F32), v_ref[0, pl.ds(rf, c), :],
            lf_ref[0, pl.ds(rf, c), :], tri_ref, wide_ref, full_ref, cum_ref, st_ref)
        ob_ref[pl.ds(rb, c), :] = _hgrn_chunk(
            1, q_ref[0, pl.ds(rb, c), :].astype(F32), v_ref[0, pl.ds(rb, c), :],
            lb_ref[0, pl.ds(rb, c), :], tri_ref, wide_ref, full_ref, cum_ref, st_ref)
        return carry

    lax.fori_loop(0, n, body, 0)
    o_ref[0] = o_ref[0] + ob_ref[...]


def _hgrn_scan(q, lf, lb, v):
    b, t, _ = q.shape
    tri, wide, full = _hgrn_tables()
    tri = jnp.asarray(tri, BF16)
    wide = jnp.asarray(wide, F32)
    full = jnp.asarray(full, F32)
    width = HG_HEADS_PER_STEP * HG_DK
    seq = lambda: pl.BlockSpec((1, t, width), lambda i, j: (i, 0, j))
    return pl.pallas_call(
        _hgrn_kernel,
        grid=(b, HG_HEADS // HG_HEADS_PER_STEP),
        in_specs=[seq(), seq(), seq(), seq(),
                  _resident(tri.shape), _resident(wide.shape), _resident(full.shape)],
        out_specs=seq(),
        out_shape=jax.ShapeDtypeStruct((b, t, D_MODEL), F32),
        scratch_shapes=[pltpu.VMEM((t, width), F32),
                        pltpu.VMEM((2, HG_CHUNK, width), F32),
                        pltpu.VMEM((2, HG_HEADS_PER_STEP, HG_DV, HG_DK), F32)],
        compiler_params=_params("parallel", "parallel"),
        name="hgrn_scan",
    )(q, lf, lb, v, tri, wide, full)


def _hgrn_out_kernel(o_ref, gate_ref, x_ref, gn_ref, w_ref, y_ref):
    gn = gn_ref[...]
    parts = []
    for h in range(HG_HEADS):
        sl = slice(h * HG_DV, (h + 1) * HG_DV)
        parts.append(_rms(o_ref[:, sl], gn) * _silu(gate_ref[:, sl].astype(F32)))
    a = jnp.concatenate(parts, axis=-1).astype(BF16)
    y_ref[...] = x_ref[...] + _dot(a, w_ref[...])


def _hgrn_out(o, gate, x, gn, w):
    m = x.shape[0]
    tm = min(TOKEN_TILE, m)
    tile = lambda: pl.BlockSpec((tm, D_MODEL), lambda i: (i, 0))
    return pl.pallas_call(
        _hgrn_out_kernel,
        grid=(m // tm,),
        in_specs=[tile(), tile(), tile(), _resident((1, HG_DV)), _resident(w.shape)],
        out_specs=tile(),
        out_shape=jax.ShapeDtypeStruct((m, D_MODEL), F32),
        compiler_params=_params("parallel"),
        name="hgrn_out",
    )(o, gate, x, gn, w)


def _hgrn_mixer(x, mix_g, w_in, g_norm, w_out, lb):
    b, t, _ = x.shape
    m = b * t
    x2 = x.reshape(m, D_MODEL)
    lb = lb.astype(F32).reshape(1, D_MODEL)
    q, lf, lbw, v, gate = _hgrn_proj(x2, mix_g, w_in, jnp.log(lb), jnp.log1p(-lb))
    r3 = lambda a: a.reshape(b, t, D_MODEL)
    o = _hgrn_scan(r3(q), r3(lf), r3(lbw), r3(v))
    y = _hgrn_out(o.reshape(m, D_MODEL), gate, x2, g_norm.reshape(1, HG_DV), w_out)
    return y.reshape(b, t, D_MODEL)


def _na_proj_kernel(x_ref, g_ref, w_ref, qg_ref, kg_ref, bd_ref, q_ref, k_ref, v_ref):
    h = _rms(x_ref[...], g_ref[...]).astype(BF16)
    bd = bd_ref[...]

    def head_norm(y, gain, mult):
        parts = []
        for j in range(D_MODEL // LANES):
            ys = y[:, j * LANES:(j + 1) * LANES]
            sq = ys * ys
            hi = sq.astype(BF16)
            lo = (sq - hi.astype(F32)).astype(BF16)
            ss = _dot(hi, bd) + _dot(lo, bd)
            parts.append(ys * lax.rsqrt(ss * (1.0 / NA_HEAD_DIM) + NORM_EPS) * (gain * mult))
        return jnp.concatenate(parts, axis=-1)

    q = _dot(h, w_ref[:, 0:D_MODEL])
    q_ref[...] = head_norm(q, qg_ref[...], NA_HEAD_DIM ** -0.5 * LOG2E).astype(BF16)
    k = _dot(h, w_ref[:, D_MODEL:2 * D_MODEL])
    k_ref[...] = head_norm(k, kg_ref[...], 1.0).astype(BF16)
    v_ref[...] = _dot(h, w_ref[:, 2 * D_MODEL:3 * D_MODEL]).astype(BF16)


def _na_proj(x, g, w, qg, kg):
    m = x.shape[0]
    tm = min(TOKEN_TILE, m)
    blk = np.arange(LANES) // NA_HEAD_DIM
    bd = jnp.asarray(blk[:, None] == blk[None, :], BF16)
    tile = lambda: pl.BlockSpec((tm, D_MODEL), lambda i: (i, 0))
    return pl.pallas_call(
        _na_proj_kernel,
        grid=(m // tm,),
        in_specs=[tile(), _resident((1, D_MODEL)), _resident(w.shape),
                  _resident((1, LANES)), _resident((1, LANES)), _resident((LANES, LANES))],
        out_specs=[tile(), tile(), tile()],
        out_shape=[jax.ShapeDtypeStruct((m, D_MODEL), BF16)] * 3,
        compiler_params=_params("parallel"),
        name="na_proj",
    )(x, g, w, qg, kg, bd)


def _na_group_start(g, rows):
    return jnp.clip(g * NA_GROUP - NA_WIN_R // 2, 0, rows - NA_UNION) if isinstance(g, jax.Array) else \
        int(np.clip(g * NA_GROUP - NA_WIN_R // 2, 0, rows - NA_UNION))


def _na_attn_kernel(rows, q_ref, k_ref, v_ref, bias_ref, cm_ref, o_ref):
    gq = NA_GROUP * GRID_W
    uk = NA_UNION * GRID_W
    n_groups = rows // NA_GROUP
    lane = lax.broadcasted_iota(jnp.int32, (gq, LANES), 1)
    first = lane < NA_HEAD_DIM
    head_sel = (jnp.where(first[0:1], 1.0, 0.0).astype(BF16), jnp.where(first[0:1], 0.0, 1.0).astype(BF16))

    def body(g, carry):
        kind = jnp.where(g == 0, 0, jnp.where(g == n_groups - 1, 2, 1))
        qs = pl.multiple_of(g * gq, gq)
        ks = pl.multiple_of(_na_group_start(g, rows) * GRID_W, NA_GROUP * GRID_W)
        q = q_ref[0, pl.ds(qs, gq), :]
        kb = k_ref[0, pl.ds(ks, uk), :]
        vb = v_ref[0, pl.ds(ks, uk), :]
        s = _dot_nt(jnp.concatenate([q * head_sel[0], q * head_sel[1]], axis=0), kb)
        cm = cm_ref[kind]
        probs, dens = [], []
        for hh in range(2):
            sh = s[hh * gq:(hh + 1) * gq] * cm + bias_ref[hh, kind]
            p = jnp.exp2(sh - jnp.max(sh, axis=-1, keepdims=True))
            dens.append(jnp.sum(p, axis=-1, keepdims=True))
            probs.append(p.astype(BF16))
        o2 = _dot(jnp.concatenate(probs, axis=0), vb)
        o = jnp.where(first, o2[:gq] * (1.0 / dens[0]), o2[gq:] * (1.0 / dens[1]))
        o_ref[0, pl.ds(qs, gq), :] = o.astype(o_ref.dtype)
        return carry

    lax.fori_loop(0, n_groups, body, 0)


def _na_attn(q, k, v, bias, cm):
    b, t, _ = q.shape
    rows = t // GRID_W
    seq = lambda: pl.BlockSpec((1, t, LANES), lambda j, i: (i, 0, j))
    return pl.pallas_call(
        functools.partial(_na_attn_kernel, rows),
        grid=(D_MODEL // LANES, b),
        in_specs=[seq(), seq(), seq(),
                  pl.BlockSpec((2,) + bias.shape[1:], lambda j, i: (j, 0, 0, 0)),
                  _resident(cm.shape)],
        out_specs=seq(),
        out_shape=jax.ShapeDtypeStruct((b, t, D_MODEL), BF16),
        compiler_params=_params("parallel", "parallel"),
        name="na_attn",
    )(q, k, v, bias, cm)


def _na_tables(rpb, rows):
    assert rows % NA_GROUP == 0 and rows >= NA_UNION + NA_GROUP
    cols = np.arange(GRID_W)
    col_start = np.clip(cols - NA_WIN_C // 2, 0, GRID_W - NA_WIN_C)
    col_mask = (cols[None, :] >= col_start[:, None]) & (cols[None, :] < col_start[:, None] + NA_WIN_C)
    col_off = np.clip(cols[None, :] - cols[:, None] + NA_WIN_C - 1, 0, 2 * NA_WIN_C - 2)
    n_groups = rows // NA_GROUP
    valid, row_off = [], []
    for g in (0, 1, n_groups - 1):
        r = g * NA_GROUP + np.arange(NA_GROUP)[:, None]
        key = _na_group_start(g, rows) + np.arange(NA_UNION)[None, :]
        r0 = np.clip(r - NA_WIN_R // 2, 0, rows - NA_WIN_R)
        valid.append((key >= r0) & (key < r0 + NA_WIN_R))
        row_off.append(np.clip(key - r + NA_WIN_R - 1, 0, 2 * NA_WIN_R - 2))
    valid = np.stack(valid)
    assert (valid.sum(-1) == NA_WIN_R).all()
    row_off = np.stack(row_off)
    full_valid = valid[:, :, None, :, None] & col_mask[None, None, :, None, :]
    shape = (3, NA_GROUP * GRID_W, NA_UNION * GRID_W)
    tab = rpb.astype(F32)[:, row_off][:, :, :, :, col_off]
    tab = tab.transpose(0, 1, 2, 4, 3, 5) * LOG2E
    tab = jnp.where(full_valid[None], tab, NEG_INF * LOG2E).reshape((NA_HEADS,) + shape)
    return tab, jnp.asarray(full_valid.reshape(shape), F32)


def _na_mixer(x, mix_g, w_in, q_norm, k_norm, rpb, w_out):
    b, t, _ = x.shape
    m = b * t
    x2 = x.reshape(m, D_MODEL)
    tile2 = lambda g: jnp.tile(g.astype(F32), LANES // NA_HEAD_DIM).reshape(1, LANES)
    q, k, v = _na_proj(x2, mix_g, w_in, tile2(q_norm), tile2(k_norm))
    bias, cm = _na_tables(rpb, t // GRID_W)
    r3 = lambda a: a.reshape(b, t, D_MODEL)
    o = _na_attn(r3(q), r3(k), r3(v), bias, cm)
    return _out_proj(o.reshape(m, D_MODEL), x2, w_out).reshape(b, t, D_MODEL)


_MLA_PERM = np.concatenate([
    MLA_NOPE + np.arange(16),
    np.arange(48),
    MLA_NOPE + 16 + np.arange(16),
    48 + np.arange(16),
    -np.ones(32, np.int64),
]).astype(np.int64)


def _mla_prep_kernel(x_ref, g_ref, win_ref, qa_ref, kva_ref, wuq_ref, wuk_ref, wuv_ref,
                     swap_ref, bd_ref, aq_ref, bq_ref, ak_ref, bk_ref, q_ref, k_ref, v_ref):
    h = _rms(x_ref[...], g_ref[...]).astype(BF16)
    c = _dot(h, win_ref[...])
    cq = _rms(c[:, :MLA_Q_LORA], qa_ref[...]).astype(BF16)
    ckv = _rms(c[:, MLA_Q_LORA:MLA_Q_LORA + MLA_KV_LORA], kva_ref[...]).astype(BF16)
    k_rope = c[:, MLA_Q_LORA + MLA_KV_LORA:]
    two = lambda a: jnp.concatenate([a, a], axis=-1)
    aq, bq, ak = two(aq_ref[...]), two(bq_ref[...]), two(ak_ref[...])
    k_rope2 = two(k_rope)
    k_partner = two(pltpu.roll(k_rope, LANES // 2, 1) * bk_ref[...])
    bd = bd_ref[...]

    def inv_rms(y):
        sq = y * y
        hi = sq.astype(BF16)
        lo = (sq - hi.astype(F32)).astype(BF16)
        ss = _dot(hi, bd) + _dot(lo, bd)
        return lax.rsqrt(ss * (1.0 / MLA_QK) + NORM_EPS)

    for p in range(MLA_HEADS // 2):
        sl = slice(p * 2 * LANES, (p + 1) * 2 * LANES)
        yq = _dot(cq, wuq_ref[:, sl])
        yq_partner = _dot(yq.astype(BF16), swap_ref[...])
        q_ref[:, sl] = ((yq * aq + yq_partner * bq) * inv_rms(yq)).astype(BF16)
        yk = _dot(ckv, wuk_ref[:, sl]) + k_rope2
        k_ref[:, sl] = ((yk * ak + k_partner) * inv_rms(yk)).astype(BF16)
    v_ref[...] = _dot(ckv, wuv_ref[...]).astype(BF16)


def _mla_prep(x, g, win, qa, kva, wuq, wuk, wuv, swap, bd, aq, bq, ak, bk, t):
    m = x.shape[0]
    tm = min(TOKEN_TILE, t)
    per_seq = t // tm
    wide = MLA_HEADS * LANES
    table = lambda: pl.BlockSpec((tm, LANES), lambda i: (i % per_seq, 0))
    return pl.pallas_call(
        _mla_prep_kernel,
        grid=(m // tm,),
        in_specs=[pl.BlockSpec((tm, D_MODEL), lambda i: (i, 0)),
                  _resident((1, D_MODEL)), _resident(win.shape),
                  _resident((1, MLA_Q_LORA)), _resident((1, MLA_KV_LORA)),
                  _resident(wuq.shape), _resident(wuk.shape), _resident(wuv.shape),
                  _resident(swap.shape), _resident(bd.shape),
                  table(), table(), table(), table()],
        out_specs=[pl.BlockSpec((tm, wide), lambda i: (i, 0)),
                   pl.BlockSpec((tm, wide), lambda i: (i, 0)),
                   pl.BlockSpec((tm, D_MODEL), lambda i: (i, 0))],
        out_shape=[jax.ShapeDtypeStruct((m, wide), BF16),
                   jax.ShapeDtypeStruct((m, wide), BF16),
                   jax.ShapeDtypeStruct((m, D_MODEL), BF16)],
        compiler_params=_params("parallel"),
        name="mla_prep",
    )(x, g, win, qa, kva, wuq, wuk, wuv, swap, bd, aq, bq, ak, bk)


def _mla_attn_kernel(q_ref, k_ref, v_ref, o_ref):
    lane = lax.broadcasted_iota(jnp.int32, (q_ref.shape[1], LANES), 1)
    first = lane < MLA_V
    v = v_ref[0]
    outs = []
    for hh in range(2):
        sl = slice(hh * LANES, (hh + 1) * LANES)
        s = _dot_nt(q_ref[0, :, sl], k_ref[0, :, sl])
        mx = jnp.max(s, axis=-1, keepdims=True)
        p = jnp.exp2(s - mx)
        den = jnp.sum(p, axis=-1, keepdims=True)
        outs.append(_dot(p.astype(BF16), v) * (1.0 / den))
    o_ref[0] = jnp.where(first, outs[0], outs[1]).astype(o_ref.dtype)


def _mla_attn(q, k, v):
    b, t, _ = q.shape
    tq = min(MLA_Q_TILE, t)
    return pl.pallas_call(
        _mla_attn_kernel,
        grid=(b, MLA_HEADS // 2, t // tq),
        in_specs=[pl.BlockSpec((1, tq, 2 * LANES), lambda i, j, l: (i, l, j)),
                  pl.BlockSpec((1, t, 2 * LANES), lambda i, j, l: (i, 0, j)),
                  pl.BlockSpec((1, t, LANES), lambda i, j, l: (i, 0, j))],
        out_specs=pl.BlockSpec((1, tq, LANES), lambda i, j, l: (i, l, j)),
        out_shape=jax.ShapeDtypeStruct((b, t, D_MODEL), BF16),
        compiler_params=_params("parallel", "parallel", "arbitrary"),
        name="mla_attn",
    )(q, k, v)


def _mla_weights(w_in, w_uq, w_ukv, q_norm, k_norm, t):
    valid = _MLA_PERM >= 0
    src = np.where(valid, _MLA_PERM, 0)
    is_rope = valid & (_MLA_PERM >= MLA_NOPE)
    is_nope = valid & (_MLA_PERM < MLA_NOPE)

    def place(w, per_head, lanes_ok, base=0):
        wh = w.reshape(w.shape[0], MLA_HEADS, per_head)[:, :, base + np.where(lanes_ok, src, 0)]
        return jnp.where(lanes_ok[None, None, :], wh, 0.0).reshape(w.shape[0], MLA_HEADS * LANES)

    wuq = place(w_uq, MLA_QK, valid)
    wuk = place(w_ukv, MLA_NOPE + MLA_V, is_nope)
    wuv = w_ukv.reshape(MLA_KV_LORA, MLA_HEADS, MLA_NOPE + MLA_V)[:, :, MLA_NOPE:].reshape(
        MLA_KV_LORA, MLA_HEADS * MLA_V)
    rope_cols = w_in[:, MLA_Q_LORA + MLA_KV_LORA:]
    rope_placed = jnp.where(is_rope[None, :], rope_cols[:, np.where(is_rope, src - MLA_NOPE, 0)], 0.0)
    win = jnp.concatenate([w_in[:, :MLA_Q_LORA + MLA_KV_LORA], rope_placed], axis=1)
    gain = lambda g: jnp.where(valid, g.astype(F32)[src], 0.0).reshape(1, LANES)
    partner = (np.arange(LANES) + LANES // 2) % LANES
    gain_partner = lambda g: jnp.where(is_rope, gain(g)[0, partner], 0.0).reshape(1, LANES)

    half = MLA_ROPE // 2
    inv_freq = ROPE_THETA ** (-jnp.arange(half, dtype=F32) / half)
    ang = jnp.arange(t).astype(F32)[:, None] * inv_freq[None, :]
    cos = jnp.ones((t, LANES), F32)
    cos = cos.at[:, 0:half].set(jnp.cos(ang)).at[:, 64:64 + half].set(jnp.cos(ang))
    sin = jnp.zeros((t, LANES), F32)
    sin = sin.at[:, 0:half].set(-jnp.sin(ang)).at[:, 64:64 + half].set(jnp.sin(ang))
    q_mult = MLA_QK ** -0.5 * LOG2E
    tables = (gain(q_norm) * cos * q_mult, gain_partner(q_norm) * sin * q_mult,
              gain(k_norm) * cos, gain_partner(k_norm) * sin)

    lane2 = np.arange(2 * LANES)
    rope2 = np.tile(is_rope, 2)
    swap = (lane2[:, None] == (lane2[None, :] // LANES) * LANES + np.tile(partner, 2)[None, :]) & rope2[None, :]
    bd = (lane2[:, None] // LANES) == (lane2[None, :] // LANES)
    return (win.astype(BF16), wuq.astype(BF16), wuk.astype(BF16), wuv.astype(BF16),
            jnp.asarray(swap, BF16), jnp.asarray(bd, BF16)) + tables


def _mla_mixer(x, mix_g, w_in, q_a_norm, w_uq, kv_a_norm, w_ukv, q_norm, k_norm, w_out):
    b, t, _ = x.shape
    m = b * t
    x2 = x.reshape(m, D_MODEL)
    prepared = _mla_weights(w_in, w_uq, w_ukv, q_norm, k_norm, t)
    q, k, v = _mla_prep(x2, mix_g, prepared[0], q_a_norm.astype(F32).reshape(1, -1),
                        kv_a_norm.astype(F32).reshape(1, -1), *prepared[1:], t)
    o = _mla_attn(q.reshape(b, t, -1), k.reshape(b, t, -1), v.reshape(b, t, -1))
    return _out_proj(o.reshape(m, D_MODEL), x2, w_out).reshape(b, t, D_MODEL)


def kernel(x, ffn1_norm, ffn1_w_gu, ffn1_w_down, mix_norm, ffn2_norm, ffn2_w_gu, ffn2_w_down,
           hg_lb_logits, hg_w_in, hg_g_norm, hg_w_out,
           na_w_in, na_q_norm, na_k_norm, na_rpb, na_w_out,
           mla_w_in, mla_q_a_norm, mla_w_uq, mla_kv_a_norm, mla_w_ukv, mla_q_norm, mla_k_norm, mla_w_out):
    b, t, d = x.shape
    m = b * t
    gam = jnp.cumsum(jax.nn.softmax(hg_lb_logits.astype(F32), axis=0), axis=0)
    lb_all = gam - gam[0:1]
    row = lambda g: g.astype(F32).reshape(1, -1)
    bf = lambda w: w.astype(BF16)
    ia = ib = ic = 0
    for layer in range(DEPTH):
        x = _ffn(x.reshape(m, d), row(ffn1_norm[layer]), bf(ffn1_w_gu[layer]),
                 bf(ffn1_w_down[layer])).reshape(b, t, d)
        g = row(mix_norm[layer])
        kind = layer % N_MIXERS
        if kind == 0:
            x = _hgrn_mixer(x, g, bf(hg_w_in[ia]), hg_g_norm[ia].astype(F32), bf(hg_w_out[ia]), lb_all[layer])
            ia += 1
        elif kind == 1:
            x = _na_mixer(x, g, bf(na_w_in[ib]), na_q_norm[ib], na_k_norm[ib], na_rpb[ib], bf(na_w_out[ib]))
            ib += 1
        else:
            x = _mla_mixer(x, g, mla_w_in[ic], mla_q_a_norm[ic], mla_w_uq[ic], mla_kv_a_norm[ic],
                           mla_w_ukv[ic], mla_q_norm[ic], mla_k_norm[ic], bf(mla_w_out[ic]))
            ic += 1
        x = _ffn(x.reshape(m, d), row(ffn2_norm[layer]), bf(ffn2_w_gu[layer]),
                 bf(ffn2_w_down[layer])).reshape(b, t, d)
    return x
```

```python
import functools

import numpy as np
import jax
import jax.numpy as jnp
from jax import lax
from jax.experimental import pallas as pl
from jax.experimental.pallas import tpu as pltpu

D_MODEL = 1024
DEPTH = 4
N_MIXERS = 3
GRID_W = 64
D_FF = 2816
NORM_EPS = 1e-6
HG_HEADS = 8
HG_DK = 128
HG_DV = 128
NA_HEADS = 16
NA_HEAD_DIM = 64
NA_WIN_R = 8
NA_WIN_C = 16
MLA_HEADS = 16
MLA_Q_LORA = 768
MLA_KV_LORA = 256
MLA_NOPE = 64
MLA_ROPE = 32
MLA_V = 64
MLA_QK = MLA_NOPE + MLA_ROPE
ROPE_THETA = 10000.0
NEG_INF = -1e30
LOG2E = 1.4426950408889634

LANES = 128
VMEM_LIMIT = 56 * 1024 * 1024
TOKEN_TILE = 512
HG_CHUNK = 128
HG_LEVELS = (64, 32, 16, 8, 4, 2, 1)
HG_SUBLANES = 8
HG_HEADS_PER_STEP = 2
HG_UNROLL = 2
MLA_Q_TILE = 256
MLA_Q_SUBTILE = 256
NA_GROUP = 4
NA_UNION = NA_WIN_R + NA_GROUP
NA_UNROLL = 2
F32 = jnp.float32
BF16 = jnp.bfloat16


def _params(*sem):
    return pltpu.CompilerParams(dimension_semantics=sem, vmem_limit_bytes=VMEM_LIMIT)


def _resident(shape):
    nd = len(shape)
    return pl.BlockSpec(shape, lambda *_: (0,) * nd, pipeline_mode=pl.Buffered(1))


def _rms(x, g):
    ms = jnp.mean(x * x, axis=-1, keepdims=True)
    return x * lax.rsqrt(ms + NORM_EPS) * g


def _silu(x):
    return x * (1.0 / (1.0 + jnp.exp(-x)))


def _dot(a, b):
    return jnp.dot(a, b, preferred_element_type=F32)


def _dot_nt(a, b):
    return lax.dot_general(a, b, (((1,), (1,)), ((), ())), preferred_element_type=F32)


def _interleave(gens):
    results = [None] * len(gens)
    pending = set(range(len(gens)))
    while pending:
        for i in sorted(pending):
            try:
                next(gens[i])
            except StopIteration as stop:
                results[i] = stop.value
                pending.discard(i)
    return results


def _dot_tn(a, b):
    return lax.dot_general(a, b, (((0,), (0,)), ((), ())), preferred_element_type=F32)


FF_CHUNKS = ((0, 1536), (1536, 2816))


def _ffn_body(x, g, wgu_ref, wd_ref):
    h = _rms(x, g).astype(BF16)
    acc = None
    for s, e in FF_CHUNKS:
        gate = _dot(h, wgu_ref[:, s:e])
        up = _dot(h, wgu_ref[:, D_FF + s:D_FF + e])
        a = (_silu(gate) * up).astype(BF16)
        d = _dot(a, wd_ref[s:e, :])
        acc = d if acc is None else acc + d
    return x + 0.5 * acc


def _ffn_kernel(x_ref, g_ref, wgu_ref, wd_ref, o_ref):
    o_ref[...] = _ffn_body(x_ref[...], g_ref[...], wgu_ref, wd_ref)


def _ffn(x, g, wgu, wd):
    m = x.shape[0]
    tm = min(TOKEN_TILE, m)
    return pl.pallas_call(
        _ffn_kernel,
        grid=(m // tm,),
        in_specs=[pl.BlockSpec((tm, D_MODEL), lambda i: (i, 0)),
                  _resident((1, D_MODEL)),
                  _resident(wgu.shape),
                  _resident(wd.shape)],
        out_specs=pl.BlockSpec((tm, D_MODEL), lambda i: (i, 0)),
        out_shape=jax.ShapeDtypeStruct((m, D_MODEL), F32),
        compiler_params=_params("parallel"),
        name="ffn",
    )(x, g, wgu, wd)


def _out_proj_kernel(a_ref, x_ref, w_ref, o_ref):
    o_ref[...] = x_ref[...] + _dot(a_ref[...], w_ref[...])


def _out_proj(a, x, w):
    m, k = a.shape
    tm = min(TOKEN_TILE, m)
    return pl.pallas_call(
        _out_proj_kernel,
        grid=(m // tm,),
        in_specs=[pl.BlockSpec((tm, k), lambda i: (i, 0)),
                  pl.BlockSpec((tm, D_MODEL), lambda i: (i, 0)),
                  _resident(w.shape)],
        out_specs=pl.BlockSpec((tm, D_MODEL), lambda i: (i, 0)),
        out_shape=jax.ShapeDtypeStruct((m, D_MODEL), F32),
        compiler_params=_params("parallel"),
        name="out_proj",
    )(a, x, w)


def _hgrn_q_row(direction, hb, t):
    return ((t % (2 * hb)) >= hb) != (direction == 1)


def _hgrn_ref_row(direction, hb, t):
    return (t // (2 * hb)) * (2 * hb) + (hb - 1 if direction == 0 else hb)


def _hgrn_tables():
    c = HG_CHUNK
    t = np.arange(c)[:, None]
    s = np.arange(c)[None, :]
    tri = np.stack([s <= t, s >= t]).astype(np.float32)
    wide, full = [], []
    for d in range(2):
        wide_d, full_d = [], []
        for hb in HG_LEVELS:
            same = (t // (2 * hb)) == (s // (2 * hb))
            own = (_hgrn_q_row(d, hb, t) & same & ~_hgrn_q_row(d, hb, s)).astype(np.float32)
            if hb >= HG_SUBLANES:
                wide_d.append(own[_hgrn_q_row(d, hb, np.arange(c))])
            else:
                full_d.append(own)
        wide.append(np.stack(wide_d))
        full.append(np.stack(full_d))
    return tri, np.stack(wide), np.stack(full)


def _hgrn_proj_kernel(x_ref, g_ref, w_ref, llb_ref, l1m_ref, q_ref, lf_ref, lb_ref, v_ref, gate_ref):
    h = _rms(x_ref[...], g_ref[...]).astype(BF16)
    proj = lambda j: _dot(h, w_ref[:, j * D_MODEL:(j + 1) * D_MODEL])
    llb = llb_ref[...]
    l1m = l1m_ref[...]

    def log2_decay(z):
        ls = jnp.minimum(z, 0.0) - jnp.log(1.0 + jnp.exp2(jnp.abs(z) * -LOG2E))
        y = l1m + ls
        return (jnp.maximum(llb, y) + jnp.log(1.0 + jnp.exp2(jnp.abs(llb - y) * -LOG2E))) * LOG2E

    q_ref[...] = (proj(0) * HG_DK ** -0.5).astype(BF16)
    lf_ref[...] = log2_decay(proj(1))
    lb_ref[...] = log2_decay(proj(2))
    v_ref[...] = proj(3).astype(BF16)
    gate_ref[...] = proj(4)


def _hgrn_proj(x, g, w, llb, l1m):
    m = x.shape[0]
    tm = min(TOKEN_TILE, m)
    tile = lambda: pl.BlockSpec((tm, D_MODEL), lambda i: (i, 0))
    return pl.pallas_call(
        _hgrn_proj_kernel,
        grid=(m // tm,),
        in_specs=[tile(), _resident((1, D_MODEL)), _resident(w.shape),
                  _resident((1, D_MODEL)), _resident((1, D_MODEL))],
        out_specs=[tile()] * 5,
        out_shape=[jax.ShapeDtypeStruct((m, D_MODEL), dt) for dt in (BF16, F32, F32, BF16, F32)],
        compiler_params=_params("parallel"),
        name="hgrn_proj",
    )(x, g, w, llb, l1m)


def _hgrn_chunk(direction, q32, v, logf2, tri_ref, wide_ref, full_ref, cum_ref, st_ref):
    c = HG_CHUNK
    nv = c // HG_SUBLANES
    rows = lambda a, i: a[i * HG_SUBLANES:(i + 1) * HG_SUBLANES]
    f = jnp.exp2(logf2)
    k = 1.0 - f
    hi = logf2.astype(BF16)
    lo = (logf2 - hi.astype(F32)).astype(BF16)
    tri = tri_ref[direction]
    cum = _dot(tri, hi) + _dot(tri, lo)
    heads = HG_HEADS_PER_STEP
    hsl = [slice(h * HG_DK, (h + 1) * HG_DK) for h in range(heads)]
    for h in range(heads):
        cum_ref[direction, h] = cum[:, hsl[h]]
    yield
    ref_row = lambda r: jnp.concatenate([cum_ref[direction, h, r:r + 1, :] for h in range(heads)], axis=-1)
    ref_rows = lambda r: jnp.broadcast_to(ref_row(r), (HG_SUBLANES, q32.shape[1]))
    att = [[None] * nv for _ in range(heads)]

    def add_rows(h, i, piece):
        att[h][i] = piece if att[h][i] is None else att[h][i] + piece

    sub = lax.broadcasted_iota(jnp.int32, (c, q32.shape[1]), 0)
    sub8 = lax.broadcasted_iota(jnp.int32, (HG_SUBLANES, q32.shape[1]), 0)
    n_wide = 0
    for li, hb in enumerate(HG_LEVELS):
        if hb >= HG_SUBLANES:
            parts, q_idx = [], []
            for i in range(nv):
                t0 = i * HG_SUBLANES
                ref = ref_rows(int(_hgrn_ref_row(direction, hb, t0)))
                if bool(_hgrn_q_row(direction, hb, t0)):
                    parts.append(rows(q32, i) * jnp.exp2(rows(cum, i) - ref))
                    q_idx.append(i)
                else:
                    parts.append(rows(k, i) * jnp.exp2(ref - rows(cum, i)))
            w = jnp.concatenate(parts, axis=0).astype(BF16)
            wq = jnp.concatenate([parts[i] for i in q_idx], axis=0).astype(BF16)
            for h in range(heads):
                a = _dot_nt(wq[:, hsl[h]], w[:, hsl[h]]) * wide_ref[direction, n_wide]
                for j, i in enumerate(q_idx):
                    add_rows(h, i, rows(a, j))
            n_wide += 1
        else:
            q_rows = _hgrn_q_row(direction, hb, sub)
            if hb == 1:
                w = jnp.where(q_rows, q32 * f, k)
            else:
                pieces = []
                for i in range(nv):
                    t0 = i * HG_SUBLANES
                    if 2 * hb == HG_SUBLANES:
                        pieces.append(ref_rows(int(_hgrn_ref_row(direction, hb, t0))))
                    else:
                        lo_ref = ref_rows(int(_hgrn_ref_row(direction, hb, t0)))
                        hi_ref = ref_rows(int(_hgrn_ref_row(direction, hb, t0 + 2 * hb)))
                        pieces.append(jnp.where(sub8 < 2 * hb, lo_ref, hi_ref))
                d = cum - jnp.concatenate(pieces, axis=0)
                w = jnp.where(q_rows, q32, k) * jnp.exp2(jnp.where(q_rows, d, -d))
            w = w.astype(BF16)
            for h in range(heads):
                a = _dot_nt(w[:, hsl[h]], w[:, hsl[h]]) * full_ref[direction, li - n_wide]
                for i in range(nv):
                    add_rows(h, i, rows(a, i))
        yield

    last = c - 1 if direction == 0 else 0
    total = ref_row(last)
    qi = (q32 * jnp.exp2(cum)).astype(BF16)
    ki = (k * jnp.exp2(total - cum)).astype(BF16)
    diag_in = (q32 * k).astype(BF16)
    ones = jnp.ones((HG_DK, HG_DV), BF16)
    decay = jnp.exp2(total)
    outs = []
    for h in range(heads):
        vh = v[:, hsl[h]]
        st = st_ref[direction, h]
        o = _dot(jnp.concatenate(att[h], axis=0).astype(BF16), vh)
        o = o + _dot(diag_in[:, hsl[h]], ones) * vh.astype(F32)
        o = o + _dot_nt(qi[:, hsl[h]], st.astype(BF16))
        st_ref[direction, h] = st * decay[:, hsl[h]] + _dot_tn(vh, ki[:, hsl[h]])
        outs.append(o)
    return jnp.concatenate(outs, axis=-1)


def _hgrn_kernel(q_ref, lf_ref, lb_ref, v_ref, tri_ref, wide_ref, full_ref,
                 o_ref, ob_ref, cum_ref, st_ref):
    t = q_ref.shape[1]
    c = HG_CHUNK
    n = t // c
    st_ref[...] = jnp.zeros_like(st_ref)

    def chunk(direction, gate_ref, r0, cum_slot):
        return _hgrn_chunk(direction, q_ref[0, pl.ds(r0, c), :].astype(F32), v_ref[0, pl.ds(r0, c), :],
                           gate_ref[0, pl.ds(r0, c), :], tri_ref, wide_ref, full_ref, cum_slot, st_ref)

    def body(i, carry):
        gens, dests = [], []
        for u in range(HG_UNROLL):
            rf = pl.multiple_of((i * HG_UNROLL + u) * c, c)
            rb = pl.multiple_of((n - 1 - i * HG_UNROLL - u) * c, c)
            gens += [chunk(0, lf_ref, rf, cum_ref.at[u]), chunk(1, lb_ref, rb, cum_ref.at[u])]
            dests += [(o_ref.at[0], rf), (ob_ref, rb)]
        for (dst, r0), o in zip(dests, _interleave(gens)):
            dst[pl.ds(r0, c), :] = o
        return carry

    lax.fori_loop(0, n // HG_UNROLL, body, 0)
    o_ref[0] = o_ref[0] + ob_ref[...]


def _hgrn_scan(q, lf, lb, v):
    b, t, _ = q.shape
    tri, wide, full = _hgrn_tables()
    tri = jnp.asarray(tri, BF16)
    wide = jnp.asarray(wide, F32)
    full = jnp.asarray(full, F32)
    width = HG_HEADS_PER_STEP * HG_DK
    seq = lambda: pl.BlockSpec((1, t, width), lambda i, j: (i, 0, j))
    return pl.pallas_call(
        _hgrn_kernel,
        grid=(b, HG_HEADS // HG_HEADS_PER_STEP),
        in_specs=[seq(), seq(), seq(), seq(),
                  _resident(tri.shape), _resident(wide.shape), _resident(full.shape)],
        out_specs=seq(),
        out_shape=jax.ShapeDtypeStruct((b, t, D_MODEL), F32),
        scratch_shapes=[pltpu.VMEM((t, width), F32),
                        pltpu.VMEM((HG_UNROLL, 2, HG_HEADS_PER_STEP, HG_CHUNK, HG_DK), F32),
                        pltpu.VMEM((2, HG_HEADS_PER_STEP, HG_DV, HG_DK), F32)],
        compiler_params=_params("parallel", "parallel"),
        name="hgrn_scan",
    )(q, lf, lb, v, tri, wide, full)


def _hgrn_out_kernel(o_ref, gate_ref, x_ref, gn_ref, w_ref, y_ref):
    gn = gn_ref[...]
    parts = []
    for h in range(HG_HEADS):
        sl = slice(h * HG_DV, (h + 1) * HG_DV)
        parts.append(_rms(o_ref[:, sl], gn) * _silu(gate_ref[:, sl].astype(F32)))
    a = jnp.concatenate(parts, axis=-1).astype(BF16)
    y_ref[...] = x_ref[...] + _dot(a, w_ref[...])


def _hgrn_out(o, gate, x, gn, w):
    m = x.shape[0]
    tm = min(TOKEN_TILE, m)
    tile = lambda: pl.BlockSpec((tm, D_MODEL), lambda i: (i, 0))
    return pl.pallas_call(
        _hgrn_out_kernel,
        grid=(m // tm,),
        in_specs=[tile(), tile(), tile(), _resident((1, HG_DV)), _resident(w.shape)],
        out_specs=tile(),
        out_shape=jax.ShapeDtypeStruct((m, D_MODEL), F32),
        compiler_params=_params("parallel"),
        name="hgrn_out",
    )(o, gate, x, gn, w)


def _hgrn_mixer(x, mix_g, w_in, g_norm, w_out, lb):
    b, t, _ = x.shape
    m = b * t
    x2 = x.reshape(m, D_MODEL)
    lb = lb.astype(F32).reshape(1, D_MODEL)
    q, lf, lbw, v, gate = _hgrn_proj(x2, mix_g, w_in, jnp.log(lb), jnp.log1p(-lb))
    r3 = lambda a: a.reshape(b, t, D_MODEL)
    o = _hgrn_scan(r3(q), r3(lf), r3(lbw), r3(v))
    y = _hgrn_out(o.reshape(m, D_MODEL), gate, x2, g_norm.reshape(1, HG_DV), w_out)
    return y.reshape(b, t, D_MODEL)


def _na_proj_kernel(x_ref, g_ref, w_ref, qg_ref, kg_ref, bd_ref, q_ref, k_ref, v_ref):
    h = _rms(x_ref[...], g_ref[...]).astype(BF16)
    bd = bd_ref[...]

    def head_norm(y, gain, mult):
        parts = []
        for j in range(D_MODEL // LANES):
            ys = y[:, j * LANES:(j + 1) * LANES]
            sq = ys * ys
            hi = sq.astype(BF16)
            lo = (sq - hi.astype(F32)).astype(BF16)
            ss = _dot(hi, bd) + _dot(lo, bd)
            parts.append(ys * lax.rsqrt(ss * (1.0 / NA_HEAD_DIM) + NORM_EPS) * (gain * mult))
        return jnp.concatenate(parts, axis=-1)

    q = _dot(h, w_ref[:, 0:D_MODEL])
    q_ref[...] = head_norm(q, qg_ref[...], NA_HEAD_DIM ** -0.5 * LOG2E).astype(BF16)
    k = _dot(h, w_ref[:, D_MODEL:2 * D_MODEL])
    k_ref[...] = head_norm(k, kg_ref[...], 1.0).astype(BF16)
    v_ref[...] = _dot(h, w_ref[:, 2 * D_MODEL:3 * D_MODEL]).astype(BF16)


def _na_proj(x, g, w, qg, kg):
    m = x.shape[0]
    tm = min(TOKEN_TILE, m)
    blk = np.arange(LANES) // NA_HEAD_DIM
    bd = jnp.asarray(blk[:, None] == blk[None, :], BF16)
    tile = lambda: pl.BlockSpec((tm, D_MODEL), lambda i: (i, 0))
    return pl.pallas_call(
        _na_proj_kernel,
        grid=(m // tm,),
        in_specs=[tile(), _resident((1, D_MODEL)), _resident(w.shape),
                  _resident((1, LANES)), _resident((1, LANES)), _resident((LANES, LANES))],
        out_specs=[tile(), tile(), tile()],
        out_shape=[jax.ShapeDtypeStruct((m, D_MODEL), BF16)] * 3,
        compiler_params=_params("parallel"),
        name="na_proj",
    )(x, g, w, qg, kg, bd)


def _na_group_start(g, rows):
    return jnp.clip(g * NA_GROUP - NA_WIN_R // 2, 0, rows - NA_UNION) if isinstance(g, jax.Array) else \
        int(np.clip(g * NA_GROUP - NA_WIN_R // 2, 0, rows - NA_UNION))


def _na_attn_kernel(rows, q_ref, k_ref, v_ref, bias_ref, cm_ref, o_ref):
    gq = NA_GROUP * GRID_W
    uk = NA_UNION * GRID_W
    n_groups = rows // NA_GROUP
    lane = lax.broadcasted_iota(jnp.int32, (gq, LANES), 1)
    first = lane < NA_HEAD_DIM
    head_sel = (jnp.where(first[0:1], 1.0, 0.0).astype(BF16), jnp.where(first[0:1], 0.0, 1.0).astype(BF16))

    def group(g):
        kind = jnp.where(g == 0, 0, jnp.where(g == n_groups - 1, 2, 1))
        qs = pl.multiple_of(g * gq, gq)
        ks = pl.multiple_of(_na_group_start(g, rows) * GRID_W, NA_GROUP * GRID_W)
        q = q_ref[0, pl.ds(qs, gq), :]
        kb = k_ref[0, pl.ds(ks, uk), :]
        vb = v_ref[0, pl.ds(ks, uk), :]
        s = _dot_nt(jnp.concatenate([q * head_sel[0], q * head_sel[1]], axis=0), kb)
        yield
        cm = cm_ref[kind]
        probs, dens = [], []
        for hh in range(2):
            sh = s[hh * gq:(hh + 1) * gq] * cm + bias_ref[hh, kind]
            p = jnp.exp2(sh - jnp.max(sh, axis=-1, keepdims=True))
            dens.append(jnp.sum(p, axis=-1, keepdims=True))
            probs.append(p.astype(BF16))
            yield
        o2 = _dot(jnp.concatenate(probs, axis=0), vb)
        o = jnp.where(first, o2[:gq] * (1.0 / dens[0]), o2[gq:] * (1.0 / dens[1]))
        o_ref[0, pl.ds(qs, gq), :] = o.astype(o_ref.dtype)

    def body(i, carry):
        _interleave([group(i * NA_UNROLL + u) for u in range(NA_UNROLL)])
        return carry

    lax.fori_loop(0, n_groups // NA_UNROLL, body, 0)


def _na_attn(q, k, v, bias, cm):
    b, t, _ = q.shape
    rows = t // GRID_W
    seq = lambda: pl.BlockSpec((1, t, LANES), lambda j, i: (i, 0, j))
    return pl.pallas_call(
        functools.partial(_na_attn_kernel, rows),
        grid=(D_MODEL // LANES, b),
        in_specs=[seq(), seq(), seq(),
                  pl.BlockSpec((2,) + bias.shape[1:], lambda j, i: (j, 0, 0, 0)),
                  _resident(cm.shape)],
        out_specs=seq(),
        out_shape=jax.ShapeDtypeStruct((b, t, D_MODEL), BF16),
        compiler_params=_params("parallel", "parallel"),
        name="na_attn",
    )(q, k, v, bias, cm)


def _na_tables(rpb, rows):
    assert rows % NA_GROUP == 0 and rows >= NA_UNION + NA_GROUP
    cols = np.arange(GRID_W)
    col_start = np.clip(cols - NA_WIN_C // 2, 0, GRID_W - NA_WIN_C)
    col_mask = (cols[None, :] >= col_start[:, None]) & (cols[None, :] < col_start[:, None] + NA_WIN_C)
    col_off = np.clip(cols[None, :] - cols[:, None] + NA_WIN_C - 1, 0, 2 * NA_WIN_C - 2)
    n_groups = rows // NA_GROUP
    by_col = rpb.astype(F32)[:, :, col_off].transpose(0, 2, 1, 3)
    pad = NA_UNION
    by_col = jnp.pad(by_col, ((0, 0), (0, 0), (pad, pad), (0, 0)))
    valid, kinds = [], []
    for g in (0, 1, n_groups - 1):
        start = _na_group_start(g, rows)
        key = start + np.arange(NA_UNION)[None, :]
        r = g * NA_GROUP + np.arange(NA_GROUP)[:, None]
        r0 = np.clip(r - NA_WIN_R // 2, 0, rows - NA_WIN_R)
        valid.append((key >= r0) & (key < r0 + NA_WIN_R))
        per_row = []
        for a in range(NA_GROUP):
            off = start - (g * NA_GROUP + a) + NA_WIN_R - 1 + pad
            per_row.append(by_col[:, :, off:off + NA_UNION, :])
        kinds.append(jnp.stack(per_row, axis=1))
    valid = np.stack(valid)
    assert (valid.sum(-1) == NA_WIN_R).all()
    full_valid = valid[:, :, None, :, None] & col_mask[None, None, :, None, :]
    shape = (3, NA_GROUP * GRID_W, NA_UNION * GRID_W)
    tab = jnp.stack(kinds, axis=1) * LOG2E
    tab = jnp.where(full_valid[None], tab, NEG_INF * LOG2E).reshape((NA_HEADS,) + shape)
    return tab, jnp.asarray(full_valid.reshape(shape), F32)


def _na_mixer(x, mix_g, w_in, q_norm, k_norm, rpb, w_out):
    b, t, _ = x.shape
    m = b * t
    x2 = x.reshape(m, D_MODEL)
    tile2 = lambda g: jnp.tile(g.astype(F32), LANES // NA_HEAD_DIM).reshape(1, LANES)
    q, k, v = _na_proj(x2, mix_g, w_in, tile2(q_norm), tile2(k_norm))
    bias, cm = _na_tables(rpb, t // GRID_W)
    r3 = lambda a: a.reshape(b, t, D_MODEL)
    o = _na_attn(r3(q), r3(k), r3(v), bias, cm)
    return _out_proj(o.reshape(m, D_MODEL), x2, w_out).reshape(b, t, D_MODEL)


_MLA_PERM = np.concatenate([
    MLA_NOPE + np.arange(16),
    np.arange(48),
    MLA_NOPE + 16 + np.arange(16),
    48 + np.arange(16),
    -np.ones(32, np.int64),
]).astype(np.int64)


def _mla_prep_kernel(x_ref, g_ref, win_ref, qa_ref, kva_ref, wuq_ref, wuk_ref, wuv_ref,
                     swap_ref, bd_ref, aq_ref, bq_ref, ak_ref, bk_ref, q_ref, k_ref, v_ref):
    h = _rms(x_ref[...], g_ref[...]).astype(BF16)
    c = _dot(h, win_ref[...])
    cq = _rms(c[:, :MLA_Q_LORA], qa_ref[...]).astype(BF16)
    ckv = _rms(c[:, MLA_Q_LORA:MLA_Q_LORA + MLA_KV_LORA], kva_ref[...]).astype(BF16)
    k_rope = c[:, MLA_Q_LORA + MLA_KV_LORA:]
    two = lambda a: jnp.concatenate([a, a], axis=-1)
    aq, bq, ak = two(aq_ref[...]), two(bq_ref[...]), two(ak_ref[...])
    k_rope2 = two(k_rope)
    k_partner = two(pltpu.roll(k_rope, LANES // 2, 1) * bk_ref[...])
    bd = bd_ref[...]

    def inv_rms(y):
        sq = y * y
        hi = sq.astype(BF16)
        lo = (sq - hi.astype(F32)).astype(BF16)
        ss = _dot(hi, bd) + _dot(lo, bd)
        return lax.rsqrt(ss * (1.0 / MLA_QK) + NORM_EPS)

    for p in range(MLA_HEADS // 2):
        sl = slice(p * 2 * LANES, (p + 1) * 2 * LANES)
        yq = _dot(cq, wuq_ref[:, sl])
        yq_partner = _dot(yq.astype(BF16), swap_ref[...])
        q_ref[:, sl] = ((yq * aq + yq_partner * bq) * inv_rms(yq)).astype(BF16)
        yk = _dot(ckv, wuk_ref[:, sl]) + k_rope2
        k_ref[:, sl] = ((yk * ak + k_partner) * inv_rms(yk)).astype(BF16)
    v_ref[...] = _dot(ckv, wuv_ref[...]).astype(BF16)


def _mla_prep(x, g, win, qa, kva, wuq, wuk, wuv, swap, bd, aq, bq, ak, bk, t):
    m = x.shape[0]
    tm = min(TOKEN_TILE, t)
    per_seq = t // tm
    wide = MLA_HEADS * LANES
    table = lambda: pl.BlockSpec((tm, LANES), lambda i: (i % per_seq, 0))
    return pl.pallas_call(
        _mla_prep_kernel,
        grid=(m // tm,),
        in_specs=[pl.BlockSpec((tm, D_MODEL), lambda i: (i, 0)),
                  _resident((1, D_MODEL)), _resident(win.shape),
                  _resident((1, MLA_Q_LORA)), _resident((1, MLA_KV_LORA)),
                  _resident(wuq.shape), _resident(wuk.shape), _resident(wuv.shape),
                  _resident(swap.shape), _resident(bd.shape),
                  table(), table(), table(), table()],
        out_specs=[pl.BlockSpec((tm, wide), lambda i: (i, 0)),
                   pl.BlockSpec((tm, wide), lambda i: (i, 0)),
                   pl.BlockSpec((tm, D_MODEL), lambda i: (i, 0))],
        out_shape=[jax.ShapeDtypeStruct((m, wide), BF16),
                   jax.ShapeDtypeStruct((m, wide), BF16),
                   jax.ShapeDtypeStruct((m, D_MODEL), BF16)],
        compiler_params=_params("parallel"),
        name="mla_prep",
    )(x, g, win, qa, kva, wuq, wuk, wuv, swap, bd, aq, bq, ak, bk)


def _mla_attn_kernel(q_ref, k_ref, v_ref, o_ref):
    sub = MLA_Q_SUBTILE
    first = lax.broadcasted_iota(jnp.int32, (sub, LANES), 1) < MLA_V

    def head(r0, hh):
        sl = slice(hh * LANES, (hh + 1) * LANES)
        s = _dot_nt(q_ref[0, r0:r0 + sub, sl], k_ref[0, :, sl])
        yield
        p = jnp.exp2(s - jnp.max(s, axis=-1, keepdims=True))
        den = jnp.sum(p, axis=-1, keepdims=True)
        yield
        return _dot(p.astype(BF16), v_ref[0]) * (1.0 / den)

    starts = range(0, q_ref.shape[1], sub)
    outs = _interleave([head(r0, hh) for r0 in starts for hh in range(2)])
    for n, r0 in enumerate(starts):
        o_ref[0, r0:r0 + sub, :] = jnp.where(first, outs[2 * n], outs[2 * n + 1]).astype(o_ref.dtype)


def _mla_attn(q, k, v):
    b, t, _ = q.shape
    tq = min(MLA_Q_TILE, t)
    return pl.pallas_call(
        _mla_attn_kernel,
        grid=(b, MLA_HEADS // 2, t // tq),
        in_specs=[pl.BlockSpec((1, tq, 2 * LANES), lambda i, j, l: (i, l, j)),
                  pl.BlockSpec((1, t, 2 * LANES), lambda i, j, l: (i, 0, j)),
                  pl.BlockSpec((1, t, LANES), lambda i, j, l: (i, 0, j))],
        out_specs=pl.BlockSpec((1, tq, LANES), lambda i, j, l: (i, l, j)),
        out_shape=jax.ShapeDtypeStruct((b, t, D_MODEL), BF16),
        compiler_params=_params("parallel", "parallel", "arbitrary"),
        name="mla_attn",
    )(q, k, v)


def _mla_weights(w_in, w_uq, w_ukv, q_norm, k_norm, t):
    valid = _MLA_PERM >= 0
    src = np.where(valid, _MLA_PERM, 0)
    is_rope = valid & (_MLA_PERM >= MLA_NOPE)
    is_nope = valid & (_MLA_PERM < MLA_NOPE)

    def place(w, per_head, lanes_ok, base=0):
        wh = w.reshape(w.shape[0], MLA_HEADS, per_head)[:, :, base + np.where(lanes_ok, src, 0)]
        return jnp.where(lanes_ok[None, None, :], wh, 0.0).reshape(w.shape[0], MLA_HEADS * LANES)

    wuq = place(w_uq, MLA_QK, valid)
    wuk = place(w_ukv, MLA_NOPE + MLA_V, is_nope)
    wuv = w_ukv.reshape(MLA_KV_LORA, MLA_HEADS, MLA_NOPE + MLA_V)[:, :, MLA_NOPE:].reshape(
        MLA_KV_LORA, MLA_HEADS * MLA_V)
    rope_cols = w_in[:, MLA_Q_LORA + MLA_KV_LORA:]
    rope_placed = jnp.where(is_rope[None, :], rope_cols[:, np.where(is_rope, src - MLA_NOPE, 0)], 0.0)
    win = jnp.concatenate([w_in[:, :MLA_Q_LORA + MLA_KV_LORA], rope_placed], axis=1)
    gain = lambda g: jnp.where(valid, g.astype(F32)[src], 0.0).reshape(1, LANES)
    partner = (np.arange(LANES) + LANES // 2) % LANES
    gain_partner = lambda g: jnp.where(is_rope, gain(g)[0, partner], 0.0).reshape(1, LANES)

    half = MLA_ROPE // 2
    inv_freq = ROPE_THETA ** (-jnp.arange(half, dtype=F32) / half)
    ang = jnp.arange(t).astype(F32)[:, None] * inv_freq[None, :]
    cos = jnp.ones((t, LANES), F32)
    cos = cos.at[:, 0:half].set(jnp.cos(ang)).at[:, 64:64 + half].set(jnp.cos(ang))
    sin = jnp.zeros((t, LANES), F32)
    sin = sin.at[:, 0:half].set(-jnp.sin(ang)).at[:, 64:64 + half].set(jnp.sin(ang))
    q_mult = MLA_QK ** -0.5 * LOG2E
    tables = (gain(q_norm) * cos * q_mult, gain_partner(q_norm) * sin * q_mult,
              gain(k_norm) * cos, gain_partner(k_norm) * sin)

    lane2 = np.arange(2 * LANES)
    rope2 = np.tile(is_rope, 2)
    swap = (lane2[:, None] == (lane2[None, :] // LANES) * LANES + np.tile(partner, 2)[None, :]) & rope2[None, :]
    bd = (lane2[:, None] // LANES) == (lane2[None, :] // LANES)
    return (win.astype(BF16), wuq.astype(BF16), wuk.astype(BF16), wuv.astype(BF16),
            jnp.asarray(swap, BF16), jnp.asarray(bd, BF16)) + tables


def _mla_mixer(x, mix_g, w_in, q_a_norm, w_uq, kv_a_norm, w_ukv, q_norm, k_norm, w_out):
    b, t, _ = x.shape
    m = b * t
    x2 = x.reshape(m, D_MODEL)
    prepared = _mla_weights(w_in, w_uq, w_ukv, q_norm, k_norm, t)
    q, k, v = _mla_prep(x2, mix_g, prepared[0], q_a_norm.astype(F32).reshape(1, -1),
                        kv_a_norm.astype(F32).reshape(1, -1), *prepared[1:], t)
    o = _mla_attn(q.reshape(b, t, -1), k.reshape(b, t, -1), v.reshape(b, t, -1))
    return _out_proj(o.reshape(m, D_MODEL), x2, w_out).reshape(b, t, D_MODEL)


def kernel(x, ffn1_norm, ffn1_w_gu, ffn1_w_down, mix_norm, ffn2_norm, ffn2_w_gu, ffn2_w_down,
           hg_lb_logits, hg_w_in, hg_g_norm, hg_w_out,
           na_w_in, na_q_norm, na_k_norm, na_rpb, na_w_out,
           mla_w_in, mla_q_a_norm, mla_w_uq, mla_kv_a_norm, mla_w_ukv, mla_q_norm, mla_k_norm, mla_w_out):
    b, t, d = x.shape
    m = b * t
    gam = jnp.cumsum(jax.nn.softmax(hg_lb_logits.astype(F32), axis=0), axis=0)
    lb_all = gam - gam[0:1]
    row = lambda g: g.astype(F32).reshape(1, -1)
    bf = lambda w: w.astype(BF16)
    ia = ib = ic = 0
    for layer in range(DEPTH):
        x = _ffn(x.reshape(m, d), row(ffn1_norm[layer]), bf(ffn1_w_gu[layer]),
                 bf(ffn1_w_down[layer])).reshape(b, t, d)
        g = row(mix_norm[layer])
        kind = layer % N_MIXERS
        if kind == 0:
            x = _hgrn_mixer(x, g, bf(hg_w_in[ia]), hg_g_norm[ia].astype(F32), bf(hg_w_out[ia]), lb_all[layer])
            ia += 1
        elif kind == 1:
            x = _na_mixer(x, g, bf(na_w_in[ib]), na_q_norm[ib], na_k_norm[ib], na_rpb[ib], bf(na_w_out[ib]))
            ib += 1
        else:
            x = _mla_mixer(x, g, mla_w_in[ic], mla_q_a_norm[ic], mla_w_uq[ic], mla_kv_a_norm[ic],
                           mla_w_ukv[ic], mla_q_norm[ic], mla_k_norm[ic], bf(mla_w_out[ic]))
            ic += 1
        x = _ffn(x.reshape(m, d), row(ffn2_norm[layer]), bf(ffn2_w_gu[layer]),
                 bf(ffn2_w_down[layer])).reshape(b, t, d)
    return x
```

```python
import functools

import numpy as np
import jax
import jax.numpy as jnp
from jax import lax
from jax.experimental import pallas as pl
from jax.experimental.pallas import tpu as pltpu

D_MODEL = 1024
DEPTH = 4
N_MIXERS = 3
GRID_W = 64
D_FF = 2816
NORM_EPS = 1e-6
HG_HEADS = 8
HG_DK = 128
HG_DV = 128
NA_HEADS = 16
NA_HEAD_DIM = 64
NA_WIN_R = 8
NA_WIN_C = 16
MLA_HEADS = 16
MLA_Q_LORA = 768
MLA_KV_LORA = 256
MLA_NOPE = 64
MLA_ROPE = 32
MLA_V = 64
MLA_QK = MLA_NOPE + MLA_ROPE
ROPE_THETA = 10000.0
NEG_INF = -1e30
LOG2E = 1.4426950408889634

LANES = 128
VMEM_LIMIT = 56 * 1024 * 1024
TOKEN_TILE = 512
HG_CHUNK = 128
HG_LEVELS = (64, 32, 16, 8, 4, 2, 1)
HG_SUBLANES = 8
HG_HEADS_PER_STEP = 2
HG_UNROLL = 2
MLA_Q_TILE = 512
MLA_Q_SUBTILE = 256
NA_GROUP = 4
NA_UNION = NA_WIN_R + NA_GROUP
NA_UNROLL = 2
F32 = jnp.float32
BF16 = jnp.bfloat16


def _params(*sem):
    return pltpu.CompilerParams(dimension_semantics=sem, vmem_limit_bytes=VMEM_LIMIT)


def _resident(shape):
    nd = len(shape)
    return pl.BlockSpec(shape, lambda *_: (0,) * nd, pipeline_mode=pl.Buffered(1))


def _rms(x, g):
    ms = jnp.mean(x * x, axis=-1, keepdims=True)
    return x * lax.rsqrt(ms + NORM_EPS) * g


def _silu(x):
    return x * (1.0 / (1.0 + jnp.exp(-x)))


def _dot(a, b):
    return jnp.dot(a, b, preferred_element_type=F32)


def _dot_nt(a, b):
    return lax.dot_general(a, b, (((1,), (1,)), ((), ())), preferred_element_type=F32)


def _interleave(gens):
    results = [None] * len(gens)
    pending = set(range(len(gens)))
    while pending:
        for i in sorted(pending):
            try:
                next(gens[i])
            except StopIteration as stop:
                results[i] = stop.value
                pending.discard(i)
    return results


def _dot_tn(a, b):
    return lax.dot_general(a, b, (((0,), (0,)), ((), ())), preferred_element_type=F32)


FF_CHUNKS = ((0, 1536), (1536, 2816))


def _ffn_body(x, g, wgu_ref, wd_ref):
    h = _rms(x, g).astype(BF16)
    acc = None
    for s, e in FF_CHUNKS:
        gate = _dot(h, wgu_ref[:, s:e])
        up = _dot(h, wgu_ref[:, D_FF + s:D_FF + e])
        a = (_silu(gate) * up).astype(BF16)
        d = _dot(a, wd_ref[s:e, :])
        acc = d if acc is None else acc + d
    return x + 0.5 * acc


def _hgrn_gated(o_ref, gate_ref, gn):
    parts = []
    for h in range(HG_HEADS):
        sl = slice(h * HG_DV, (h + 1) * HG_DV)
        parts.append(_rms(o_ref[:, sl], gn) * _silu(gate_ref[:, sl]))
    return jnp.concatenate(parts, axis=-1).astype(BF16)


def _ffn_kernel(mixer, *refs):
    *mixer_refs, x_ref, g_ref, wgu_ref, wd_ref, o_ref = refs
    x = x_ref[...]
    if mixer == "proj":
        a_ref, w_ref = mixer_refs
        x = x + _dot(a_ref[...], w_ref[...])
    elif mixer == "hgrn":
        s_ref, gate_ref, gn_ref, w_ref = mixer_refs
        x = x + _dot(_hgrn_gated(s_ref, gate_ref, gn_ref[...]), w_ref[...])
    o_ref[...] = _ffn_body(x, g_ref[...], wgu_ref, wd_ref)


def _ffn(x, g, wgu, wd, mixer=None, mixer_args=()):
    m = x.shape[0]
    tm = min(TOKEN_TILE, m)
    tile = lambda a: pl.BlockSpec((tm, a.shape[1]), lambda i: (i, 0))
    n_tiled = {None: 0, "proj": 1, "hgrn": 2}[mixer]
    mixer_specs = [tile(a) if i < n_tiled else _resident(a.shape) for i, a in enumerate(mixer_args)]
    return pl.pallas_call(
        functools.partial(_ffn_kernel, mixer),
        grid=(m // tm,),
        in_specs=mixer_specs + [tile(x), _resident((1, D_MODEL)), _resident(wgu.shape), _resident(wd.shape)],
        out_specs=tile(x),
        out_shape=jax.ShapeDtypeStruct((m, D_MODEL), F32),
        compiler_params=_params("parallel"),
        name="ffn" if mixer is None else "ffn_" + mixer,
    )(*mixer_args, x, g, wgu, wd)


def _hgrn_q_row(direction, hb, t):
    return ((t % (2 * hb)) >= hb) != (direction == 1)


def _hgrn_ref_row(direction, hb, t):
    return (t // (2 * hb)) * (2 * hb) + (hb - 1 if direction == 0 else hb)


def _hgrn_tables():
    c = HG_CHUNK
    t = np.arange(c)[:, None]
    s = np.arange(c)[None, :]
    tri = np.stack([s <= t, s >= t]).astype(np.float32)
    wide, full = [], []
    for d in range(2):
        wide_d, full_d = [], []
        for hb in HG_LEVELS:
            same = (t // (2 * hb)) == (s // (2 * hb))
            own = (_hgrn_q_row(d, hb, t) & same & ~_hgrn_q_row(d, hb, s)).astype(np.float32)
            if hb >= HG_SUBLANES:
                wide_d.append(own[_hgrn_q_row(d, hb, np.arange(c))])
            else:
                full_d.append(own)
        wide.append(np.stack(wide_d))
        full.append(np.stack(full_d))
    return tri, np.stack(wide), np.stack(full)


def _hgrn_proj_kernel(x_ref, g_ref, w_ref, llb_ref, l1m_ref, q_ref, lf_ref, lb_ref, v_ref, gate_ref):
    h = _rms(x_ref[...], g_ref[...]).astype(BF16)
    proj = lambda j: _dot(h, w_ref[:, j * D_MODEL:(j + 1) * D_MODEL])
    llb = llb_ref[...]
    l1m = l1m_ref[...]

    def log2_decay(z):
        ls = jnp.minimum(z, 0.0) - jnp.log(1.0 + jnp.exp2(jnp.abs(z) * -LOG2E))
        y = l1m + ls
        return (jnp.maximum(llb, y) + jnp.log(1.0 + jnp.exp2(jnp.abs(llb - y) * -LOG2E))) * LOG2E

    q_ref[...] = (proj(0) * HG_DK ** -0.5).astype(BF16)
    lf_ref[...] = log2_decay(proj(1))
    lb_ref[...] = log2_decay(proj(2))
    v_ref[...] = proj(3).astype(BF16)
    gate_ref[...] = proj(4)


def _hgrn_proj(x, g, w, llb, l1m):
    m = x.shape[0]
    tm = min(TOKEN_TILE, m)
    tile = lambda: pl.BlockSpec((tm, D_MODEL), lambda i: (i, 0))
    return pl.pallas_call(
        _hgrn_proj_kernel,
        grid=(m // tm,),
        in_specs=[tile(), _resident((1, D_MODEL)), _resident(w.shape),
                  _resident((1, D_MODEL)), _resident((1, D_MODEL))],
        out_specs=[tile()] * 5,
        out_shape=[jax.ShapeDtypeStruct((m, D_MODEL), dt) for dt in (BF16, F32, F32, BF16, F32)],
        compiler_params=_params("parallel"),
        name="hgrn_proj",
    )(x, g, w, llb, l1m)


def _hgrn_chunk(direction, q32, v, logf2, tri_ref, wide_ref, full_ref, cum_ref, st_ref):
    c = HG_CHUNK
    nv = c // HG_SUBLANES
    rows = lambda a, i: a[i * HG_SUBLANES:(i + 1) * HG_SUBLANES]
    f = jnp.exp2(logf2)
    k = 1.0 - f
    hi = logf2.astype(BF16)
    lo = (logf2 - hi.astype(F32)).astype(BF16)
    tri = tri_ref[direction]
    cum = _dot(tri, hi) + _dot(tri, lo)
    heads = HG_HEADS_PER_STEP
    hsl = [slice(h * HG_DK, (h + 1) * HG_DK) for h in range(heads)]
    for h in range(heads):
        cum_ref[direction, h] = cum[:, hsl[h]]
    yield
    ref_row = lambda r: jnp.concatenate([cum_ref[direction, h, r:r + 1, :] for h in range(heads)], axis=-1)
    ref_rows = lambda r: jnp.broadcast_to(ref_row(r), (HG_SUBLANES, q32.shape[1]))
    att = [[None] * nv for _ in range(heads)]

    def add_rows(h, i, piece):
        att[h][i] = piece if att[h][i] is None else att[h][i] + piece

    sub = lax.broadcasted_iota(jnp.int32, (c, q32.shape[1]), 0)
    sub8 = lax.broadcasted_iota(jnp.int32, (HG_SUBLANES, q32.shape[1]), 0)
    n_wide = 0
    for li, hb in enumerate(HG_LEVELS):
        if hb >= HG_SUBLANES:
            parts, q_idx = [], []
            for i in range(nv):
                t0 = i * HG_SUBLANES
                ref = ref_rows(int(_hgrn_ref_row(direction, hb, t0)))
                if bool(_hgrn_q_row(direction, hb, t0)):
                    parts.append(rows(q32, i) * jnp.exp2(rows(cum, i) - ref))
                    q_idx.append(i)
                else:
                    parts.append(rows(k, i) * jnp.exp2(ref - rows(cum, i)))
            w = jnp.concatenate(parts, axis=0).astype(BF16)
            wq = jnp.concatenate([parts[i] for i in q_idx], axis=0).astype(BF16)
            for h in range(heads):
                a = _dot_nt(wq[:, hsl[h]], w[:, hsl[h]]) * wide_ref[direction, n_wide]
                for j, i in enumerate(q_idx):
                    add_rows(h, i, rows(a, j))
            n_wide += 1
        else:
            q_rows = _hgrn_q_row(direction, hb, sub)
            if hb == 1:
                w = jnp.where(q_rows, q32 * f, k)
            else:
                pieces = []
                for i in range(nv):
                    t0 = i * HG_SUBLANES
                    if 2 * hb == HG_SUBLANES:
                        pieces.append(ref_rows(int(_hgrn_ref_row(direction, hb, t0))))
                    else:
                        lo_ref = ref_rows(int(_hgrn_ref_row(direction, hb, t0)))
                        hi_ref = ref_rows(int(_hgrn_ref_row(direction, hb, t0 + 2 * hb)))
                        pieces.append(jnp.where(sub8 < 2 * hb, lo_ref, hi_ref))
                d = cum - jnp.concatenate(pieces, axis=0)
                w = jnp.where(q_rows, q32, k) * jnp.exp2(jnp.where(q_rows, d, -d))
            w = w.astype(BF16)
            for h in range(heads):
                a = _dot_nt(w[:, hsl[h]], w[:, hsl[h]]) * full_ref[direction, li - n_wide]
                for i in range(nv):
                    add_rows(h, i, rows(a, i))
        yield

    last = c - 1 if direction == 0 else 0
    total = ref_row(last)
    qi = (q32 * jnp.exp2(cum)).astype(BF16)
    ki = (k * jnp.exp2(total - cum)).astype(BF16)
    diag_in = (q32 * k).astype(BF16)
    ones = jnp.ones((HG_DK, HG_DV), BF16)
    decay = jnp.exp2(total)
    outs = []
    for h in range(heads):
        vh = v[:, hsl[h]]
        st = st_ref[direction, h]
        o = _dot(jnp.concatenate(att[h], axis=0).astype(BF16), vh)
        o = o + _dot(diag_in[:, hsl[h]], ones) * vh.astype(F32)
        o = o + _dot_nt(qi[:, hsl[h]], st.astype(BF16))
        st_ref[direction, h] = st * decay[:, hsl[h]] + _dot_tn(vh, ki[:, hsl[h]])
        outs.append(o)
    return jnp.concatenate(outs, axis=-1)


def _hgrn_kernel(q_ref, lf_ref, lb_ref, v_ref, tri_ref, wide_ref, full_ref,
                 o_ref, ob_ref, cum_ref, st_ref):
    t = q_ref.shape[1]
    c = HG_CHUNK
    n = t // c
    st_ref[...] = jnp.zeros_like(st_ref)

    def chunk(direction, gate_ref, r0, cum_slot):
        return _hgrn_chunk(direction, q_ref[0, pl.ds(r0, c), :].astype(F32), v_ref[0, pl.ds(r0, c), :],
                           gate_ref[0, pl.ds(r0, c), :], tri_ref, wide_ref, full_ref, cum_slot, st_ref)

    def body(i, carry):
        gens, dests = [], []
        for u in range(HG_UNROLL):
            rf = pl.multiple_of((i * HG_UNROLL + u) * c, c)
            rb = pl.multiple_of((n - 1 - i * HG_UNROLL - u) * c, c)
            gens += [chunk(0, lf_ref, rf, cum_ref.at[u]), chunk(1, lb_ref, rb, cum_ref.at[u])]
            dests += [(o_ref.at[0], rf), (ob_ref, rb)]
        for (dst, r0), o in zip(dests, _interleave(gens)):
            dst[pl.ds(r0, c), :] = o
        return carry

    lax.fori_loop(0, n // HG_UNROLL, body, 0)
    o_ref[0] = o_ref[0] + ob_ref[...]


def _hgrn_scan(q, lf, lb, v):
    b, t, _ = q.shape
    tri, wide, full = _hgrn_tables()
    tri = jnp.asarray(tri, BF16)
    wide = jnp.asarray(wide, F32)
    full = jnp.asarray(full, F32)
    width = HG_HEADS_PER_STEP * HG_DK
    seq = lambda: pl.BlockSpec((1, t, width), lambda i, j: (i, 0, j))
    return pl.pallas_call(
        _hgrn_kernel,
        grid=(b, HG_HEADS // HG_HEADS_PER_STEP),
        in_specs=[seq(), seq(), seq(), seq(),
                  _resident(tri.shape), _resident(wide.shape), _resident(full.shape)],
        out_specs=seq(),
        out_shape=jax.ShapeDtypeStruct((b, t, D_MODEL), F32),
        scratch_shapes=[pltpu.VMEM((t, width), F32),
                        pltpu.VMEM((HG_UNROLL, 2, HG_HEADS_PER_STEP, HG_CHUNK, HG_DK), F32),
                        pltpu.VMEM((2, HG_HEADS_PER_STEP, HG_DV, HG_DK), F32)],
        compiler_params=_params("parallel", "parallel"),
        name="hgrn_scan",
    )(q, lf, lb, v, tri, wide, full)


def _hgrn_mixer(x, mix_g, w_in, g_norm, w_out, lb):
    b, t, _ = x.shape
    m = b * t
    lb = lb.astype(F32).reshape(1, D_MODEL)
    q, lf, lbw, v, gate = _hgrn_proj(x.reshape(m, D_MODEL), mix_g, w_in, jnp.log(lb), jnp.log1p(-lb))
    r3 = lambda a: a.reshape(b, t, D_MODEL)
    o = _hgrn_scan(r3(q), r3(lf), r3(lbw), r3(v))
    return "hgrn", (o.reshape(m, D_MODEL), gate, g_norm.reshape(1, HG_DV), w_out)


def _na_proj_kernel(x_ref, g_ref, w_ref, qg_ref, kg_ref, bd_ref, q_ref, k_ref, v_ref):
    h = _rms(x_ref[...], g_ref[...]).astype(BF16)
    bd = bd_ref[...]

    def head_norm(y, gain, mult):
        parts = []
        for j in range(D_MODEL // LANES):
            ys = y[:, j * LANES:(j + 1) * LANES]
            sq = ys * ys
            hi = sq.astype(BF16)
            lo = (sq - hi.astype(F32)).astype(BF16)
            ss = _dot(hi, bd) + _dot(lo, bd)
            parts.append(ys * lax.rsqrt(ss * (1.0 / NA_HEAD_DIM) + NORM_EPS) * (gain * mult))
        return jnp.concatenate(parts, axis=-1)

    q = _dot(h, w_ref[:, 0:D_MODEL])
    q_ref[...] = head_norm(q, qg_ref[...], NA_HEAD_DIM ** -0.5 * LOG2E).astype(BF16)
    k = _dot(h, w_ref[:, D_MODEL:2 * D_MODEL])
    k_ref[...] = head_norm(k, kg_ref[...], 1.0).astype(BF16)
    v_ref[...] = _dot(h, w_ref[:, 2 * D_MODEL:3 * D_MODEL]).astype(BF16)


def _na_proj(x, g, w, qg, kg):
    m = x.shape[0]
    tm = min(TOKEN_TILE, m)
    blk = np.arange(LANES) // NA_HEAD_DIM
    bd = jnp.asarray(blk[:, None] == blk[None, :], BF16)
    tile = lambda: pl.BlockSpec((tm, D_MODEL), lambda i: (i, 0))
    return pl.pallas_call(
        _na_proj_kernel,
        grid=(m // tm,),
        in_specs=[tile(), _resident((1, D_MODEL)), _resident(w.shape),
                  _resident((1, LANES)), _resident((1, LANES)), _resident((LANES, LANES))],
        out_specs=[tile(), tile(), tile()],
        out_shape=[jax.ShapeDtypeStruct((m, D_MODEL), BF16)] * 3,
        compiler_params=_params("parallel"),
        name="na_proj",
    )(x, g, w, qg, kg, bd)


def _na_group_start(g, rows):
    return jnp.clip(g * NA_GROUP - NA_WIN_R // 2, 0, rows - NA_UNION) if isinstance(g, jax.Array) else \
        int(np.clip(g * NA_GROUP - NA_WIN_R // 2, 0, rows - NA_UNION))


def _na_attn_kernel(rows, q_ref, k_ref, v_ref, bias_ref, cm_ref, o_ref):
    gq = NA_GROUP * GRID_W
    uk = NA_UNION * GRID_W
    n_groups = rows // NA_GROUP
    lane = lax.broadcasted_iota(jnp.int32, (gq, LANES), 1)
    first = lane < NA_HEAD_DIM
    head_sel = (jnp.where(first[0:1], 1.0, 0.0).astype(BF16), jnp.where(first[0:1], 0.0, 1.0).astype(BF16))

    def group(g):
        kind = jnp.where(g == 0, 0, jnp.where(g == n_groups - 1, 2, 1))
        qs = pl.multiple_of(g * gq, gq)
        ks = pl.multiple_of(_na_group_start(g, rows) * GRID_W, NA_GROUP * GRID_W)
        q = q_ref[0, pl.ds(qs, gq), :]
        kb = k_ref[0, pl.ds(ks, uk), :]
        vb = v_ref[0, pl.ds(ks, uk), :]
        s = _dot_nt(jnp.concatenate([q * head_sel[0], q * head_sel[1]], axis=0), kb)
        yield
        cm = cm_ref[kind]
        probs = []
        for hh in range(2):
            sh = s[hh * gq:(hh + 1) * gq] * cm + bias_ref[hh, kind]
            probs.append(jnp.exp2((sh - jnp.max(sh, axis=-1, keepdims=True)).astype(BF16)))
            yield
        v_ext = jnp.concatenate([vb, jnp.ones_like(vb)], axis=-1)
        o2 = _dot(jnp.concatenate(probs, axis=0), v_ext)
        o2 = o2[:, :LANES] * (1.0 / o2[:, LANES:])
        o_ref[0, pl.ds(qs, gq), :] = jnp.where(first, o2[:gq], o2[gq:]).astype(o_ref.dtype)

    def body(i, carry):
        _interleave([group(i * NA_UNROLL + u) for u in range(NA_UNROLL)])
        return carry

    lax.fori_loop(0, n_groups // NA_UNROLL, body, 0)


def _na_attn(q, k, v, bias, cm):
    b, t, _ = q.shape
    rows = t // GRID_W
    seq = lambda: pl.BlockSpec((1, t, LANES), lambda j, i: (i, 0, j))
    return pl.pallas_call(
        functools.partial(_na_attn_kernel, rows),
        grid=(D_MODEL // LANES, b),
        in_specs=[seq(), seq(), seq(),
                  pl.BlockSpec((2,) + bias.shape[1:], lambda j, i: (j, 0, 0, 0)),
                  _resident(cm.shape)],
        out_specs=seq(),
        out_shape=jax.ShapeDtypeStruct((b, t, D_MODEL), BF16),
        compiler_params=_params("parallel", "parallel"),
        name="na_attn",
    )(q, k, v, bias, cm)


def _na_tables(rpb, rows):
    assert rows % NA_GROUP == 0 and rows >= NA_UNION + NA_GROUP
    cols = np.arange(GRID_W)
    col_start = np.clip(cols - NA_WIN_C // 2, 0, GRID_W - NA_WIN_C)
    col_mask = (cols[None, :] >= col_start[:, None]) & (cols[None, :] < col_start[:, None] + NA_WIN_C)
    col_off = np.clip(cols[None, :] - cols[:, None] + NA_WIN_C - 1, 0, 2 * NA_WIN_C - 2)
    n_groups = rows // NA_GROUP
    by_col = rpb.astype(F32)[:, :, col_off].transpose(0, 2, 1, 3)
    pad = NA_UNION
    by_col = jnp.pad(by_col, ((0, 0), (0, 0), (pad, pad), (0, 0)))
    valid, kinds = [], []
    for g in (0, 1, n_groups - 1):
        start = _na_group_start(g, rows)
        key = start + np.arange(NA_UNION)[None, :]
        r = g * NA_GROUP + np.arange(NA_GROUP)[:, None]
        r0 = np.clip(r - NA_WIN_R // 2, 0, rows - NA_WIN_R)
        valid.append((key >= r0) & (key < r0 + NA_WIN_R))
        per_row = []
        for a in range(NA_GROUP):
            off = start - (g * NA_GROUP + a) + NA_WIN_R - 1 + pad
            per_row.append(by_col[:, :, off:off + NA_UNION, :])
        kinds.append(jnp.stack(per_row, axis=1))
    valid = np.stack(valid)
    assert (valid.sum(-1) == NA_WIN_R).all()
    full_valid = valid[:, :, None, :, None] & col_mask[None, None, :, None, :]
    shape = (3, NA_GROUP * GRID_W, NA_UNION * GRID_W)
    tab = jnp.stack(kinds, axis=1) * LOG2E
    tab = jnp.where(full_valid[None], tab, NEG_INF * LOG2E).reshape((NA_HEADS,) + shape)
    return tab, jnp.asarray(full_valid.reshape(shape), F32)


def _na_mixer(x, mix_g, w_in, q_norm, k_norm, rpb, w_out):
    b, t, _ = x.shape
    m = b * t
    tile2 = lambda g: jnp.tile(g.astype(F32), LANES // NA_HEAD_DIM).reshape(1, LANES)
    q, k, v = _na_proj(x.reshape(m, D_MODEL), mix_g, w_in, tile2(q_norm), tile2(k_norm))
    bias, cm = _na_tables(rpb, t // GRID_W)
    r3 = lambda a: a.reshape(b, t, D_MODEL)
    o = _na_attn(r3(q), r3(k), r3(v), bias, cm)
    return "proj", (o.reshape(m, D_MODEL), w_out)


_MLA_PERM = np.concatenate([
    MLA_NOPE + np.arange(16),
    np.arange(48),
    MLA_NOPE + 16 + np.arange(16),
    48 + np.arange(16),
    -np.ones(32, np.int64),
]).astype(np.int64)


def _mla_prep_kernel(x_ref, g_ref, win_ref, qa_ref, kva_ref, wuq_ref, wuk_ref, wuv_ref,
                     swap_ref, bd_ref, aq_ref, bq_ref, ak_ref, bk_ref, q_ref, k_ref, v_ref):
    h = _rms(x_ref[...], g_ref[...]).astype(BF16)
    c = _dot(h, win_ref[...])
    cq = _rms(c[:, :MLA_Q_LORA], qa_ref[...]).astype(BF16)
    ckv = _rms(c[:, MLA_Q_LORA:MLA_Q_LORA + MLA_KV_LORA], kva_ref[...]).astype(BF16)
    k_rope = c[:, MLA_Q_LORA + MLA_KV_LORA:]
    two = lambda a: jnp.concatenate([a, a], axis=-1)
    aq, bq, ak = two(aq_ref[...]), two(bq_ref[...]), two(ak_ref[...])
    k_rope2 = two(k_rope)
    k_partner = two(pltpu.roll(k_rope, LANES // 2, 1) * bk_ref[...])
    bd = bd_ref[...]

    def inv_rms(y):
        sq = y * y
        hi = sq.astype(BF16)
        lo = (sq - hi.astype(F32)).astype(BF16)
        ss = _dot(hi, bd) + _dot(lo, bd)
        return lax.rsqrt(ss * (1.0 / MLA_QK) + NORM_EPS)

    for p in range(MLA_HEADS // 2):
        sl = slice(p * 2 * LANES, (p + 1) * 2 * LANES)
        yq = _dot(cq, wuq_ref[:, sl])
        yq_partner = _dot(yq.astype(BF16), swap_ref[...])
        q_ref[:, sl] = ((yq * aq + yq_partner * bq) * inv_rms(yq)).astype(BF16)
        yk = _dot(ckv, wuk_ref[:, sl]) + k_rope2
        k_ref[:, sl] = ((yk * ak + k_partner) * inv_rms(yk)).astype(BF16)
    v_ref[...] = _dot(ckv, wuv_ref[...]).astype(BF16)


def _mla_prep(x, g, win, qa, kva, wuq, wuk, wuv, swap, bd, aq, bq, ak, bk, t):
    m = x.shape[0]
    tm = min(TOKEN_TILE, t)
    per_seq = t // tm
    wide = MLA_HEADS * LANES
    table = lambda: pl.BlockSpec((tm, LANES), lambda i: (i % per_seq, 0))
    return pl.pallas_call(
        _mla_prep_kernel,
        grid=(m // tm,),
        in_specs=[pl.BlockSpec((tm, D_MODEL), lambda i: (i, 0)),
                  _resident((1, D_MODEL)), _resident(win.shape),
                  _resident((1, MLA_Q_LORA)), _resident((1, MLA_KV_LORA)),
                  _resident(wuq.shape), _resident(wuk.shape), _resident(wuv.shape),
                  _resident(swap.shape), _resident(bd.shape),
                  table(), table(), table(), table()],
        out_specs=[pl.BlockSpec((tm, wide), lambda i: (i, 0)),
                   pl.BlockSpec((tm, wide), lambda i: (i, 0)),
                   pl.BlockSpec((tm, D_MODEL), lambda i: (i, 0))],
        out_shape=[jax.ShapeDtypeStruct((m, wide), BF16),
                   jax.ShapeDtypeStruct((m, wide), BF16),
                   jax.ShapeDtypeStruct((m, D_MODEL), BF16)],
        compiler_params=_params("parallel"),
        name="mla_prep",
    )(x, g, win, qa, kva, wuq, wuk, wuv, swap, bd, aq, bq, ak, bk)


def _mla_attn_kernel(q_ref, k_ref, v_ref, o_ref):
    sub = MLA_Q_SUBTILE
    first = lax.broadcasted_iota(jnp.int32, (sub, LANES), 1) < MLA_V
    v_ext = jnp.concatenate([v_ref[0], jnp.ones((v_ref.shape[1], LANES), BF16)], axis=-1)

    def head(r0, hh):
        sl = slice(hh * LANES, (hh + 1) * LANES)
        s = _dot_nt(q_ref[0, r0:r0 + sub, sl], k_ref[0, :, sl])
        yield
        p = jnp.exp2((s - jnp.max(s, axis=-1, keepdims=True)).astype(BF16))
        yield
        o = _dot(p, v_ext)
        return o[:, :LANES] * (1.0 / o[:, LANES:])

    starts = range(0, q_ref.shape[1], sub)
    outs = _interleave([head(r0, hh) for r0 in starts for hh in range(2)])
    for n, r0 in enumerate(starts):
        o_ref[0, r0:r0 + sub, :] = jnp.where(first, outs[2 * n], outs[2 * n + 1]).astype(o_ref.dtype)


def _mla_attn(q, k, v):
    b, t, _ = q.shape
    tq = min(MLA_Q_TILE, t)
    return pl.pallas_call(
        _mla_attn_kernel,
        grid=(b, MLA_HEADS // 2, t // tq),
        in_specs=[pl.BlockSpec((1, tq, 2 * LANES), lambda i, j, l: (i, l, j)),
                  pl.BlockSpec((1, t, 2 * LANES), lambda i, j, l: (i, 0, j)),
                  pl.BlockSpec((1, t, LANES), lambda i, j, l: (i, 0, j))],
        out_specs=pl.BlockSpec((1, tq, LANES), lambda i, j, l: (i, l, j)),
        out_shape=jax.ShapeDtypeStruct((b, t, D_MODEL), BF16),
        compiler_params=_params("parallel", "parallel", "arbitrary"),
        name="mla_attn",
    )(q, k, v)


def _mla_weights(w_in, w_uq, w_ukv, q_norm, k_norm, t):
    valid = _MLA_PERM >= 0
    src = np.where(valid, _MLA_PERM, 0)
    is_rope = valid & (_MLA_PERM >= MLA_NOPE)
    is_nope = valid & (_MLA_PERM < MLA_NOPE)

    def place(w, per_head, lanes_ok, base=0):
        wh = w.reshape(w.shape[0], MLA_HEADS, per_head)[:, :, base + np.where(lanes_ok, src, 0)]
        return jnp.where(lanes_ok[None, None, :], wh, 0.0).reshape(w.shape[0], MLA_HEADS * LANES)

    wuq = place(w_uq, MLA_QK, valid)
    wuk = place(w_ukv, MLA_NOPE + MLA_V, is_nope)
    wuv = w_ukv.reshape(MLA_KV_LORA, MLA_HEADS, MLA_NOPE + MLA_V)[:, :, MLA_NOPE:].reshape(
        MLA_KV_LORA, MLA_HEADS * MLA_V)
    rope_cols = w_in[:, MLA_Q_LORA + MLA_KV_LORA:]
    rope_placed = jnp.where(is_rope[None, :], rope_cols[:, np.where(is_rope, src - MLA_NOPE, 0)], 0.0)
    win = jnp.concatenate([w_in[:, :MLA_Q_LORA + MLA_KV_LORA], rope_placed], axis=1)
    gain = lambda g: jnp.where(valid, g.astype(F32)[src], 0.0).reshape(1, LANES)
    partner = (np.arange(LANES) + LANES // 2) % LANES
    gain_partner = lambda g: jnp.where(is_rope, gain(g)[0, partner], 0.0).reshape(1, LANES)

    half = MLA_ROPE // 2
    inv_freq = ROPE_THETA ** (-jnp.arange(half, dtype=F32) / half)
    ang = jnp.arange(t).astype(F32)[:, None] * inv_freq[None, :]
    cos = jnp.ones((t, LANES), F32)
    cos = cos.at[:, 0:half].set(jnp.cos(ang)).at[:, 64:64 + half].set(jnp.cos(ang))
    sin = jnp.zeros((t, LANES), F32)
    sin = sin.at[:, 0:half].set(-jnp.sin(ang)).at[:, 64:64 + half].set(jnp.sin(ang))
    q_mult = MLA_QK ** -0.5 * LOG2E
    tables = (gain(q_norm) * cos * q_mult, gain_partner(q_norm) * sin * q_mult,
              gain(k_norm) * cos, gain_partner(k_norm) * sin)

    lane2 = np.arange(2 * LANES)
    rope2 = np.tile(is_rope, 2)
    swap = (lane2[:, None] == (lane2[None, :] // LANES) * LANES + np.tile(partner, 2)[None, :]) & rope2[None, :]
    bd = (lane2[:, None] // LANES) == (lane2[None, :] // LANES)
    return (win.astype(BF16), wuq.astype(BF16), wuk.astype(BF16), wuv.astype(BF16),
            jnp.asarray(swap, BF16), jnp.asarray(bd, BF16)) + tables


def _mla_mixer(x, mix_g, w_in, q_a_norm, w_uq, kv_a_norm, w_ukv, q_norm, k_norm, w_out):
    b, t, _ = x.shape
    m = b * t
    prepared = _mla_weights(w_in, w_uq, w_ukv, q_norm, k_norm, t)
    q, k, v = _mla_prep(x.reshape(m, D_MODEL), mix_g, prepared[0], q_a_norm.astype(F32).reshape(1, -1),
                        kv_a_norm.astype(F32).reshape(1, -1), *prepared[1:], t)
    o = _mla_attn(q.reshape(b, t, -1), k.reshape(b, t, -1), v.reshape(b, t, -1))
    return "proj", (o.reshape(m, D_MODEL), w_out)


def kernel(x, ffn1_norm, ffn1_w_gu, ffn1_w_down, mix_norm, ffn2_norm, ffn2_w_gu, ffn2_w_down,
           hg_lb_logits, hg_w_in, hg_g_norm, hg_w_out,
           na_w_in, na_q_norm, na_k_norm, na_rpb, na_w_out,
           mla_w_in, mla_q_a_norm, mla_w_uq, mla_kv_a_norm, mla_w_ukv, mla_q_norm, mla_k_norm, mla_w_out):
    b, t, d = x.shape
    m = b * t
    gam = jnp.cumsum(jax.nn.softmax(hg_lb_logits.astype(F32), axis=0), axis=0)
    lb_all = gam - gam[0:1]
    row = lambda g: g.astype(F32).reshape(1, -1)
    bf = lambda w: w.astype(BF16)
    ia = ib = ic = 0
    for layer in range(DEPTH):
        x = _ffn(x.reshape(m, d), row(ffn1_norm[layer]), bf(ffn1_w_gu[layer]),
                 bf(ffn1_w_down[layer])).reshape(b, t, d)
        g = row(mix_norm[layer])
        kind = layer % N_MIXERS
        if kind == 0:
            mixer = _hgrn_mixer(x, g, bf(hg_w_in[ia]), hg_g_norm[ia].astype(F32), bf(hg_w_out[ia]), lb_all[layer])
            ia += 1
        elif kind == 1:
            mixer = _na_mixer(x, g, bf(na_w_in[ib]), na_q_norm[ib], na_k_norm[ib], na_rpb[ib], bf(na_w_out[ib]))
            ib += 1
        else:
            mixer = _mla_mixer(x, g, mla_w_in[ic], mla_q_a_norm[ic], mla_w_uq[ic], mla_kv_a_norm[ic],
                               mla_w_ukv[ic], mla_q_norm[ic], mla_k_norm[ic], bf(mla_w_out[ic]))
            ic += 1
        x = _ffn(x.reshape(m, d), row(ffn2_norm[layer]), bf(ffn2_w_gu[layer]),
                 bf(ffn2_w_down[layer]), *mixer).reshape(b, t, d)
    return x
```

```python
import functools

import numpy as np
import jax
import jax.numpy as jnp
from jax import lax
from jax.experimental import pallas as pl
from jax.experimental.pallas import tpu as pltpu

D_MODEL = 1024
DEPTH = 4
N_MIXERS = 3
GRID_W = 64
D_FF = 2816
NORM_EPS = 1e-6
HG_HEADS = 8
HG_DK = 128
HG_DV = 128
NA_HEADS = 16
NA_HEAD_DIM = 64
NA_WIN_R = 8
NA_WIN_C = 16
MLA_HEADS = 16
MLA_Q_LORA = 768
MLA_KV_LORA = 256
MLA_NOPE = 64
MLA_ROPE = 32
MLA_V = 64
MLA_QK = MLA_NOPE + MLA_ROPE
ROPE_THETA = 10000.0
NEG_INF = -1e30
LOG2E = 1.4426950408889634

LANES = 128
VMEM_LIMIT = 56 * 1024 * 1024
TOKEN_TILE = 512
HG_CHUNK = 128
HG_LEVELS = (64, 32, 16, 8, 4, 2, 1)
HG_SUBLANES = 8
HG_HEADS_PER_STEP = 2
HG_UNROLL = 2
MLA_Q_TILE = 512
MLA_Q_SUBTILE = 256
NA_GROUP = 4
NA_UNION = NA_WIN_R + NA_GROUP
NA_UNROLL = 2
F32 = jnp.float32
BF16 = jnp.bfloat16


def _params(*sem):
    return pltpu.CompilerParams(dimension_semantics=sem, vmem_limit_bytes=VMEM_LIMIT)


def _resident(shape):
    nd = len(shape)
    return pl.BlockSpec(shape, lambda *_: (0,) * nd, pipeline_mode=pl.Buffered(1))


def _rms(x, g):
    ms = jnp.mean(x * x, axis=-1, keepdims=True)
    return x * lax.rsqrt(ms + NORM_EPS) * g


def _silu(x):
    return x * (1.0 / (1.0 + jnp.exp(-x)))


def _dot(a, b):
    return jnp.dot(a, b, preferred_element_type=F32)


def _dot_nt(a, b):
    return lax.dot_general(a, b, (((1,), (1,)), ((), ())), preferred_element_type=F32)


def _interleave(gens):
    results = [None] * len(gens)
    pending = set(range(len(gens)))
    while pending:
        for i in sorted(pending):
            try:
                next(gens[i])
            except StopIteration as stop:
                results[i] = stop.value
                pending.discard(i)
    return results


def _dot_tn(a, b):
    return lax.dot_general(a, b, (((0,), (0,)), ((), ())), preferred_element_type=F32)


FF_CHUNKS = ((0, 1536), (1536, 2816))


def _ffn_body(x, g, wgu_ref, wd_ref):
    h = _rms(x, g).astype(BF16)
    acc = None
    for s, e in FF_CHUNKS:
        gate = _dot(h, wgu_ref[:, s:e])
        up = _dot(h, wgu_ref[:, D_FF + s:D_FF + e])
        a = (_silu(gate) * up).astype(BF16)
        d = _dot(a, wd_ref[s:e, :])
        acc = d if acc is None else acc + d
    return x + 0.5 * acc


def _hgrn_gated(o_ref, gate_ref, gn):
    parts = []
    for h in range(HG_HEADS):
        sl = slice(h * HG_DV, (h + 1) * HG_DV)
        parts.append(_rms(o_ref[:, sl], gn) * _silu(gate_ref[:, sl]))
    return jnp.concatenate(parts, axis=-1).astype(BF16)


def _ffn_kernel(mixer, *refs):
    *mixer_refs, x_ref, g_ref, wgu_ref, wd_ref, o_ref = refs
    x = x_ref[...]
    if mixer == "proj":
        a_ref, w_ref = mixer_refs
        x = x + _dot(a_ref[...], w_ref[...])
    elif mixer == "hgrn":
        s_ref, gate_ref, gn_ref, w_ref = mixer_refs
        x = x + _dot(_hgrn_gated(s_ref, gate_ref, gn_ref[...]), w_ref[...])
    o_ref[...] = _ffn_body(x, g_ref[...], wgu_ref, wd_ref)


def _ffn(x, g, wgu, wd, mixer=None, mixer_args=()):
    m = x.shape[0]
    tm = min(TOKEN_TILE, m)
    tile = lambda a: pl.BlockSpec((tm, a.shape[1]), lambda i: (i, 0))
    n_tiled = {None: 0, "proj": 1, "hgrn": 2}[mixer]
    mixer_specs = [tile(a) if i < n_tiled else _resident(a.shape) for i, a in enumerate(mixer_args)]
    return pl.pallas_call(
        functools.partial(_ffn_kernel, mixer),
        grid=(m // tm,),
        in_specs=mixer_specs + [tile(x), _resident((1, D_MODEL)), _resident(wgu.shape), _resident(wd.shape)],
        out_specs=tile(x),
        out_shape=jax.ShapeDtypeStruct((m, D_MODEL), F32),
        compiler_params=_params("parallel"),
        name="ffn" if mixer is None else "ffn_" + mixer,
    )(*mixer_args, x, g, wgu, wd)


def _hgrn_q_row(direction, hb, t):
    return ((t % (2 * hb)) >= hb) != (direction == 1)


def _hgrn_ref_row(direction, hb, t):
    return (t // (2 * hb)) * (2 * hb) + (hb - 1 if direction == 0 else hb)


def _hgrn_tables():
    c = HG_CHUNK
    t = np.arange(c)[:, None]
    s = np.arange(c)[None, :]
    tri = np.stack([s <= t, s >= t]).astype(np.float32)
    wide, full = [], []
    for d in range(2):
        wide_d, full_d = [], []
        for hb in HG_LEVELS:
            same = (t // (2 * hb)) == (s // (2 * hb))
            own = (_hgrn_q_row(d, hb, t) & same & ~_hgrn_q_row(d, hb, s)).astype(np.float32)
            if hb >= HG_SUBLANES:
                wide_d.append(own[_hgrn_q_row(d, hb, np.arange(c))])
            else:
                full_d.append(own)
        wide.append(np.stack(wide_d))
        full.append(np.stack(full_d))
    return tri, np.stack(wide), np.stack(full)


def _hgrn_proj_kernel(x_ref, g_ref, w_ref, llb_ref, l1m_ref, q_ref, lf_ref, lb_ref, v_ref, gate_ref):
    h = _rms(x_ref[...], g_ref[...]).astype(BF16)
    proj = lambda j: _dot(h, w_ref[:, j * D_MODEL:(j + 1) * D_MODEL])
    llb = llb_ref[...]
    l1m = l1m_ref[...]

    def log2_decay(z):
        ls = jnp.minimum(z, 0.0) - jnp.log(1.0 + jnp.exp2(jnp.abs(z) * -LOG2E))
        y = l1m + ls
        return (jnp.maximum(llb, y) + jnp.log(1.0 + jnp.exp2(jnp.abs(llb - y) * -LOG2E))) * LOG2E

    q_ref[...] = (proj(0) * HG_DK ** -0.5).astype(BF16)
    lf_ref[...] = log2_decay(proj(1))
    lb_ref[...] = log2_decay(proj(2))
    v_ref[...] = proj(3).astype(BF16)
    gate_ref[...] = proj(4)


def _hgrn_proj(x, g, w, llb, l1m):
    m = x.shape[0]
    tm = min(TOKEN_TILE, m)
    tile = lambda: pl.BlockSpec((tm, D_MODEL), lambda i: (i, 0))
    return pl.pallas_call(
        _hgrn_proj_kernel,
        grid=(m // tm,),
        in_specs=[tile(), _resident((1, D_MODEL)), _resident(w.shape),
                  _resident((1, D_MODEL)), _resident((1, D_MODEL))],
        out_specs=[tile()] * 5,
        out_shape=[jax.ShapeDtypeStruct((m, D_MODEL), dt) for dt in (BF16, F32, F32, BF16, F32)],
        compiler_params=_params("parallel"),
        name="hgrn_proj",
    )(x, g, w, llb, l1m)


def _hgrn_chunk(direction, q32, v, logf2, tri_ref, wide_ref, full_ref, cum_ref, st_ref):
    c = HG_CHUNK
    nv = c // HG_SUBLANES
    rows = lambda a, i: a[i * HG_SUBLANES:(i + 1) * HG_SUBLANES]
    f = jnp.exp2(logf2)
    k = 1.0 - f
    hi = logf2.astype(BF16)
    lo = (logf2 - hi.astype(F32)).astype(BF16)
    tri = tri_ref[direction]
    cum = _dot(tri, hi) + _dot(tri, lo)
    heads = HG_HEADS_PER_STEP
    hsl = [slice(h * HG_DK, (h + 1) * HG_DK) for h in range(heads)]
    for h in range(heads):
        cum_ref[direction, h] = cum[:, hsl[h]]
    yield
    ref_row = lambda r: jnp.concatenate([cum_ref[direction, h, r:r + 1, :] for h in range(heads)], axis=-1)
    ref_rows = lambda r: jnp.broadcast_to(ref_row(r), (HG_SUBLANES, q32.shape[1]))
    att = [[None] * nv for _ in range(heads)]

    def add_rows(h, i, piece):
        att[h][i] = piece if att[h][i] is None else att[h][i] + piece

    sub = lax.broadcasted_iota(jnp.int32, (c, q32.shape[1]), 0)
    sub8 = lax.broadcasted_iota(jnp.int32, (HG_SUBLANES, q32.shape[1]), 0)
    n_wide = 0
    for li, hb in enumerate(HG_LEVELS):
        if hb >= HG_SUBLANES:
            parts, q_idx = [], []
            for i in range(nv):
                t0 = i * HG_SUBLANES
                ref = ref_rows(int(_hgrn_ref_row(direction, hb, t0)))
                if bool(_hgrn_q_row(direction, hb, t0)):
                    parts.append(rows(q32, i) * jnp.exp2(rows(cum, i) - ref))
                    q_idx.append(i)
                else:
                    parts.append(rows(k, i) * jnp.exp2(ref - rows(cum, i)))
            w = jnp.concatenate(parts, axis=0).astype(BF16)
            wq = jnp.concatenate([parts[i] for i in q_idx], axis=0).astype(BF16)
            for h in range(heads):
                a = _dot_nt(wq[:, hsl[h]], w[:, hsl[h]]) * wide_ref[direction, n_wide]
                for j, i in enumerate(q_idx):
                    add_rows(h, i, rows(a, j))
            n_wide += 1
        else:
            q_rows = _hgrn_q_row(direction, hb, sub)
            if hb == 1:
                w = jnp.where(q_rows, q32 * f, k)
            else:
                pieces = []
                for i in range(nv):
                    t0 = i * HG_SUBLANES
                    if 2 * hb == HG_SUBLANES:
                        pieces.append(ref_rows(int(_hgrn_ref_row(direction, hb, t0))))
                    else:
                        lo_ref = ref_rows(int(_hgrn_ref_row(direction, hb, t0)))
                        hi_ref = ref_rows(int(_hgrn_ref_row(direction, hb, t0 + 2 * hb)))
                        pieces.append(jnp.where(sub8 < 2 * hb, lo_ref, hi_ref))
                d = cum - jnp.concatenate(pieces, axis=0)
                w = jnp.where(q_rows, q32, k) * jnp.exp2(jnp.where(q_rows, d, -d))
            w = w.astype(BF16)
            for h in range(heads):
                a = _dot_nt(w[:, hsl[h]], w[:, hsl[h]]) * full_ref[direction, li - n_wide]
                for i in range(nv):
                    add_rows(h, i, rows(a, i))
        yield

    last = c - 1 if direction == 0 else 0
    total = ref_row(last)
    qi = (q32 * jnp.exp2(cum)).astype(BF16)
    ki = (k * jnp.exp2(total - cum)).astype(BF16)
    diag_in = (q32 * k).astype(BF16)
    ones = jnp.ones((HG_DK, HG_DV), BF16)
    decay = jnp.exp2(total)
    outs = []
    for h in range(heads):
        vh = v[:, hsl[h]]
        st = st_ref[direction, h]
        o = _dot(jnp.concatenate(att[h], axis=0).astype(BF16), vh)
        o = o + _dot(diag_in[:, hsl[h]], ones) * vh.astype(F32)
        o = o + _dot_nt(qi[:, hsl[h]], st.astype(BF16))
        st_ref[direction, h] = st * decay[:, hsl[h]] + _dot_tn(vh, ki[:, hsl[h]])
        outs.append(o)
    return jnp.concatenate(outs, axis=-1)


def _hgrn_kernel(q_ref, lf_ref, lb_ref, v_ref, tri_ref, wide_ref, full_ref,
                 o_ref, ob_ref, cum_ref, st_ref):
    t = q_ref.shape[1]
    c = HG_CHUNK
    n = t // c
    st_ref[...] = jnp.zeros_like(st_ref)

    def chunk(direction, gate_ref, r0, cum_slot):
        return _hgrn_chunk(direction, q_ref[0, pl.ds(r0, c), :].astype(F32), v_ref[0, pl.ds(r0, c), :],
                           gate_ref[0, pl.ds(r0, c), :], tri_ref, wide_ref, full_ref, cum_slot, st_ref)

    def body(i, carry):
        gens, dests = [], []
        for u in range(HG_UNROLL):
            rf = pl.multiple_of((i * HG_UNROLL + u) * c, c)
            rb = pl.multiple_of((n - 1 - i * HG_UNROLL - u) * c, c)
            gens += [chunk(0, lf_ref, rf, cum_ref.at[u]), chunk(1, lb_ref, rb, cum_ref.at[u])]
            dests += [(o_ref.at[0], rf), (ob_ref, rb)]
        for (dst, r0), o in zip(dests, _interleave(gens)):
            dst[pl.ds(r0, c), :] = o
        return carry

    lax.fori_loop(0, n // HG_UNROLL, body, 0)
    o_ref[0] = o_ref[0] + ob_ref[...]


def _hgrn_scan(q, lf, lb, v):
    b, t, _ = q.shape
    tri, wide, full = _hgrn_tables()
    tri = jnp.asarray(tri, BF16)
    wide = jnp.asarray(wide, F32)
    full = jnp.asarray(full, F32)
    width = HG_HEADS_PER_STEP * HG_DK
    seq = lambda: pl.BlockSpec((1, t, width), lambda i, j: (i, 0, j))
    return pl.pallas_call(
        _hgrn_kernel,
        grid=(b, HG_HEADS // HG_HEADS_PER_STEP),
        in_specs=[seq(), seq(), seq(), seq(),
                  _resident(tri.shape), _resident(wide.shape), _resident(full.shape)],
        out_specs=seq(),
        out_shape=jax.ShapeDtypeStruct((b, t, D_MODEL), F32),
        scratch_shapes=[pltpu.VMEM((t, width), F32),
                        pltpu.VMEM((HG_UNROLL, 2, HG_HEADS_PER_STEP, HG_CHUNK, HG_DK), F32),
                        pltpu.VMEM((2, HG_HEADS_PER_STEP, HG_DV, HG_DK), F32)],
        compiler_params=_params("parallel", "parallel"),
        name="hgrn_scan",
    )(q, lf, lb, v, tri, wide, full)


def _hgrn_mixer(x, mix_g, w_in, g_norm, w_out, lb):
    b, t, _ = x.shape
    m = b * t
    lb = lb.astype(F32).reshape(1, D_MODEL)
    q, lf, lbw, v, gate = _hgrn_proj(x.reshape(m, D_MODEL), mix_g, w_in, jnp.log(lb), jnp.log1p(-lb))
    r3 = lambda a: a.reshape(b, t, D_MODEL)
    o = _hgrn_scan(r3(q), r3(lf), r3(lbw), r3(v))
    return "hgrn", (o.reshape(m, D_MODEL), gate, g_norm.reshape(1, HG_DV), w_out)


def _na_proj_kernel(x_ref, g_ref, w_ref, qg_ref, kg_ref, bd_ref, q_ref, k_ref, v_ref):
    h = _rms(x_ref[...], g_ref[...]).astype(BF16)
    bd = bd_ref[...]

    def head_norm(y, gain, mult):
        width = bd.shape[0]
        parts = []
        for j in range(D_MODEL // width):
            ys = y[:, j * width:(j + 1) * width]
            ss = _dot((ys * ys).astype(BF16), bd)
            parts.append(ys * lax.rsqrt(ss * (1.0 / NA_HEAD_DIM) + NORM_EPS) * (gain * mult))
        return jnp.concatenate(parts, axis=-1)

    q = _dot(h, w_ref[:, 0:D_MODEL])
    q_ref[...] = head_norm(q, qg_ref[...], NA_HEAD_DIM ** -0.5 * LOG2E).astype(BF16)
    k = _dot(h, w_ref[:, D_MODEL:2 * D_MODEL])
    k_ref[...] = head_norm(k, kg_ref[...], 1.0).astype(BF16)
    v_ref[...] = _dot(h, w_ref[:, 2 * D_MODEL:3 * D_MODEL]).astype(BF16)


def _na_proj(x, g, w, qg, kg):
    m = x.shape[0]
    tm = min(TOKEN_TILE, m)
    blk = np.arange(2 * LANES) // NA_HEAD_DIM
    bd = jnp.asarray(blk[:, None] == blk[None, :], BF16)
    tile = lambda: pl.BlockSpec((tm, D_MODEL), lambda i: (i, 0))
    return pl.pallas_call(
        _na_proj_kernel,
        grid=(m // tm,),
        in_specs=[tile(), _resident((1, D_MODEL)), _resident(w.shape),
                  _resident(qg.shape), _resident(kg.shape), _resident(bd.shape)],
        out_specs=[tile(), tile(), tile()],
        out_shape=[jax.ShapeDtypeStruct((m, D_MODEL), BF16)] * 3,
        compiler_params=_params("parallel"),
        name="na_proj",
    )(x, g, w, qg, kg, bd)


def _na_group_start(g, rows):
    return jnp.clip(g * NA_GROUP - NA_WIN_R // 2, 0, rows - NA_UNION) if isinstance(g, jax.Array) else \
        int(np.clip(g * NA_GROUP - NA_WIN_R // 2, 0, rows - NA_UNION))


def _na_attn_kernel(rows, q_ref, k_ref, v_ref, bias_ref, cm_ref, o_ref):
    gq = NA_GROUP * GRID_W
    uk = NA_UNION * GRID_W
    n_groups = rows // NA_GROUP
    lane = lax.broadcasted_iota(jnp.int32, (gq, LANES), 1)
    first = lane < NA_HEAD_DIM
    head_sel = (jnp.where(first[0:1], 1.0, 0.0).astype(BF16), jnp.where(first[0:1], 0.0, 1.0).astype(BF16))

    def group(g):
        kind = jnp.where(g == 0, 0, jnp.where(g == n_groups - 1, 2, 1))
        qs = pl.multiple_of(g * gq, gq)
        ks = pl.multiple_of(_na_group_start(g, rows) * GRID_W, NA_GROUP * GRID_W)
        q = q_ref[0, pl.ds(qs, gq), :]
        kb = k_ref[0, pl.ds(ks, uk), :]
        vb = v_ref[0, pl.ds(ks, uk), :]
        s = _dot_nt(jnp.concatenate([q * head_sel[0], q * head_sel[1]], axis=0), kb)
        yield
        cm = cm_ref[kind]
        probs = []
        for hh in range(2):
            sh = s[hh * gq:(hh + 1) * gq] * cm + bias_ref[hh, kind]
            probs.append(jnp.exp2((sh - jnp.max(sh, axis=-1, keepdims=True)).astype(BF16)))
            yield
        v_ext = jnp.concatenate([vb, jnp.ones_like(vb)], axis=-1)
        o2 = _dot(jnp.concatenate(probs, axis=0), v_ext)
        o2 = o2[:, :LANES] * (1.0 / o2[:, LANES:])
        o_ref[0, pl.ds(qs, gq), :] = jnp.where(first, o2[:gq], o2[gq:]).astype(o_ref.dtype)

    def body(i, carry):
        _interleave([group(i * NA_UNROLL + u) for u in range(NA_UNROLL)])
        return carry

    lax.fori_loop(0, n_groups // NA_UNROLL, body, 0)


def _na_attn(q, k, v, bias, cm):
    b, t, _ = q.shape
    rows = t // GRID_W
    seq = lambda: pl.BlockSpec((1, t, LANES), lambda j, i: (i, 0, j))
    return pl.pallas_call(
        functools.partial(_na_attn_kernel, rows),
        grid=(D_MODEL // LANES, b),
        in_specs=[seq(), seq(), seq(),
                  pl.BlockSpec((2,) + bias.shape[1:], lambda j, i: (j, 0, 0, 0)),
                  _resident(cm.shape)],
        out_specs=seq(),
        out_shape=jax.ShapeDtypeStruct((b, t, D_MODEL), BF16),
        compiler_params=_params("parallel", "parallel"),
        name="na_attn",
    )(q, k, v, bias, cm)


def _na_tables(rpb, rows):
    assert rows % NA_GROUP == 0 and rows >= NA_UNION + NA_GROUP
    cols = np.arange(GRID_W)
    col_start = np.clip(cols - NA_WIN_C // 2, 0, GRID_W - NA_WIN_C)
    col_mask = (cols[None, :] >= col_start[:, None]) & (cols[None, :] < col_start[:, None] + NA_WIN_C)
    col_off = np.clip(cols[None, :] - cols[:, None] + NA_WIN_C - 1, 0, 2 * NA_WIN_C - 2)
    n_groups = rows // NA_GROUP
    by_col = rpb.astype(F32)[:, :, col_off].transpose(0, 2, 1, 3)
    pad = NA_UNION
    by_col = jnp.pad(by_col, ((0, 0), (0, 0), (pad, pad), (0, 0)))
    valid, kinds = [], []
    for g in (0, 1, n_groups - 1):
        start = _na_group_start(g, rows)
        key = start + np.arange(NA_UNION)[None, :]
        r = g * NA_GROUP + np.arange(NA_GROUP)[:, None]
        r0 = np.clip(r - NA_WIN_R // 2, 0, rows - NA_WIN_R)
        valid.append((key >= r0) & (key < r0 + NA_WIN_R))
        per_row = []
        for a in range(NA_GROUP):
            off = start - (g * NA_GROUP + a) + NA_WIN_R - 1 + pad
            per_row.append(by_col[:, :, off:off + NA_UNION, :])
        kinds.append(jnp.stack(per_row, axis=1))
    valid = np.stack(valid)
    assert (valid.sum(-1) == NA_WIN_R).all()
    full_valid = valid[:, :, None, :, None] & col_mask[None, None, :, None, :]
    shape = (3, NA_GROUP * GRID_W, NA_UNION * GRID_W)
    tab = jnp.stack(kinds, axis=1) * LOG2E
    tab = jnp.where(full_valid[None], tab, NEG_INF * LOG2E).reshape((NA_HEADS,) + shape)
    return tab, jnp.asarray(full_valid.reshape(shape), F32)


def _na_mixer(x, mix_g, w_in, q_norm, k_norm, rpb, w_out):
    b, t, _ = x.shape
    m = b * t
    tile2 = lambda g: jnp.tile(g.astype(F32), 2 * LANES // NA_HEAD_DIM).reshape(1, 2 * LANES)
    q, k, v = _na_proj(x.reshape(m, D_MODEL), mix_g, w_in, tile2(q_norm), tile2(k_norm))
    bias, cm = _na_tables(rpb, t // GRID_W)
    r3 = lambda a: a.reshape(b, t, D_MODEL)
    o = _na_attn(r3(q), r3(k), r3(v), bias, cm)
    return "proj", (o.reshape(m, D_MODEL), w_out)


_MLA_PERM = np.concatenate([
    MLA_NOPE + np.arange(16),
    np.arange(48),
    MLA_NOPE + 16 + np.arange(16),
    48 + np.arange(16),
    -np.ones(32, np.int64),
]).astype(np.int64)


def _mla_prep_kernel(x_ref, g_ref, win_ref, qa_ref, kva_ref, wuq_ref, wuk_ref, wuv_ref,
                     swap_ref, bd_ref, aq_ref, bq_ref, ak_ref, bk_ref, q_ref, k_ref, v_ref):
    h = _rms(x_ref[...], g_ref[...]).astype(BF16)
    c = _dot(h, win_ref[...])
    cq = _rms(c[:, :MLA_Q_LORA], qa_ref[...]).astype(BF16)
    ckv = _rms(c[:, MLA_Q_LORA:MLA_Q_LORA + MLA_KV_LORA], kva_ref[...]).astype(BF16)
    k_rope = c[:, MLA_Q_LORA + MLA_KV_LORA:]
    two = lambda a: jnp.concatenate([a, a], axis=-1)
    aq, bq, ak = two(aq_ref[...]), two(bq_ref[...]), two(ak_ref[...])
    k_rope2 = two(k_rope)
    k_partner = two(pltpu.roll(k_rope, LANES // 2, 1) * bk_ref[...])
    bd = bd_ref[...]

    def inv_rms(y):
        ss = _dot((y * y).astype(BF16), bd)
        return lax.rsqrt(ss * (1.0 / MLA_QK) + NORM_EPS)

    for p in range(MLA_HEADS // 2):
        sl = slice(p * 2 * LANES, (p + 1) * 2 * LANES)
        yq = _dot(cq, wuq_ref[:, sl])
        yq_partner = _dot(yq.astype(BF16), swap_ref[...])
        q_ref[:, sl] = ((yq * aq + yq_partner * bq) * inv_rms(yq)).astype(BF16)
        yk = _dot(ckv, wuk_ref[:, sl]) + k_rope2
        k_ref[:, sl] = ((yk * ak + k_partner) * inv_rms(yk)).astype(BF16)
    v_ref[...] = _dot(ckv, wuv_ref[...]).astype(BF16)


def _mla_prep(x, g, win, qa, kva, wuq, wuk, wuv, swap, bd, aq, bq, ak, bk, t):
    m = x.shape[0]
    tm = min(TOKEN_TILE, t)
    per_seq = t // tm
    wide = MLA_HEADS * LANES
    table = lambda: pl.BlockSpec((tm, LANES), lambda i: (i % per_seq, 0))
    return pl.pallas_call(
        _mla_prep_kernel,
        grid=(m // tm,),
        in_specs=[pl.BlockSpec((tm, D_MODEL), lambda i: (i, 0)),
                  _resident((1, D_MODEL)), _resident(win.shape),
                  _resident((1, MLA_Q_LORA)), _resident((1, MLA_KV_LORA)),
                  _resident(wuq.shape), _resident(wuk.shape), _resident(wuv.shape),
                  _resident(swap.shape), _resident(bd.shape),
                  table(), table(), table(), table()],
        out_specs=[pl.BlockSpec((tm, wide), lambda i: (i, 0)),
                   pl.BlockSpec((tm, wide), lambda i: (i, 0)),
                   pl.BlockSpec((tm, D_MODEL), lambda i: (i, 0))],
        out_shape=[jax.ShapeDtypeStruct((m, wide), BF16),
                   jax.ShapeDtypeStruct((m, wide), BF16),
                   jax.ShapeDtypeStruct((m, D_MODEL), BF16)],
        compiler_params=_params("parallel"),
        name="mla_prep",
    )(x, g, win, qa, kva, wuq, wuk, wuv, swap, bd, aq, bq, ak, bk)


def _mla_attn_kernel(q_ref, k_ref, v_ref, o_ref):
    sub = MLA_Q_SUBTILE
    first = lax.broadcasted_iota(jnp.int32, (sub, LANES), 1) < MLA_V
    v_ext = jnp.concatenate([v_ref[0], jnp.ones((v_ref.shape[1], LANES), BF16)], axis=-1)

    def head(r0, hh):
        sl = slice(hh * LANES, (hh + 1) * LANES)
        s = _dot_nt(q_ref[0, r0:r0 + sub, sl], k_ref[0, :, sl])
        yield
        p = jnp.exp2((s - jnp.max(s, axis=-1, keepdims=True)).astype(BF16))
        yield
        o = _dot(p, v_ext)
        return o[:, :LANES] * (1.0 / o[:, LANES:])

    starts = range(0, q_ref.shape[1], sub)
    outs = _interleave([head(r0, hh) for r0 in starts for hh in range(2)])
    for n, r0 in enumerate(starts):
        o_ref[0, r0:r0 + sub, :] = jnp.where(first, outs[2 * n], outs[2 * n + 1]).astype(o_ref.dtype)


def _mla_attn(q, k, v):
    b, t, _ = q.shape
    tq = min(MLA_Q_TILE, t)
    return pl.pallas_call(
        _mla_attn_kernel,
        grid=(b, MLA_HEADS // 2, t // tq),
        in_specs=[pl.BlockSpec((1, tq, 2 * LANES), lambda i, j, l: (i, l, j)),
                  pl.BlockSpec((1, t, 2 * LANES), lambda i, j, l: (i, 0, j)),
                  pl.BlockSpec((1, t, LANES), lambda i, j, l: (i, 0, j))],
        out_specs=pl.BlockSpec((1, tq, LANES), lambda i, j, l: (i, l, j)),
        out_shape=jax.ShapeDtypeStruct((b, t, D_MODEL), BF16),
        compiler_params=_params("parallel", "parallel", "arbitrary"),
        name="mla_attn",
    )(q, k, v)


def _mla_weights(w_in, w_uq, w_ukv, q_norm, k_norm, t):
    valid = _MLA_PERM >= 0
    src = np.where(valid, _MLA_PERM, 0)
    is_rope = valid & (_MLA_PERM >= MLA_NOPE)
    is_nope = valid & (_MLA_PERM < MLA_NOPE)

    def place(w, per_head, lanes_ok, base=0):
        wh = w.reshape(w.shape[0], MLA_HEADS, per_head)[:, :, base + np.where(lanes_ok, src, 0)]
        return jnp.where(lanes_ok[None, None, :], wh, 0.0).reshape(w.shape[0], MLA_HEADS * LANES)

    wuq = place(w_uq, MLA_QK, valid)
    wuk = place(w_ukv, MLA_NOPE + MLA_V, is_nope)
    wuv = w_ukv.reshape(MLA_KV_LORA, MLA_HEADS, MLA_NOPE + MLA_V)[:, :, MLA_NOPE:].reshape(
        MLA_KV_LORA, MLA_HEADS * MLA_V)
    rope_cols = w_in[:, MLA_Q_LORA + MLA_KV_LORA:]
    rope_placed = jnp.where(is_rope[None, :], rope_cols[:, np.where(is_rope, src - MLA_NOPE, 0)], 0.0)
    win = jnp.concatenate([w_in[:, :MLA_Q_LORA + MLA_KV_LORA], rope_placed], axis=1)
    gain = lambda g: jnp.where(valid, g.astype(F32)[src], 0.0).reshape(1, LANES)
    partner = (np.arange(LANES) + LANES // 2) % LANES
    gain_partner = lambda g: jnp.where(is_rope, gain(g)[0, partner], 0.0).reshape(1, LANES)

    half = MLA_ROPE // 2
    inv_freq = ROPE_THETA ** (-jnp.arange(half, dtype=F32) / half)
    ang = jnp.arange(t).astype(F32)[:, None] * inv_freq[None, :]
    cos = jnp.ones((t, LANES), F32)
    cos = cos.at[:, 0:half].set(jnp.cos(ang)).at[:, 64:64 + half].set(jnp.cos(ang))
    sin = jnp.zeros((t, LANES), F32)
    sin = sin.at[:, 0:half].set(-jnp.sin(ang)).at[:, 64:64 + half].set(jnp.sin(ang))
    q_mult = MLA_QK ** -0.5 * LOG2E
    tables = (gain(q_norm) * cos * q_mult, gain_partner(q_norm) * sin * q_mult,
              gain(k_norm) * cos, gain_partner(k_norm) * sin)

    lane2 = np.arange(2 * LANES)
    rope2 = np.tile(is_rope, 2)
    swap = (lane2[:, None] == (lane2[None, :] // LANES) * LANES + np.tile(partner, 2)[None, :]) & rope2[None, :]
    bd = (lane2[:, None] // LANES) == (lane2[None, :] // LANES)
    return (win.astype(BF16), wuq.astype(BF16), wuk.astype(BF16), wuv.astype(BF16),
            jnp.asarray(swap, BF16), jnp.asarray(bd, BF16)) + tables


def _mla_mixer(x, mix_g, w_in, q_a_norm, w_uq, kv_a_norm, w_ukv, q_norm, k_norm, w_out):
    b, t, _ = x.shape
    m = b * t
    prepared = _mla_weights(w_in, w_uq, w_ukv, q_norm, k_norm, t)
    q, k, v = _mla_prep(x.reshape(m, D_MODEL), mix_g, prepared[0], q_a_norm.astype(F32).reshape(1, -1),
                        kv_a_norm.astype(F32).reshape(1, -1), *prepared[1:], t)
    o = _mla_attn(q.reshape(b, t, -1), k.reshape(b, t, -1), v.reshape(b, t, -1))
    return "proj", (o.reshape(m, D_MODEL), w_out)


def kernel(x, ffn1_norm, ffn1_w_gu, ffn1_w_down, mix_norm, ffn2_norm, ffn2_w_gu, ffn2_w_down,
           hg_lb_logits, hg_w_in, hg_g_norm, hg_w_out,
           na_w_in, na_q_norm, na_k_norm, na_rpb, na_w_out,
           mla_w_in, mla_q_a_norm, mla_w_uq, mla_kv_a_norm, mla_w_ukv, mla_q_norm, mla_k_norm, mla_w_out):
    b, t, d = x.shape
    m = b * t
    gam = jnp.cumsum(jax.nn.softmax(hg_lb_logits.astype(F32), axis=0), axis=0)
    lb_all = gam - gam[0:1]
    row = lambda g: g.astype(F32).reshape(1, -1)
    bf = lambda w: w.astype(BF16)
    ia = ib = ic = 0
    for layer in range(DEPTH):
        x = _ffn(x.reshape(m, d), row(ffn1_norm[layer]), bf(ffn1_w_gu[layer]),
                 bf(ffn1_w_down[layer])).reshape(b, t, d)
        g = row(mix_norm[layer])
        kind = layer % N_MIXERS
        if kind == 0:
            mixer = _hgrn_mixer(x, g, bf(hg_w_in[ia]), hg_g_norm[ia].astype(F32), bf(hg_w_out[ia]), lb_all[layer])
            ia += 1
        elif kind == 1:
            mixer = _na_mixer(x, g, bf(na_w_in[ib]), na_q_norm[ib], na_k_norm[ib], na_rpb[ib], bf(na_w_out[ib]))
            ib += 1
        else:
            mixer = _mla_mixer(x, g, mla_w_in[ic], mla_q_a_norm[ic], mla_w_uq[ic], mla_kv_a_norm[ic],
                               mla_w_ukv[ic], mla_q_norm[ic], mla_k_norm[ic], bf(mla_w_out[ic]))
            ic += 1
        x = _ffn(x.reshape(m, d), row(ffn2_norm[layer]), bf(ffn2_w_gu[layer]),
                 bf(ffn2_w_down[layer]), *mixer).reshape(b, t, d)
    return x
```

```python
import functools

import numpy as np
import jax
import jax.numpy as jnp
from jax import lax
from jax.experimental import pallas as pl
from jax.experimental.pallas import tpu as pltpu

D_MODEL = 1024
DEPTH = 4
N_MIXERS = 3
GRID_W = 64
D_FF = 2816
NORM_EPS = 1e-6
HG_HEADS = 8
HG_DK = 128
HG_DV = 128
NA_HEADS = 16
NA_HEAD_DIM = 64
NA_WIN_R = 8
NA_WIN_C = 16
MLA_HEADS = 16
MLA_Q_LORA = 768
MLA_KV_LORA = 256
MLA_NOPE = 64
MLA_ROPE = 32
MLA_V = 64
MLA_QK = MLA_NOPE + MLA_ROPE
ROPE_THETA = 10000.0
NEG_INF = -1e30
LOG2E = 1.4426950408889634

LANES = 128
VMEM_LIMIT = 56 * 1024 * 1024
TOKEN_TILE = 512
HG_CHUNK = 128
HG_LEVELS = (64, 32, 16, 8, 4, 2, 1)
HG_SUBLANES = 8
HG_HEADS_PER_STEP = 2
HG_GROUPS = HG_HEADS // HG_HEADS_PER_STEP
HG_WIDTH = HG_HEADS_PER_STEP * HG_DK
HG_UNROLL = 2
MLA_Q_TILE = 512
MLA_Q_SUBTILE = 256
NA_GROUP = 4
NA_UNION = NA_WIN_R + NA_GROUP
NA_UNROLL = 2
F32 = jnp.float32
BF16 = jnp.bfloat16


def _params(*sem):
    return pltpu.CompilerParams(dimension_semantics=sem, vmem_limit_bytes=VMEM_LIMIT)


def _resident(shape):
    nd = len(shape)
    return pl.BlockSpec(shape, lambda *_: (0,) * nd, pipeline_mode=pl.Buffered(1))


def _rms(x, g):
    ms = jnp.mean(x * x, axis=-1, keepdims=True)
    return x * lax.rsqrt(ms + NORM_EPS) * g


def _silu(x):
    return x * (1.0 / (1.0 + jnp.exp(-x)))


def _dot(a, b):
    return jnp.dot(a, b, preferred_element_type=F32)


def _dot_nt(a, b):
    return lax.dot_general(a, b, (((1,), (1,)), ((), ())), preferred_element_type=F32)


def _interleave(gens):
    results = [None] * len(gens)
    pending = set(range(len(gens)))
    while pending:
        for i in sorted(pending):
            try:
                next(gens[i])
            except StopIteration as stop:
                results[i] = stop.value
                pending.discard(i)
    return results


def _dot_tn(a, b):
    return lax.dot_general(a, b, (((0,), (0,)), ((), ())), preferred_element_type=F32)


FF_CHUNKS = ((0, 1536), (1536, 2816))


def _ffn_body(x, g, wgu_ref, wd_ref):
    h = _rms(x, g).astype(BF16)
    acc = None
    for s, e in FF_CHUNKS:
        gate = _dot(h, wgu_ref[:, s:e])
        up = _dot(h, wgu_ref[:, D_FF + s:D_FF + e])
        a = (_silu(gate) * up).astype(BF16)
        d = _dot(a, wd_ref[s:e, :])
        acc = d if acc is None else acc + d
    return x + 0.5 * acc


def _hgrn_gated(o_ref, gate_ref, gn):
    parts = []
    for h in range(HG_HEADS):
        p, within = divmod(h, HG_HEADS_PER_STEP)
        o = o_ref[0, p, :, within * HG_DV:(within + 1) * HG_DV]
        parts.append(_rms(o, gn) * _silu(gate_ref[:, h * HG_DV:(h + 1) * HG_DV]))
    return jnp.concatenate(parts, axis=-1).astype(BF16)


def _ffn_kernel(mixer, *refs):
    *mixer_refs, x_ref, g_ref, wgu_ref, wd_ref, o_ref = refs
    x = x_ref[...]
    if mixer == "proj":
        a_ref, w_ref = mixer_refs
        x = x + _dot(a_ref[...], w_ref[...])
    elif mixer == "hgrn":
        s_ref, gate_ref, gn_ref, w_ref = mixer_refs
        x = x + _dot(_hgrn_gated(s_ref, gate_ref, gn_ref[...]), w_ref[...])
    o_ref[...] = _ffn_body(x, g_ref[...], wgu_ref, wd_ref)


def _ffn(x, g, wgu, wd, mixer=None, mixer_args=()):
    m = x.shape[0]
    tm = min(TOKEN_TILE, m)
    tile = lambda a: pl.BlockSpec((tm, a.shape[1]), lambda i: (i, 0))
    n_tiled = {None: 0, "proj": 1, "hgrn": 2}[mixer]
    mixer_specs = [tile(a) if i < n_tiled else _resident(a.shape) for i, a in enumerate(mixer_args)]
    if mixer == "hgrn":
        mixer_specs[0] = _hgrn_grouped_tile(tm, mixer_args[0].shape[2])
    return pl.pallas_call(
        functools.partial(_ffn_kernel, mixer),
        grid=(m // tm,),
        in_specs=mixer_specs + [tile(x), _resident((1, D_MODEL)), _resident(wgu.shape), _resident(wd.shape)],
        out_specs=tile(x),
        out_shape=jax.ShapeDtypeStruct((m, D_MODEL), F32),
        compiler_params=_params("parallel"),
        name="ffn" if mixer is None else "ffn_" + mixer,
    )(*mixer_args, x, g, wgu, wd)


def _hgrn_q_row(direction, hb, t):
    return ((t % (2 * hb)) >= hb) != (direction == 1)


def _hgrn_ref_row(direction, hb, t):
    return (t // (2 * hb)) * (2 * hb) + (hb - 1 if direction == 0 else hb)


def _hgrn_tables():
    c = HG_CHUNK
    t = np.arange(c)[:, None]
    s = np.arange(c)[None, :]
    tri = np.stack([s <= t, s >= t]).astype(np.float32)
    wide, full = [], []
    for d in range(2):
        wide_d, full_d = [], []
        for hb in HG_LEVELS:
            same = (t // (2 * hb)) == (s // (2 * hb))
            own = (_hgrn_q_row(d, hb, t) & same & ~_hgrn_q_row(d, hb, s)).astype(np.float32)
            if hb >= HG_SUBLANES:
                wide_d.append(own[_hgrn_q_row(d, hb, np.arange(c))])
            else:
                full_d.append(own)
        wide.append(np.stack(wide_d))
        full.append(np.stack(full_d))
    return tri, np.stack(wide), np.stack(full)


def _hgrn_proj_kernel(x_ref, g_ref, w_ref, llb_ref, l1m_ref, q_ref, lf_ref, lb_ref, v_ref, gate_ref):
    h = _rms(x_ref[...], g_ref[...]).astype(BF16)
    proj = lambda j: _dot(h, w_ref[:, j * D_MODEL:(j + 1) * D_MODEL])
    llb = llb_ref[...]
    l1m = l1m_ref[...]

    def log2_decay(z):
        ls = jnp.minimum(z, 0.0) - jnp.log(1.0 + jnp.exp2(jnp.abs(z) * -LOG2E))
        y = l1m + ls
        return (jnp.maximum(llb, y) + jnp.log(1.0 + jnp.exp2(jnp.abs(llb - y) * -LOG2E))) * LOG2E

    def put(o_ref, y):
        for p in range(HG_GROUPS):
            o_ref[0, p] = y[:, p * HG_WIDTH:(p + 1) * HG_WIDTH]

    put(q_ref, (proj(0) * HG_DK ** -0.5).astype(BF16))
    put(lf_ref, log2_decay(proj(1)))
    put(lb_ref, log2_decay(proj(2)))
    put(v_ref, proj(3).astype(BF16))
    gate_ref[...] = proj(4)


def _hgrn_grouped_tile(tm, t):
    per_seq = t // tm
    return pl.BlockSpec((1, HG_GROUPS, tm, HG_WIDTH), lambda i: (i // per_seq, 0, i % per_seq, 0))


def _hgrn_proj(x, g, w, llb, l1m, t):
    m = x.shape[0]
    tm = min(TOKEN_TILE, t)
    tile = lambda: pl.BlockSpec((tm, D_MODEL), lambda i: (i, 0))
    grouped = lambda dt: jax.ShapeDtypeStruct((m // t, HG_GROUPS, t, HG_WIDTH), dt)
    return pl.pallas_call(
        _hgrn_proj_kernel,
        grid=(m // tm,),
        in_specs=[tile(), _resident((1, D_MODEL)), _resident(w.shape),
                  _resident((1, D_MODEL)), _resident((1, D_MODEL))],
        out_specs=[_hgrn_grouped_tile(tm, t)] * 4 + [tile()],
        out_shape=[grouped(BF16), grouped(F32), grouped(F32), grouped(BF16),
                   jax.ShapeDtypeStruct((m, D_MODEL), F32)],
        compiler_params=_params("parallel"),
        name="hgrn_proj",
    )(x, g, w, llb, l1m)


def _hgrn_chunk(direction, q32, v, logf2, tri_ref, wide_ref, full_ref, cum_ref, st_ref):
    c = HG_CHUNK
    nv = c // HG_SUBLANES
    rows = lambda a, i: a[i * HG_SUBLANES:(i + 1) * HG_SUBLANES]
    f = jnp.exp2(logf2)
    k = 1.0 - f
    hi = logf2.astype(BF16)
    lo = (logf2 - hi.astype(F32)).astype(BF16)
    tri = tri_ref[direction]
    cum = _dot(tri, hi) + _dot(tri, lo)
    heads = HG_HEADS_PER_STEP
    hsl = [slice(h * HG_DK, (h + 1) * HG_DK) for h in range(heads)]
    for h in range(heads):
        cum_ref[direction, h] = cum[:, hsl[h]]
    yield
    ref_row = lambda r: jnp.concatenate([cum_ref[direction, h, r:r + 1, :] for h in range(heads)], axis=-1)
    ref_rows = lambda r: jnp.broadcast_to(ref_row(r), (HG_SUBLANES, q32.shape[1]))
    att = [[None] * nv for _ in range(heads)]

    def add_rows(h, i, piece):
        att[h][i] = piece if att[h][i] is None else att[h][i] + piece

    sub = lax.broadcasted_iota(jnp.int32, (c, q32.shape[1]), 0)
    sub8 = lax.broadcasted_iota(jnp.int32, (HG_SUBLANES, q32.shape[1]), 0)
    n_wide = 0
    for li, hb in enumerate(HG_LEVELS):
        if hb >= HG_SUBLANES:
            parts, q_idx = [], []
            for i in range(nv):
                t0 = i * HG_SUBLANES
                ref = ref_rows(int(_hgrn_ref_row(direction, hb, t0)))
                if bool(_hgrn_q_row(direction, hb, t0)):
                    parts.append(rows(q32, i) * jnp.exp2(rows(cum, i) - ref))
                    q_idx.append(i)
                else:
                    parts.append(rows(k, i) * jnp.exp2(ref - rows(cum, i)))
            w = jnp.concatenate(parts, axis=0).astype(BF16)
            wq = jnp.concatenate([parts[i] for i in q_idx], axis=0).astype(BF16)
            for h in range(heads):
                a = _dot_nt(wq[:, hsl[h]], w[:, hsl[h]]) * wide_ref[direction, n_wide]
                for j, i in enumerate(q_idx):
                    add_rows(h, i, rows(a, j))
            n_wide += 1
        else:
            q_rows = _hgrn_q_row(direction, hb, sub)
            if hb == 1:
                w = jnp.where(q_rows, q32 * f, k)
            else:
                pieces = []
                for i in range(nv):
                    t0 = i * HG_SUBLANES
                    if 2 * hb == HG_SUBLANES:
                        pieces.append(ref_rows(int(_hgrn_ref_row(direction, hb, t0))))
                    else:
                        lo_ref = ref_rows(int(_hgrn_ref_row(direction, hb, t0)))
                        hi_ref = ref_rows(int(_hgrn_ref_row(direction, hb, t0 + 2 * hb)))
                        pieces.append(jnp.where(sub8 < 2 * hb, lo_ref, hi_ref))
                d = cum - jnp.concatenate(pieces, axis=0)
                w = jnp.where(q_rows, q32, k) * jnp.exp2(jnp.where(q_rows, d, -d))
            w = w.astype(BF16)
            for h in range(heads):
                a = _dot_nt(w[:, hsl[h]], w[:, hsl[h]]) * full_ref[direction, li - n_wide]
                for i in range(nv):
                    add_rows(h, i, rows(a, i))
        yield

    last = c - 1 if direction == 0 else 0
    total = ref_row(last)
    qi = (q32 * jnp.exp2(cum)).astype(BF16)
    ki = (k * jnp.exp2(total - cum)).astype(BF16)
    diag_in = (q32 * k).astype(BF16)
    ones = jnp.ones((HG_DK, HG_DV), BF16)
    decay = jnp.exp2(total)
    outs = []
    for h in range(heads):
        vh = v[:, hsl[h]]
        st = st_ref[direction, h]
        o = _dot(jnp.concatenate(att[h], axis=0).astype(BF16), vh)
        o = o + _dot(diag_in[:, hsl[h]], ones) * vh.astype(F32)
        o = o + _dot_nt(qi[:, hsl[h]], st.astype(BF16))
        st_ref[direction, h] = st * decay[:, hsl[h]] + _dot_tn(vh, ki[:, hsl[h]])
        outs.append(o)
    return jnp.concatenate(outs, axis=-1)


def _hgrn_kernel(q_ref, lf_ref, lb_ref, v_ref, tri_ref, wide_ref, full_ref,
                 o_ref, ob_ref, cum_ref, st_ref):
    t = q_ref.shape[2]
    c = HG_CHUNK
    n = t // c
    st_ref[...] = jnp.zeros_like(st_ref)

    def chunk(direction, gate_ref, r0, cum_slot):
        return _hgrn_chunk(direction, q_ref[0, 0, pl.ds(r0, c), :].astype(F32), v_ref[0, 0, pl.ds(r0, c), :],
                           gate_ref[0, 0, pl.ds(r0, c), :], tri_ref, wide_ref, full_ref, cum_slot, st_ref)

    def body(i, carry):
        gens, dests = [], []
        for u in range(HG_UNROLL):
            rf = pl.multiple_of((i * HG_UNROLL + u) * c, c)
            rb = pl.multiple_of((n - 1 - i * HG_UNROLL - u) * c, c)
            gens += [chunk(0, lf_ref, rf, cum_ref.at[u]), chunk(1, lb_ref, rb, cum_ref.at[u])]
            dests += [(o_ref.at[0, 0], rf), (ob_ref, rb)]
        for (dst, r0), o in zip(dests, _interleave(gens)):
            dst[pl.ds(r0, c), :] = o
        return carry

    lax.fori_loop(0, n // HG_UNROLL, body, 0)
    o_ref[0, 0] = o_ref[0, 0] + ob_ref[...]


def _hgrn_scan(q, lf, lb, v):
    b, _, t, _ = q.shape
    tri, wide, full = _hgrn_tables()
    tri = jnp.asarray(tri, BF16)
    wide = jnp.asarray(wide, F32)
    full = jnp.asarray(full, F32)
    seq = lambda: pl.BlockSpec((1, 1, t, HG_WIDTH), lambda i, j: (i, j, 0, 0))
    return pl.pallas_call(
        _hgrn_kernel,
        grid=(b, HG_GROUPS),
        in_specs=[seq(), seq(), seq(), seq(),
                  _resident(tri.shape), _resident(wide.shape), _resident(full.shape)],
        out_specs=seq(),
        out_shape=jax.ShapeDtypeStruct((b, HG_GROUPS, t, HG_WIDTH), F32),
        scratch_shapes=[pltpu.VMEM((t, HG_WIDTH), F32),
                        pltpu.VMEM((HG_UNROLL, 2, HG_HEADS_PER_STEP, HG_CHUNK, HG_DK), F32),
                        pltpu.VMEM((2, HG_HEADS_PER_STEP, HG_DV, HG_DK), F32)],
        compiler_params=_params("parallel", "parallel"),
        name="hgrn_scan",
    )(q, lf, lb, v, tri, wide, full)


def _hgrn_mixer(x, mix_g, w_in, g_norm, w_out, lb):
    b, t, _ = x.shape
    m = b * t
    lb = lb.astype(F32).reshape(1, D_MODEL)
    q, lf, lbw, v, gate = _hgrn_proj(x.reshape(m, D_MODEL), mix_g, w_in, jnp.log(lb), jnp.log1p(-lb), t)
    return "hgrn", (_hgrn_scan(q, lf, lbw, v), gate, g_norm.reshape(1, HG_DV), w_out)


def _na_proj_kernel(x_ref, g_ref, w_ref, qg_ref, kg_ref, bd_ref, q_ref, k_ref, v_ref):
    h = _rms(x_ref[...], g_ref[...]).astype(BF16)
    bd = bd_ref[...]

    def head_norm(y, gain, mult):
        width = bd.shape[0]
        parts = []
        for j in range(D_MODEL // width):
            ys = y[:, j * width:(j + 1) * width]
            ss = _dot((ys * ys).astype(BF16), bd)
            parts.append(ys * lax.rsqrt(ss * (1.0 / NA_HEAD_DIM) + NORM_EPS) * (gain * mult))
        return jnp.concatenate(parts, axis=-1)

    q = _dot(h, w_ref[:, 0:D_MODEL])
    q_ref[...] = head_norm(q, qg_ref[...], NA_HEAD_DIM ** -0.5 * LOG2E).astype(BF16)
    k = _dot(h, w_ref[:, D_MODEL:2 * D_MODEL])
    k_ref[...] = head_norm(k, kg_ref[...], 1.0).astype(BF16)
    v_ref[...] = _dot(h, w_ref[:, 2 * D_MODEL:3 * D_MODEL]).astype(BF16)


def _na_proj(x, g, w, qg, kg):
    m = x.shape[0]
    tm = min(TOKEN_TILE, m)
    blk = np.arange(2 * LANES) // NA_HEAD_DIM
    bd = jnp.asarray(blk[:, None] == blk[None, :], BF16)
    tile = lambda: pl.BlockSpec((tm, D_MODEL), lambda i: (i, 0))
    return pl.pallas_call(
        _na_proj_kernel,
        grid=(m // tm,),
        in_specs=[tile(), _resident((1, D_MODEL)), _resident(w.shape),
                  _resident(qg.shape), _resident(kg.shape), _resident(bd.shape)],
        out_specs=[tile(), tile(), tile()],
        out_shape=[jax.ShapeDtypeStruct((m, D_MODEL), BF16)] * 3,
        compiler_params=_params("parallel"),
        name="na_proj",
    )(x, g, w, qg, kg, bd)


def _na_group_start(g, rows):
    return jnp.clip(g * NA_GROUP - NA_WIN_R // 2, 0, rows - NA_UNION) if isinstance(g, jax.Array) else \
        int(np.clip(g * NA_GROUP - NA_WIN_R // 2, 0, rows - NA_UNION))


def _na_attn_kernel(rows, q_ref, k_ref, v_ref, bias_ref, cm_ref, o_ref):
    gq = NA_GROUP * GRID_W
    uk = NA_UNION * GRID_W
    n_groups = rows // NA_GROUP
    lane = lax.broadcasted_iota(jnp.int32, (gq, LANES), 1)
    first = lane < NA_HEAD_DIM
    head_sel = (jnp.where(first[0:1], 1.0, 0.0).astype(BF16), jnp.where(first[0:1], 0.0, 1.0).astype(BF16))

    def group(g):
        kind = jnp.where(g == 0, 0, jnp.where(g == n_groups - 1, 2, 1))
        qs = pl.multiple_of(g * gq, gq)
        ks = pl.multiple_of(_na_group_start(g, rows) * GRID_W, NA_GROUP * GRID_W)
        q = q_ref[0, pl.ds(qs, gq), :]
        kb = k_ref[0, pl.ds(ks, uk), :]
        vb = v_ref[0, pl.ds(ks, uk), :]
        s = _dot_nt(jnp.concatenate([q * head_sel[0], q * head_sel[1]], axis=0), kb)
        yield
        cm = cm_ref[kind]
        probs = []
        for hh in range(2):
            sh = s[hh * gq:(hh + 1) * gq] * cm + bias_ref[hh, kind]
            probs.append(jnp.exp2((sh - jnp.max(sh, axis=-1, keepdims=True)).astype(BF16)))
            yield
        v_ext = jnp.concatenate([vb, jnp.ones_like(vb)], axis=-1)
        o2 = _dot(jnp.concatenate(probs, axis=0), v_ext)
        o2 = o2[:, :LANES] * (1.0 / o2[:, LANES:])
        o_ref[0, pl.ds(qs, gq), :] = jnp.where(first, o2[:gq], o2[gq:]).astype(o_ref.dtype)

    def body(i, carry):
        _interleave([group(i * NA_UNROLL + u) for u in range(NA_UNROLL)])
        return carry

    lax.fori_loop(0, n_groups // NA_UNROLL, body, 0)


def _na_attn(q, k, v, bias, cm):
    b, t, _ = q.shape
    rows = t // GRID_W
    seq = lambda: pl.BlockSpec((1, t, LANES), lambda j, i: (i, 0, j))
    return pl.pallas_call(
        functools.partial(_na_attn_kernel, rows),
        grid=(D_MODEL // LANES, b),
        in_specs=[seq(), seq(), seq(),
                  pl.BlockSpec((2,) + bias.shape[1:], lambda j, i: (j, 0, 0, 0)),
                  _resident(cm.shape)],
        out_specs=seq(),
        out_shape=jax.ShapeDtypeStruct((b, t, D_MODEL), BF16),
        compiler_params=_params("parallel", "parallel"),
        name="na_attn",
    )(q, k, v, bias, cm)


def _na_tables(rpb, rows):
    assert rows % NA_GROUP == 0 and rows >= NA_UNION + NA_GROUP
    cols = np.arange(GRID_W)
    col_start = np.clip(cols - NA_WIN_C // 2, 0, GRID_W - NA_WIN_C)
    col_mask = (cols[None, :] >= col_start[:, None]) & (cols[None, :] < col_start[:, None] + NA_WIN_C)
    col_off = np.clip(cols[None, :] - cols[:, None] + NA_WIN_C - 1, 0, 2 * NA_WIN_C - 2)
    n_groups = rows // NA_GROUP
    by_col = rpb.astype(F32)[:, :, col_off].transpose(0, 2, 1, 3)
    pad = NA_UNION
    by_col = jnp.pad(by_col, ((0, 0), (0, 0), (pad, pad), (0, 0)))
    valid, kinds = [], []
    for g in (0, 1, n_groups - 1):
        start = _na_group_start(g, rows)
        key = start + np.arange(NA_UNION)[None, :]
        r = g * NA_GROUP + np.arange(NA_GROUP)[:, None]
        r0 = np.clip(r - NA_WIN_R // 2, 0, rows - NA_WIN_R)
        valid.append((key >= r0) & (key < r0 + NA_WIN_R))
        per_row = []
        for a in range(NA_GROUP):
            off = start - (g * NA_GROUP + a) + NA_WIN_R - 1 + pad
            per_row.append(by_col[:, :, off:off + NA_UNION, :])
        kinds.append(jnp.stack(per_row, axis=1))
    valid = np.stack(valid)
    assert (valid.sum(-1) == NA_WIN_R).all()
    full_valid = valid[:, :, None, :, None] & col_mask[None, None, :, None, :]
    shape = (3, NA_GROUP * GRID_W, NA_UNION * GRID_W)
    tab = jnp.stack(kinds, axis=1) * LOG2E
    tab = jnp.where(full_valid[None], tab, NEG_INF * LOG2E).reshape((NA_HEADS,) + shape)
    return tab, jnp.asarray(full_valid.reshape(shape), F32)


def _na_mixer(x, mix_g, w_in, q_norm, k_norm, rpb, w_out):
    b, t, _ = x.shape
    m = b * t
    tile2 = lambda g: jnp.tile(g.astype(F32), 2 * LANES // NA_HEAD_DIM).reshape(1, 2 * LANES)
    q, k, v = _na_proj(x.reshape(m, D_MODEL), mix_g, w_in, tile2(q_norm), tile2(k_norm))
    bias, cm = _na_tables(rpb, t // GRID_W)
    r3 = lambda a: a.reshape(b, t, D_MODEL)
    o = _na_attn(r3(q), r3(k), r3(v), bias, cm)
    return "proj", (o.reshape(m, D_MODEL), w_out)


_MLA_PERM = np.concatenate([
    MLA_NOPE + np.arange(16),
    np.arange(48),
    MLA_NOPE + 16 + np.arange(16),
    48 + np.arange(16),
    -np.ones(32, np.int64),
]).astype(np.int64)


def _mla_prep_kernel(x_ref, g_ref, win_ref, qa_ref, kva_ref, wuq_ref, wuk_ref, wuv_ref,
                     swap_ref, bd_ref, aq_ref, bq_ref, ak_ref, bk_ref, q_ref, k_ref, v_ref):
    h = _rms(x_ref[...], g_ref[...]).astype(BF16)
    c = _dot(h, win_ref[...])
    cq = _rms(c[:, :MLA_Q_LORA], qa_ref[...]).astype(BF16)
    ckv = _rms(c[:, MLA_Q_LORA:MLA_Q_LORA + MLA_KV_LORA], kva_ref[...]).astype(BF16)
    k_rope = c[:, MLA_Q_LORA + MLA_KV_LORA:]
    two = lambda a: jnp.concatenate([a, a], axis=-1)
    aq, bq, ak = two(aq_ref[...]), two(bq_ref[...]), two(ak_ref[...])
    k_rope2 = two(k_rope)
    k_partner = two(pltpu.roll(k_rope, LANES // 2, 1) * bk_ref[...])
    bd = bd_ref[...]

    def inv_rms(y):
        ss = _dot((y * y).astype(BF16), bd)
        return lax.rsqrt(ss * (1.0 / MLA_QK) + NORM_EPS)

    for p in range(MLA_HEADS // 2):
        sl = slice(p * 2 * LANES, (p + 1) * 2 * LANES)
        yq = _dot(cq, wuq_ref[:, sl])
        yq_partner = _dot(yq.astype(BF16), swap_ref[...])
        q_ref[:, sl] = ((yq * aq + yq_partner * bq) * inv_rms(yq)).astype(BF16)
        yk = _dot(ckv, wuk_ref[:, sl]) + k_rope2
        k_ref[:, sl] = ((yk * ak + k_partner) * inv_rms(yk)).astype(BF16)
    v_ref[...] = _dot(ckv, wuv_ref[...]).astype(BF16)


def _mla_prep(x, g, win, qa, kva, wuq, wuk, wuv, swap, bd, aq, bq, ak, bk, t):
    m = x.shape[0]
    tm = min(TOKEN_TILE, t)
    per_seq = t // tm
    wide = MLA_HEADS * LANES
    table = lambda: pl.BlockSpec((tm, LANES), lambda i: (i % per_seq, 0))
    return pl.pallas_call(
        _mla_prep_kernel,
        grid=(m // tm,),
        in_specs=[pl.BlockSpec((tm, D_MODEL), lambda i: (i, 0)),
                  _resident((1, D_MODEL)), _resident(win.shape),
                  _resident((1, MLA_Q_LORA)), _resident((1, MLA_KV_LORA)),
                  _resident(wuq.shape), _resident(wuk.shape), _resident(wuv.shape),
                  _resident(swap.shape), _resident(bd.shape),
                  table(), table(), table(), table()],
        out_specs=[pl.BlockSpec((tm, wide), lambda i: (i, 0)),
                   pl.BlockSpec((tm, wide), lambda i: (i, 0)),
                   pl.BlockSpec((tm, D_MODEL), lambda i: (i, 0))],
        out_shape=[jax.ShapeDtypeStruct((m, wide), BF16),
                   jax.ShapeDtypeStruct((m, wide), BF16),
                   jax.ShapeDtypeStruct((m, D_MODEL), BF16)],
        compiler_params=_params("parallel"),
        name="mla_prep",
    )(x, g, win, qa, kva, wuq, wuk, wuv, swap, bd, aq, bq, ak, bk)


def _mla_attn_kernel(q_ref, k_ref, v_ref, o_ref):
    sub = MLA_Q_SUBTILE
    first = lax.broadcasted_iota(jnp.int32, (sub, LANES), 1) < MLA_V
    v_ext = jnp.concatenate([v_ref[0], jnp.ones((v_ref.shape[1], LANES), BF16)], axis=-1)

    def head(r0, hh):
        sl = slice(hh * LANES, (hh + 1) * LANES)
        s = _dot_nt(q_ref[0, r0:r0 + sub, sl], k_ref[0, :, sl])
        yield
        p = jnp.exp2((s - jnp.max(s, axis=-1, keepdims=True)).astype(BF16))
        yield
        o = _dot(p, v_ext)
        return o[:, :LANES] * (1.0 / o[:, LANES:])

    starts = range(0, q_ref.shape[1], sub)
    outs = _interleave([head(r0, hh) for r0 in starts for hh in range(2)])
    for n, r0 in enumerate(starts):
        o_ref[0, r0:r0 + sub, :] = jnp.where(first, outs[2 * n], outs[2 * n + 1]).astype(o_ref.dtype)


def _mla_attn(q, k, v):
    b, t, _ = q.shape
    tq = min(MLA_Q_TILE, t)
    return pl.pallas_call(
        _mla_attn_kernel,
        grid=(b, MLA_HEADS // 2, t // tq),
        in_specs=[pl.BlockSpec((1, tq, 2 * LANES), lambda i, j, l: (i, l, j)),
                  pl.BlockSpec((1, t, 2 * LANES), lambda i, j, l: (i, 0, j)),
                  pl.BlockSpec((1, t, LANES), lambda i, j, l: (i, 0, j))],
        out_specs=pl.BlockSpec((1, tq, LANES), lambda i, j, l: (i, l, j)),
        out_shape=jax.ShapeDtypeStruct((b, t, D_MODEL), BF16),
        compiler_params=_params("parallel", "parallel", "arbitrary"),
        name="mla_attn",
    )(q, k, v)


def _mla_weights(w_in, w_uq, w_ukv, q_norm, k_norm, t):
    valid = _MLA_PERM >= 0
    src = np.where(valid, _MLA_PERM, 0)
    is_rope = valid & (_MLA_PERM >= MLA_NOPE)
    is_nope = valid & (_MLA_PERM < MLA_NOPE)

    def place(w, per_head, lanes_ok, base=0):
        wh = w.reshape(w.shape[0], MLA_HEADS, per_head)[:, :, base + np.where(lanes_ok, src, 0)]
        return jnp.where(lanes_ok[None, None, :], wh, 0.0).reshape(w.shape[0], MLA_HEADS * LANES)

    wuq = place(w_uq, MLA_QK, valid)
    wuk = place(w_ukv, MLA_NOPE + MLA_V, is_nope)
    wuv = w_ukv.reshape(MLA_KV_LORA, MLA_HEADS, MLA_NOPE + MLA_V)[:, :, MLA_NOPE:].reshape(
        MLA_KV_LORA, MLA_HEADS * MLA_V)
    rope_cols = w_in[:, MLA_Q_LORA + MLA_KV_LORA:]
    rope_placed = jnp.where(is_rope[None, :], rope_cols[:, np.where(is_rope, src - MLA_NOPE, 0)], 0.0)
    win = jnp.concatenate([w_in[:, :MLA_Q_LORA + MLA_KV_LORA], rope_placed], axis=1)
    gain = lambda g: jnp.where(valid, g.astype(F32)[src], 0.0).reshape(1, LANES)
    partner = (np.arange(LANES) + LANES // 2) % LANES
    gain_partner = lambda g: jnp.where(is_rope, gain(g)[0, partner], 0.0).reshape(1, LANES)

    half = MLA_ROPE // 2
    inv_freq = ROPE_THETA ** (-jnp.arange(half, dtype=F32) / half)
    ang = jnp.arange(t).astype(F32)[:, None] * inv_freq[None, :]
    cos = jnp.ones((t, LANES), F32)
    cos = cos.at[:, 0:half].set(jnp.cos(ang)).at[:, 64:64 + half].set(jnp.cos(ang))
    sin = jnp.zeros((t, LANES), F32)
    sin = sin.at[:, 0:half].set(-jnp.sin(ang)).at[:, 64:64 + half].set(jnp.sin(ang))
    q_mult = MLA_QK ** -0.5 * LOG2E
    tables = (gain(q_norm) * cos * q_mult, gain_partner(q_norm) * sin * q_mult,
              gain(k_norm) * cos, gain_partner(k_norm) * sin)

    lane2 = np.arange(2 * LANES)
    rope2 = np.tile(is_rope, 2)
    swap = (lane2[:, None] == (lane2[None, :] // LANES) * LANES + np.tile(partner, 2)[None, :]) & rope2[None, :]
    bd = (lane2[:, None] // LANES) == (lane2[None, :] // LANES)
    return (win.astype(BF16), wuq.astype(BF16), wuk.astype(BF16), wuv.astype(BF16),
            jnp.asarray(swap, BF16), jnp.asarray(bd, BF16)) + tables


def _mla_mixer(x, mix_g, w_in, q_a_norm, w_uq, kv_a_norm, w_ukv, q_norm, k_norm, w_out):
    b, t, _ = x.shape
    m = b * t
    prepared = _mla_weights(w_in, w_uq, w_ukv, q_norm, k_norm, t)
    q, k, v = _mla_prep(x.reshape(m, D_MODEL), mix_g, prepared[0], q_a_norm.astype(F32).reshape(1, -1),
                        kv_a_norm.astype(F32).reshape(1, -1), *prepared[1:], t)
    o = _mla_attn(q.reshape(b, t, -1), k.reshape(b, t, -1), v.reshape(b, t, -1))
    return "proj", (o.reshape(m, D_MODEL), w_out)


def kernel(x, ffn1_norm, ffn1_w_gu, ffn1_w_down, mix_norm, ffn2_norm, ffn2_w_gu, ffn2_w_down,
           hg_lb_logits, hg_w_in, hg_g_norm, hg_w_out,
           na_w_in, na_q_norm, na_k_norm, na_rpb, na_w_out,
           mla_w_in, mla_q_a_norm, mla_w_uq, mla_kv_a_norm, mla_w_ukv, mla_q_norm, mla_k_norm, mla_w_out):
    b, t, d = x.shape
    m = b * t
    gam = jnp.cumsum(jax.nn.softmax(hg_lb_logits.astype(F32), axis=0), axis=0)
    lb_all = gam - gam[0:1]
    row = lambda g: g.astype(F32).reshape(1, -1)
    bf = lambda w: w.astype(BF16)
    ia = ib = ic = 0
    for layer in range(DEPTH):
        x = _ffn(x.reshape(m, d), row(ffn1_norm[layer]), bf(ffn1_w_gu[layer]),
                 bf(ffn1_w_down[layer])).reshape(b, t, d)
        g = row(mix_norm[layer])
        kind = layer % N_MIXERS
        if kind == 0:
            mixer = _hgrn_mixer(x, g, bf(hg_w_in[ia]), hg_g_norm[ia].astype(F32), bf(hg_w_out[ia]), lb_all[layer])
            ia += 1
        elif kind == 1:
            mixer = _na_mixer(x, g, bf(na_w_in[ib]), na_q_norm[ib], na_k_norm[ib], na_rpb[ib], bf(na_w_out[ib]))
            ib += 1
        else:
            mixer = _mla_mixer(x, g, mla_w_in[ic], mla_q_a_norm[ic], mla_w_uq[ic], mla_kv_a_norm[ic],
                               mla_w_ukv[ic], mla_q_norm[ic], mla_k_norm[ic], bf(mla_w_out[ic]))
            ic += 1
        x = _ffn(x.reshape(m, d), row(ffn2_norm[layer]), bf(ffn2_w_gu[layer]),
                 bf(ffn2_w_down[layer]), *mixer).reshape(b, t, d)
    return x
```

```python
import functools

import numpy as np
import jax
import jax.numpy as jnp
from jax import lax
from jax.experimental import pallas as pl
from jax.experimental.pallas import tpu as pltpu

D_MODEL = 1024
DEPTH = 4
N_MIXERS = 3
GRID_W = 64
D_FF = 2816
NORM_EPS = 1e-6
HG_HEADS = 8
HG_DK = 128
HG_DV = 128
NA_HEADS = 16
NA_HEAD_DIM = 64
NA_WIN_R = 8
NA_WIN_C = 16
MLA_HEADS = 16
MLA_Q_LORA = 768
MLA_KV_LORA = 256
MLA_NOPE = 64
MLA_ROPE = 32
MLA_V = 64
MLA_QK = MLA_NOPE + MLA_ROPE
ROPE_THETA = 10000.0
NEG_INF = -1e30
LOG2E = 1.4426950408889634

LANES = 128
VMEM_LIMIT = 56 * 1024 * 1024
TOKEN_TILE = 512
HG_CHUNK = 128
HG_LEVELS = (64, 32, 16, 8, 4, 2, 1)
HG_SUBLANES = 8
HG_HEADS_PER_STEP = 2
HG_UNROLL = 4
MLA_Q_TILE = 512
MLA_Q_SUBTILE = 512
NA_GROUP = 4
NA_UNION = NA_WIN_R + NA_GROUP
NA_UNROLL = 4
F32 = jnp.float32
BF16 = jnp.bfloat16


def _params(*sem):
    return pltpu.CompilerParams(dimension_semantics=sem, vmem_limit_bytes=VMEM_LIMIT)


def _resident(shape):
    nd = len(shape)
    return pl.BlockSpec(shape, lambda *_: (0,) * nd, pipeline_mode=pl.Buffered(1))


def _rms(x, g):
    ms = jnp.mean(x * x, axis=-1, keepdims=True)
    return x * lax.rsqrt(ms + NORM_EPS) * g


def _silu(x):
    return x * (1.0 / (1.0 + jnp.exp(-x)))


def _dot(a, b):
    return jnp.dot(a, b, preferred_element_type=F32)


def _dot_nt(a, b):
    return lax.dot_general(a, b, (((1,), (1,)), ((), ())), preferred_element_type=F32)


def _interleave(gens):
    results = [None] * len(gens)
    pending = set(range(len(gens)))
    while pending:
        for i in sorted(pending):
            try:
                next(gens[i])
            except StopIteration as stop:
                results[i] = stop.value
                pending.discard(i)
    return results


def _dot_tn(a, b):
    return lax.dot_general(a, b, (((0,), (0,)), ((), ())), preferred_element_type=F32)


FF_CHUNKS = ((0, 1536), (1536, 2816))


def _ffn_body(x, g, wgu_ref, wd_ref):
    h = _rms(x, g).astype(BF16)
    acc = None
    for s, e in FF_CHUNKS:
        gate = _dot(h, wgu_ref[:, s:e])
        up = _dot(h, wgu_ref[:, D_FF + s:D_FF + e])
        a = (_silu(gate) * up).astype(BF16)
        d = _dot(a, wd_ref[s:e, :])
        acc = d if acc is None else acc + d
    return x + 0.5 * acc


def _hgrn_gated(o_ref, gate_ref, gn):
    parts = []
    for h in range(HG_HEADS):
        sl = slice(h * HG_DV, (h + 1) * HG_DV)
        parts.append(_rms(o_ref[:, sl], gn) * _silu(gate_ref[:, sl]))
    return jnp.concatenate(parts, axis=-1).astype(BF16)


def _ffn_kernel(mixer, *refs):
    *mixer_refs, x_ref, g_ref, wgu_ref, wd_ref, o_ref = refs
    x = x_ref[...]
    if mixer == "proj":
        a_ref, w_ref = mixer_refs
        x = x + _dot(a_ref[...], w_ref[...])
    elif mixer == "hgrn":
        s_ref, gate_ref, gn_ref, w_ref = mixer_refs
        x = x + _dot(_hgrn_gated(s_ref, gate_ref, gn_ref[...]), w_ref[...])
    o_ref[...] = _ffn_body(x, g_ref[...], wgu_ref, wd_ref)


def _ffn(x, g, wgu, wd, mixer=None, mixer_args=()):
    m = x.shape[0]
    tm = min(TOKEN_TILE, m)
    tile = lambda a: pl.BlockSpec((tm, a.shape[1]), lambda i: (i, 0))
    n_tiled = {None: 0, "proj": 1, "hgrn": 2}[mixer]
    mixer_specs = [tile(a) if i < n_tiled else _resident(a.shape) for i, a in enumerate(mixer_args)]
    return pl.pallas_call(
        functools.partial(_ffn_kernel, mixer),
        grid=(m // tm,),
        in_specs=mixer_specs + [tile(x), _resident((1, D_MODEL)), _resident(wgu.shape), _resident(wd.shape)],
        out_specs=tile(x),
        out_shape=jax.ShapeDtypeStruct((m, D_MODEL), F32),
        compiler_params=_params("parallel"),
        name="ffn" if mixer is None else "ffn_" + mixer,
    )(*mixer_args, x, g, wgu, wd)


def _hgrn_q_row(direction, hb, t):
    return ((t % (2 * hb)) >= hb) != (direction == 1)


def _hgrn_ref_row(direction, hb, t):
    return (t // (2 * hb)) * (2 * hb) + (hb - 1 if direction == 0 else hb)


def _hgrn_tables():
    c = HG_CHUNK
    t = np.arange(c)[:, None]
    s = np.arange(c)[None, :]
    tri = np.stack([s <= t, s >= t]).astype(np.float32)
    wide, full = [], []
    for d in range(2):
        wide_d, full_d = [], []
        for hb in HG_LEVELS:
            same = (t // (2 * hb)) == (s // (2 * hb))
            own = (_hgrn_q_row(d, hb, t) & same & ~_hgrn_q_row(d, hb, s)).astype(np.float32)
            if hb >= HG_SUBLANES:
                wide_d.append(own[_hgrn_q_row(d, hb, np.arange(c))])
            else:
                full_d.append(own)
        wide.append(np.stack(wide_d))
        full.append(np.stack(full_d))
    return tri, np.stack(wide), np.stack(full)


def _hgrn_proj_kernel(x_ref, g_ref, w_ref, llb_ref, l1m_ref, q_ref, lf_ref, lb_ref, v_ref, gate_ref):
    h = _rms(x_ref[...], g_ref[...]).astype(BF16)
    proj = lambda j: _dot(h, w_ref[:, j * D_MODEL:(j + 1) * D_MODEL])
    llb = llb_ref[...]
    l1m = l1m_ref[...]

    def log2_decay(z):
        ls = jnp.minimum(z, 0.0) - jnp.log(1.0 + jnp.exp2(jnp.abs(z) * -LOG2E))
        y = l1m + ls
        return (jnp.maximum(llb, y) + jnp.log(1.0 + jnp.exp2(jnp.abs(llb - y) * -LOG2E))) * LOG2E

    q_ref[...] = (proj(0) * HG_DK ** -0.5).astype(BF16)
    lf_ref[...] = log2_decay(proj(1))
    lb_ref[...] = log2_decay(proj(2))
    v_ref[...] = proj(3).astype(BF16)
    gate_ref[...] = proj(4)


def _hgrn_proj(x, g, w, llb, l1m):
    m = x.shape[0]
    tm = min(TOKEN_TILE, m)
    tile = lambda: pl.BlockSpec((tm, D_MODEL), lambda i: (i, 0))
    return pl.pallas_call(
        _hgrn_proj_kernel,
        grid=(m // tm,),
        in_specs=[tile(), _resident((1, D_MODEL)), _resident(w.shape),
                  _resident((1, D_MODEL)), _resident((1, D_MODEL))],
        out_specs=[tile()] * 5,
        out_shape=[jax.ShapeDtypeStruct((m, D_MODEL), dt) for dt in (BF16, F32, F32, BF16, F32)],
        compiler_params=_params("parallel"),
        name="hgrn_proj",
    )(x, g, w, llb, l1m)


def _hgrn_chunk(direction, q32, v, logf2, tri_ref, wide_ref, full_ref, cum_ref, st_ref):
    c = HG_CHUNK
    nv = c // HG_SUBLANES
    rows = lambda a, i: a[i * HG_SUBLANES:(i + 1) * HG_SUBLANES]
    f = jnp.exp2(logf2)
    k = 1.0 - f
    hi = logf2.astype(BF16)
    lo = (logf2 - hi.astype(F32)).astype(BF16)
    tri = tri_ref[direction]
    cum = _dot(tri, hi) + _dot(tri, lo)
    heads = HG_HEADS_PER_STEP
    hsl = [slice(h * HG_DK, (h + 1) * HG_DK) for h in range(heads)]
    for h in range(heads):
        cum_ref[direction, h] = cum[:, hsl[h]]
    yield
    ref_row = lambda r: jnp.concatenate([cum_ref[direction, h, r:r + 1, :] for h in range(heads)], axis=-1)
    ref_rows = lambda r: jnp.broadcast_to(ref_row(r), (HG_SUBLANES, q32.shape[1]))
    att = [[None] * nv for _ in range(heads)]

    def add_rows(h, i, piece):
        att[h][i] = piece if att[h][i] is None else att[h][i] + piece

    sub = lax.broadcasted_iota(jnp.int32, (c, q32.shape[1]), 0)
    sub8 = lax.broadcasted_iota(jnp.int32, (HG_SUBLANES, q32.shape[1]), 0)
    n_wide = 0
    for li, hb in enumerate(HG_LEVELS):
        if hb >= HG_SUBLANES:
            parts, q_idx = [], []
            for i in range(nv):
                t0 = i * HG_SUBLANES
                ref = ref_rows(int(_hgrn_ref_row(direction, hb, t0)))
                if bool(_hgrn_q_row(direction, hb, t0)):
                    parts.append(rows(q32, i) * jnp.exp2(rows(cum, i) - ref))
                    q_idx.append(i)
                else:
                    parts.append(rows(k, i) * jnp.exp2(ref - rows(cum, i)))
            w = jnp.concatenate(parts, axis=0).astype(BF16)
            wq = jnp.concatenate([parts[i] for i in q_idx], axis=0).astype(BF16)
            for h in range(heads):
                a = _dot_nt(wq[:, hsl[h]], w[:, hsl[h]]) * wide_ref[direction, n_wide]
                for j, i in enumerate(q_idx):
                    add_rows(h, i, rows(a, j))
            n_wide += 1
        else:
            q_rows = _hgrn_q_row(direction, hb, sub)
            if hb == 1:
                w = jnp.where(q_rows, q32 * f, k)
            else:
                pieces = []
                for i in range(nv):
                    t0 = i * HG_SUBLANES
                    if 2 * hb == HG_SUBLANES:
                        pieces.append(ref_rows(int(_hgrn_ref_row(direction, hb, t0))))
                    else:
                        lo_ref = ref_rows(int(_hgrn_ref_row(direction, hb, t0)))
                        hi_ref = ref_rows(int(_hgrn_ref_row(direction, hb, t0 + 2 * hb)))
                        pieces.append(jnp.where(sub8 < 2 * hb, lo_ref, hi_ref))
                d = cum - jnp.concatenate(pieces, axis=0)
                w = jnp.where(q_rows, q32, k) * jnp.exp2(jnp.where(q_rows, d, -d))
            w = w.astype(BF16)
            for h in range(heads):
                a = _dot_nt(w[:, hsl[h]], w[:, hsl[h]]) * full_ref[direction, li - n_wide]
                for i in range(nv):
                    add_rows(h, i, rows(a, i))
        yield

    last = c - 1 if direction == 0 else 0
    total = ref_row(last)
    qi = (q32 * jnp.exp2(cum)).astype(BF16)
    ki = (k * jnp.exp2(total - cum)).astype(BF16)
    diag_in = (q32 * k).astype(BF16)
    ones = jnp.ones((HG_DK, HG_DV), BF16)
    decay = jnp.exp2(total)
    outs = []
    for h in range(heads):
        vh = v[:, hsl[h]]
        st = st_ref[direction, h]
        o = _dot(jnp.concatenate(att[h], axis=0).astype(BF16), vh)
        o = o + _dot(diag_in[:, hsl[h]], ones) * vh.astype(F32)
        o = o + _dot_nt(qi[:, hsl[h]], st.astype(BF16))
        st_ref[direction, h] = st * decay[:, hsl[h]] + _dot_tn(vh, ki[:, hsl[h]])
        outs.append(o)
    return jnp.concatenate(outs, axis=-1)


def _hgrn_kernel(q_ref, lf_ref, lb_ref, v_ref, tri_ref, wide_ref, full_ref,
                 o_ref, ob_ref, cum_ref, st_ref):
    t = q_ref.shape[1]
    c = HG_CHUNK
    n = t // c
    st_ref[...] = jnp.zeros_like(st_ref)

    def chunk(direction, gate_ref, r0, cum_slot):
        return _hgrn_chunk(direction, q_ref[0, pl.ds(r0, c), :].astype(F32), v_ref[0, pl.ds(r0, c), :],
                           gate_ref[0, pl.ds(r0, c), :], tri_ref, wide_ref, full_ref, cum_slot, st_ref)

    def body(i, carry):
        gens, dests = [], []
        for u in range(HG_UNROLL):
            rf = pl.multiple_of((i * HG_UNROLL + u) * c, c)
            rb = pl.multiple_of((n - 1 - i * HG_UNROLL - u) * c, c)
            gens += [chunk(0, lf_ref, rf, cum_ref.at[u]), chunk(1, lb_ref, rb, cum_ref.at[u])]
            dests += [(o_ref.at[0], rf), (ob_ref, rb)]
        for (dst, r0), o in zip(dests, _interleave(gens)):
            dst[pl.ds(r0, c), :] = o
        return carry

    lax.fori_loop(0, n // HG_UNROLL, body, 0)
    o_ref[0] = o_ref[0] + ob_ref[...]


def _hgrn_scan(q, lf, lb, v):
    b, t, _ = q.shape
    tri, wide, full = _hgrn_tables()
    tri = jnp.asarray(tri, BF16)
    wide = jnp.asarray(wide, F32)
    full = jnp.asarray(full, F32)
    width = HG_HEADS_PER_STEP * HG_DK
    seq = lambda: pl.BlockSpec((1, t, width), lambda i, j: (i, 0, j))
    return pl.pallas_call(
        _hgrn_kernel,
        grid=(b, HG_HEADS // HG_HEADS_PER_STEP),
        in_specs=[seq(), seq(), seq(), seq(),
                  _resident(tri.shape), _resident(wide.shape), _resident(full.shape)],
        out_specs=seq(),
        out_shape=jax.ShapeDtypeStruct((b, t, D_MODEL), F32),
        scratch_shapes=[pltpu.VMEM((t, width), F32),
                        pltpu.VMEM((HG_UNROLL, 2, HG_HEADS_PER_STEP, HG_CHUNK, HG_DK), F32),
                        pltpu.VMEM((2, HG_HEADS_PER_STEP, HG_DV, HG_DK), F32)],
        compiler_params=_params("parallel", "parallel"),
        name="hgrn_scan",
    )(q, lf, lb, v, tri, wide, full)


def _hgrn_mixer(x, mix_g, w_in, g_norm, w_out, lb):
    b, t, _ = x.shape
    m = b * t
    lb = lb.astype(F32).reshape(1, D_MODEL)
    q, lf, lbw, v, gate = _hgrn_proj(x.reshape(m, D_MODEL), mix_g, w_in, jnp.log(lb), jnp.log1p(-lb))
    r3 = lambda a: a.reshape(b, t, D_MODEL)
    o = _hgrn_scan(r3(q), r3(lf), r3(lbw), r3(v))
    return "hgrn", (o.reshape(m, D_MODEL), gate, g_norm.reshape(1, HG_DV), w_out)


def _na_proj_kernel(x_ref, g_ref, w_ref, qg_ref, kg_ref, bd_ref, q_ref, k_ref, v_ref):
    h = _rms(x_ref[...], g_ref[...]).astype(BF16)
    bd = bd_ref[...]

    def head_norm(y, gain, mult):
        width = bd.shape[0]
        parts = []
        for j in range(D_MODEL // width):
            ys = y[:, j * width:(j + 1) * width]
            ss = _dot((ys * ys).astype(BF16), bd)
            parts.append(ys * lax.rsqrt(ss * (1.0 / NA_HEAD_DIM) + NORM_EPS) * (gain * mult))
        return jnp.concatenate(parts, axis=-1)

    q = _dot(h, w_ref[:, 0:D_MODEL])
    q_ref[...] = head_norm(q, qg_ref[...], NA_HEAD_DIM ** -0.5 * LOG2E).astype(BF16)
    k = _dot(h, w_ref[:, D_MODEL:2 * D_MODEL])
    k_ref[...] = head_norm(k, kg_ref[...], 1.0).astype(BF16)
    v_ref[...] = _dot(h, w_ref[:, 2 * D_MODEL:3 * D_MODEL]).astype(BF16)


def _na_proj(x, g, w, qg, kg):
    m = x.shape[0]
    tm = min(TOKEN_TILE, m)
    blk = np.arange(2 * LANES) // NA_HEAD_DIM
    bd = jnp.asarray(blk[:, None] == blk[None, :], BF16)
    tile = lambda: pl.BlockSpec((tm, D_MODEL), lambda i: (i, 0))
    return pl.pallas_call(
        _na_proj_kernel,
        grid=(m // tm,),
        in_specs=[tile(), _resident((1, D_MODEL)), _resident(w.shape),
                  _resident(qg.shape), _resident(kg.shape), _resident(bd.shape)],
        out_specs=[tile(), tile(), tile()],
        out_shape=[jax.ShapeDtypeStruct((m, D_MODEL), BF16)] * 3,
        compiler_params=_params("parallel"),
        name="na_proj",
    )(x, g, w, qg, kg, bd)


def _na_group_start(g, rows):
    return jnp.clip(g * NA_GROUP - NA_WIN_R // 2, 0, rows - NA_UNION) if isinstance(g, jax.Array) else \
        int(np.clip(g * NA_GROUP - NA_WIN_R // 2, 0, rows - NA_UNION))


def _na_attn_kernel(rows, q_ref, k_ref, v_ref, bias_ref, cm_ref, o_ref):
    gq = NA_GROUP * GRID_W
    uk = NA_UNION * GRID_W
    n_groups = rows // NA_GROUP
    lane = lax.broadcasted_iota(jnp.int32, (gq, LANES), 1)
    first = lane < NA_HEAD_DIM
    head_sel = (jnp.where(first[0:1], 1.0, 0.0).astype(BF16), jnp.where(first[0:1], 0.0, 1.0).astype(BF16))

    def group(g):
        kind = jnp.where(g == 0, 0, jnp.where(g == n_groups - 1, 2, 1))
        qs = pl.multiple_of(g * gq, gq)
        ks = pl.multiple_of(_na_group_start(g, rows) * GRID_W, NA_GROUP * GRID_W)
        q = q_ref[0, pl.ds(qs, gq), :]
        kb = k_ref[0, pl.ds(ks, uk), :]
        vb = v_ref[0, pl.ds(ks, uk), :]
        s = _dot_nt(jnp.concatenate([q * head_sel[0], q * head_sel[1]], axis=0), kb)
        yield
        cm = cm_ref[kind]
        probs = []
        for hh in range(2):
            sh = s[hh * gq:(hh + 1) * gq] * cm + bias_ref[hh, kind]
            probs.append(jnp.exp2((sh - jnp.max(sh, axis=-1, keepdims=True)).astype(BF16)))
            yield
        v_ext = jnp.concatenate([vb, jnp.ones_like(vb)], axis=-1)
        o2 = _dot(jnp.concatenate(probs, axis=0), v_ext)
        o2 = o2[:, :LANES] * (1.0 / o2[:, LANES:])
        o_ref[0, pl.ds(qs, gq), :] = jnp.where(first, o2[:gq], o2[gq:]).astype(o_ref.dtype)

    def body(i, carry):
        _interleave([group(i * NA_UNROLL + u) for u in range(NA_UNROLL)])
        return carry

    lax.fori_loop(0, n_groups // NA_UNROLL, body, 0)


def _na_attn(q, k, v, bias, cm):
    b, t, _ = q.shape
    rows = t // GRID_W
    seq = lambda: pl.BlockSpec((1, t, LANES), lambda j, i: (i, 0, j))
    return pl.pallas_call(
        functools.partial(_na_attn_kernel, rows),
        grid=(D_MODEL // LANES, b),
        in_specs=[seq(), seq(), seq(),
                  pl.BlockSpec((2,) + bias.shape[1:], lambda j, i: (j, 0, 0, 0)),
                  _resident(cm.shape)],
        out_specs=seq(),
        out_shape=jax.ShapeDtypeStruct((b, t, D_MODEL), BF16),
        compiler_params=_params("parallel", "parallel"),
        name="na_attn",
    )(q, k, v, bias, cm)


def _na_tables(rpb, rows):
    assert rows % NA_GROUP == 0 and rows >= NA_UNION + NA_GROUP
    cols = np.arange(GRID_W)
    col_start = np.clip(cols - NA_WIN_C // 2, 0, GRID_W - NA_WIN_C)
    col_mask = (cols[None, :] >= col_start[:, None]) & (cols[None, :] < col_start[:, None] + NA_WIN_C)
    col_off = np.clip(cols[None, :] - cols[:, None] + NA_WIN_C - 1, 0, 2 * NA_WIN_C - 2)
    n_groups = rows // NA_GROUP
    by_col = rpb.astype(F32)[:, :, col_off].transpose(0, 2, 1, 3)
    pad = NA_UNION
    by_col = jnp.pad(by_col, ((0, 0), (0, 0), (pad, pad), (0, 0)))
    valid, kinds = [], []
    for g in (0, 1, n_groups - 1):
        start = _na_group_start(g, rows)
        key = start + np.arange(NA_UNION)[None, :]
        r = g * NA_GROUP + np.arange(NA_GROUP)[:, None]
        r0 = np.clip(r - NA_WIN_R // 2, 0, rows - NA_WIN_R)
        valid.append((key >= r0) & (key < r0 + NA_WIN_R))
        per_row = []
        for a in range(NA_GROUP):
            off = start - (g * NA_GROUP + a) + NA_WIN_R - 1 + pad
            per_row.append(by_col[:, :, off:off + NA_UNION, :])
        kinds.append(jnp.stack(per_row, axis=1))
    valid = np.stack(valid)
    assert (valid.sum(-1) == NA_WIN_R).all()
    full_valid = valid[:, :, None, :, None] & col_mask[None, None, :, None, :]
    shape = (3, NA_GROUP * GRID_W, NA_UNION * GRID_W)
    tab = jnp.stack(kinds, axis=1) * LOG2E
    tab = jnp.where(full_valid[None], tab, NEG_INF * LOG2E).reshape((NA_HEADS,) + shape)
    return tab, jnp.asarray(full_valid.reshape(shape), F32)


def _na_mixer(x, mix_g, w_in, q_norm, k_norm, rpb, w_out):
    b, t, _ = x.shape
    m = b * t
    tile2 = lambda g: jnp.tile(g.astype(F32), 2 * LANES // NA_HEAD_DIM).reshape(1, 2 * LANES)
    q, k, v = _na_proj(x.reshape(m, D_MODEL), mix_g, w_in, tile2(q_norm), tile2(k_norm))
    bias, cm = _na_tables(rpb, t // GRID_W)
    r3 = lambda a: a.reshape(b, t, D_MODEL)
    o = _na_attn(r3(q), r3(k), r3(v), bias, cm)
    return "proj", (o.reshape(m, D_MODEL), w_out)


_MLA_PERM = np.concatenate([
    MLA_NOPE + np.arange(16),
    np.arange(48),
    MLA_NOPE + 16 + np.arange(16),
    48 + np.arange(16),
    -np.ones(32, np.int64),
]).astype(np.int64)


def _mla_prep_kernel(x_ref, g_ref, win_ref, qa_ref, kva_ref, wuq_ref, wuk_ref, wuv_ref,
                     swap_ref, bd_ref, aq_ref, bq_ref, ak_ref, bk_ref, q_ref, k_ref, v_ref):
    h = _rms(x_ref[...], g_ref[...]).astype(BF16)
    c = _dot(h, win_ref[...])
    cq = _rms(c[:, :MLA_Q_LORA], qa_ref[...]).astype(BF16)
    ckv = _rms(c[:, MLA_Q_LORA:MLA_Q_LORA + MLA_KV_LORA], kva_ref[...]).astype(BF16)
    k_rope = c[:, MLA_Q_LORA + MLA_KV_LORA:]
    two = lambda a: jnp.concatenate([a, a], axis=-1)
    aq, bq, ak = two(aq_ref[...]), two(bq_ref[...]), two(ak_ref[...])
    k_rope2 = two(k_rope)
    k_partner = two(pltpu.roll(k_rope, LANES // 2, 1) * bk_ref[...])
    bd = bd_ref[...]

    def inv_rms(y):
        ss = _dot((y * y).astype(BF16), bd)
        return lax.rsqrt(ss * (1.0 / MLA_QK) + NORM_EPS)

    for p in range(MLA_HEADS // 2):
        sl = slice(p * 2 * LANES, (p + 1) * 2 * LANES)
        yq = _dot(cq, wuq_ref[:, sl])
        yq_partner = _dot(yq.astype(BF16), swap_ref[...])
        q_ref[:, sl] = ((yq * aq + yq_partner * bq) * inv_rms(yq)).astype(BF16)
        yk = _dot(ckv, wuk_ref[:, sl]) + k_rope2
        k_ref[:, sl] = ((yk * ak + k_partner) * inv_rms(yk)).astype(BF16)
    v_ref[...] = _dot(ckv, wuv_ref[...]).astype(BF16)


def _mla_prep(x, g, win, qa, kva, wuq, wuk, wuv, swap, bd, aq, bq, ak, bk, t):
    m = x.shape[0]
    tm = min(TOKEN_TILE, t)
    per_seq = t // tm
    wide = MLA_HEADS * LANES
    table = lambda: pl.BlockSpec((tm, LANES), lambda i: (i % per_seq, 0))
    return pl.pallas_call(
        _mla_prep_kernel,
        grid=(m // tm,),
        in_specs=[pl.BlockSpec((tm, D_MODEL), lambda i: (i, 0)),
                  _resident((1, D_MODEL)), _resident(win.shape),
                  _resident((1, MLA_Q_LORA)), _resident((1, MLA_KV_LORA)),
                  _resident(wuq.shape), _resident(wuk.shape), _resident(wuv.shape),
                  _resident(swap.shape), _resident(bd.shape),
                  table(), table(), table(), table()],
        out_specs=[pl.BlockSpec((tm, wide), lambda i: (i, 0)),
                   pl.BlockSpec((tm, wide), lambda i: (i, 0)),
                   pl.BlockSpec((tm, D_MODEL), lambda i: (i, 0))],
        out_shape=[jax.ShapeDtypeStruct((m, wide), BF16),
                   jax.ShapeDtypeStruct((m, wide), BF16),
                   jax.ShapeDtypeStruct((m, D_MODEL), BF16)],
        compiler_params=_params("parallel"),
        name="mla_prep",
    )(x, g, win, qa, kva, wuq, wuk, wuv, swap, bd, aq, bq, ak, bk)


def _mla_attn_kernel(q_ref, k_ref, v_ref, o_ref):
    sub = MLA_Q_SUBTILE
    first = lax.broadcasted_iota(jnp.int32, (sub, LANES), 1) < MLA_V
    v_ext = jnp.concatenate([v_ref[0], jnp.ones((v_ref.shape[1], LANES), BF16)], axis=-1)

    def head(r0, hh):
        sl = slice(hh * LANES, (hh + 1) * LANES)
        s = _dot_nt(q_ref[0, r0:r0 + sub, sl], k_ref[0, :, sl])
        yield
        p = jnp.exp2((s - jnp.max(s, axis=-1, keepdims=True)).astype(BF16))
        yield
        o = _dot(p, v_ext)
        return o[:, :LANES] * (1.0 / o[:, LANES:])

    starts = range(0, q_ref.shape[1], sub)
    outs = _interleave([head(r0, hh) for r0 in starts for hh in range(2)])
    for n, r0 in enumerate(starts):
        o_ref[0, r0:r0 + sub, :] = jnp.where(first, outs[2 * n], outs[2 * n + 1]).astype(o_ref.dtype)


def _mla_attn(q, k, v):
    b, t, _ = q.shape
    tq = min(MLA_Q_TILE, t)
    return pl.pallas_call(
        _mla_attn_kernel,
        grid=(b, MLA_HEADS // 2, t // tq),
        in_specs=[pl.BlockSpec((1, tq, 2 * LANES), lambda i, j, l: (i, l, j)),
                  pl.BlockSpec((1, t, 2 * LANES), lambda i, j, l: (i, 0, j)),
                  pl.BlockSpec((1, t, LANES), lambda i, j, l: (i, 0, j))],
        out_specs=pl.BlockSpec((1, tq, LANES), lambda i, j, l: (i, l, j)),
        out_shape=jax.ShapeDtypeStruct((b, t, D_MODEL), BF16),
        compiler_params=_params("parallel", "parallel", "arbitrary"),
        name="mla_attn",
    )(q, k, v)


def _mla_weights(w_in, w_uq, w_ukv, q_norm, k_norm, t):
    valid = _MLA_PERM >= 0
    src = np.where(valid, _MLA_PERM, 0)
    is_rope = valid & (_MLA_PERM >= MLA_NOPE)
    is_nope = valid & (_MLA_PERM < MLA_NOPE)

    def place(w, per_head, lanes_ok, base=0):
        wh = w.reshape(w.shape[0], MLA_HEADS, per_head)[:, :, base + np.where(lanes_ok, src, 0)]
        return jnp.where(lanes_ok[None, None, :], wh, 0.0).reshape(w.shape[0], MLA_HEADS * LANES)

    wuq = place(w_uq, MLA_QK, valid)
    wuk = place(w_ukv, MLA_NOPE + MLA_V, is_nope)
    wuv = w_ukv.reshape(MLA_KV_LORA, MLA_HEADS, MLA_NOPE + MLA_V)[:, :, MLA_NOPE:].reshape(
        MLA_KV_LORA, MLA_HEADS * MLA_V)
    rope_cols = w_in[:, MLA_Q_LORA + MLA_KV_LORA:]
    rope_placed = jnp.where(is_rope[None, :], rope_cols[:, np.where(is_rope, src - MLA_NOPE, 0)], 0.0)
    win = jnp.concatenate([w_in[:, :MLA_Q_LORA + MLA_KV_LORA], rope_placed], axis=1)
    gain = lambda g: jnp.where(valid, g.astype(F32)[src], 0.0).reshape(1, LANES)
    partner = (np.arange(LANES) + LANES // 2) % LANES
    gain_partner = lambda g: jnp.where(is_rope, gain(g)[0, partner], 0.0).reshape(1, LANES)

    half = MLA_ROPE // 2
    inv_freq = ROPE_THETA ** (-jnp.arange(half, dtype=F32) / half)
    ang = jnp.arange(t).astype(F32)[:, None] * inv_freq[None, :]
    cos = jnp.ones((t, LANES), F32)
    cos = cos.at[:, 0:half].set(jnp.cos(ang)).at[:, 64:64 + half].set(jnp.cos(ang))
    sin = jnp.zeros((t, LANES), F32)
    sin = sin.at[:, 0:half].set(-jnp.sin(ang)).at[:, 64:64 + half].set(jnp.sin(ang))
    q_mult = MLA_QK ** -0.5 * LOG2E
    tables = (gain(q_norm) * cos * q_mult, gain_partner(q_norm) * sin * q_mult,
              gain(k_norm) * cos, gain_partner(k_norm) * sin)

    lane2 = np.arange(2 * LANES)
    rope2 = np.tile(is_rope, 2)
    swap = (lane2[:, None] == (lane2[None, :] // LANES) * LANES + np.tile(partner, 2)[None, :]) & rope2[None, :]
    bd = (lane2[:, None] // LANES) == (lane2[None, :] // LANES)
    return (win.astype(BF16), wuq.astype(BF16), wuk.astype(BF16), wuv.astype(BF16),
            jnp.asarray(swap, BF16), jnp.asarray(bd, BF16)) + tables


def _mla_mixer(x, mix_g, w_in, q_a_norm, w_uq, kv_a_norm, w_ukv, q_norm, k_norm, w_out):
    b, t, _ = x.shape
    m = b * t
    prepared = _mla_weights(w_in, w_uq, w_ukv, q_norm, k_norm, t)
    q, k, v = _mla_prep(x.reshape(m, D_MODEL), mix_g, prepared[0], q_a_norm.astype(F32).reshape(1, -1),
                        kv_a_norm.astype(F32).reshape(1, -1), *prepared[1:], t)
    o = _mla_attn(q.reshape(b, t, -1), k.reshape(b, t, -1), v.reshape(b, t, -1))
    return "proj", (o.reshape(m, D_MODEL), w_out)


def kernel(x, ffn1_norm, ffn1_w_gu, ffn1_w_down, mix_norm, ffn2_norm, ffn2_w_gu, ffn2_w_down,
           hg_lb_logits, hg_w_in, hg_g_norm, hg_w_out,
           na_w_in, na_q_norm, na_k_norm, na_rpb, na_w_out,
           mla_w_in, mla_q_a_norm, mla_w_uq, mla_kv_a_norm, mla_w_ukv, mla_q_norm, mla_k_norm, mla_w_out):
    b, t, d = x.shape
    m = b * t
    gam = jnp.cumsum(jax.nn.softmax(hg_lb_logits.astype(F32), axis=0), axis=0)
    lb_all = gam - gam[0:1]
    row = lambda g: g.astype(F32).reshape(1, -1)
    bf = lambda w: w.astype(BF16)
    ia = ib = ic = 0
    for layer in range(DEPTH):
        x = _ffn(x.reshape(m, d), row(ffn1_norm[layer]), bf(ffn1_w_gu[layer]),
                 bf(ffn1_w_down[layer])).reshape(b, t, d)
        g = row(mix_norm[layer])
        kind = layer % N_MIXERS
        if kind == 0:
            mixer = _hgrn_mixer(x, g, bf(hg_w_in[ia]), hg_g_norm[ia].astype(F32), bf(hg_w_out[ia]), lb_all[layer])
            ia += 1
        elif kind == 1:
            mixer = _na_mixer(x, g, bf(na_w_in[ib]), na_q_norm[ib], na_k_norm[ib], na_rpb[ib], bf(na_w_out[ib]))
            ib += 1
        else:
            mixer = _mla_mixer(x, g, mla_w_in[ic], mla_q_a_norm[ic], mla_w_uq[ic], mla_kv_a_norm[ic],
                               mla_w_ukv[ic], mla_q_norm[ic], mla_k_norm[ic], bf(mla_w_out[ic]))
            ic += 1
        x = _ffn(x.reshape(m, d), row(ffn2_norm[layer]), bf(ffn2_w_gu[layer]),
                 bf(ffn2_w_down[layer]), *mixer).reshape(b, t, d)
    return x
```

```python
import functools

import numpy as np
import jax
import jax.numpy as jnp
from jax import lax
from jax.experimental import pallas as pl
from jax.experimental.pallas import tpu as pltpu

D_MODEL = 1024
DEPTH = 4
N_MIXERS = 3
GRID_W = 64
D_FF = 2816
NORM_EPS = 1e-6
HG_HEADS = 8
HG_DK = 128
HG_DV = 128
NA_HEADS = 16
NA_HEAD_DIM = 64
NA_WIN_R = 8
NA_WIN_C = 16
MLA_HEADS = 16
MLA_Q_LORA = 768
MLA_KV_LORA = 256
MLA_NOPE = 64
MLA_ROPE = 32
MLA_V = 64
MLA_QK = MLA_NOPE + MLA_ROPE
ROPE_THETA = 10000.0
NEG_INF = -1e30
LOG2E = 1.4426950408889634

LANES = 128
VMEM_LIMIT = 56 * 1024 * 1024
TOKEN_TILE = 512
HG_CHUNK = 128
HG_LEVELS = (64, 32, 16, 8, 4, 2, 1)
HG_SUBLANES = 8
HG_HEADS_PER_STEP = 2
HG_UNROLL = 4
MLA_Q_TILE = 512
MLA_Q_SUBTILE = 512
MLA_PREP_UNROLL = 2
HG_PROJ_ROW_SPLITS = 4
NA_GROUP = 4
NA_UNION = NA_WIN_R + NA_GROUP
NA_UNROLL = 4
F32 = jnp.float32
BF16 = jnp.bfloat16


def _params(*sem):
    return pltpu.CompilerParams(dimension_semantics=sem, vmem_limit_bytes=VMEM_LIMIT)


def _resident(shape):
    nd = len(shape)
    return pl.BlockSpec(shape, lambda *_: (0,) * nd, pipeline_mode=pl.Buffered(1))


def _rms(x, g):
    ms = jnp.mean(x * x, axis=-1, keepdims=True)
    return x * lax.rsqrt(ms + NORM_EPS) * g


def _silu(x):
    return x * (1.0 / (1.0 + jnp.exp(-x)))


def _dot(a, b):
    return jnp.dot(a, b, preferred_element_type=F32)


def _dot_nt(a, b):
    return lax.dot_general(a, b, (((1,), (1,)), ((), ())), preferred_element_type=F32)


def _interleave(gens):
    results = [None] * len(gens)
    pending = set(range(len(gens)))
    while pending:
        for i in sorted(pending):
            try:
                next(gens[i])
            except StopIteration as stop:
                results[i] = stop.value
                pending.discard(i)
    return results


def _dot_tn(a, b):
    return lax.dot_general(a, b, (((0,), (0,)), ((), ())), preferred_element_type=F32)


FF_CHUNKS = ((0, 1536), (1536, 2816))


def _ffn_body(x, g, wgu_ref, wd_ref):
    h = _rms(x, g).astype(BF16)
    acc = None
    for s, e in FF_CHUNKS:
        gate = _dot(h, wgu_ref[:, s:e])
        up = _dot(h, wgu_ref[:, D_FF + s:D_FF + e])
        a = (_silu(gate) * up).astype(BF16)
        d = _dot(a, wd_ref[s:e, :])
        acc = d if acc is None else acc + d
    return x + 0.5 * acc


def _hgrn_gated(o_ref, gate_ref, gn):
    parts = []
    for h in range(HG_HEADS):
        sl = slice(h * HG_DV, (h + 1) * HG_DV)
        parts.append(_rms(o_ref[:, sl], gn) * _silu(gate_ref[:, sl]))
    return jnp.concatenate(parts, axis=-1).astype(BF16)


def _ffn_kernel(mixer, *refs):
    *mixer_refs, x_ref, g_ref, wgu_ref, wd_ref, o_ref = refs
    x = x_ref[...]
    if mixer == "proj":
        a_ref, w_ref = mixer_refs
        x = x + _dot(a_ref[...], w_ref[...])
    elif mixer == "hgrn":
        s_ref, gate_ref, gn_ref, w_ref = mixer_refs
        x = x + _dot(_hgrn_gated(s_ref, gate_ref, gn_ref[...]), w_ref[...])
    o_ref[...] = _ffn_body(x, g_ref[...], wgu_ref, wd_ref)


def _ffn(x, g, wgu, wd, mixer=None, mixer_args=()):
    m = x.shape[0]
    tm = min(TOKEN_TILE, m)
    tile = lambda a: pl.BlockSpec((tm, a.shape[1]), lambda i: (i, 0))
    n_tiled = {None: 0, "proj": 1, "hgrn": 2}[mixer]
    mixer_specs = [tile(a) if i < n_tiled else _resident(a.shape) for i, a in enumerate(mixer_args)]
    return pl.pallas_call(
        functools.partial(_ffn_kernel, mixer),
        grid=(m // tm,),
        in_specs=mixer_specs + [tile(x), _resident((1, D_MODEL)), _resident(wgu.shape), _resident(wd.shape)],
        out_specs=tile(x),
        out_shape=jax.ShapeDtypeStruct((m, D_MODEL), F32),
        compiler_params=_params("parallel"),
        name="ffn" if mixer is None else "ffn_" + mixer,
    )(*mixer_args, x, g, wgu, wd)


def _hgrn_q_row(direction, hb, t):
    return ((t % (2 * hb)) >= hb) != (direction == 1)


def _hgrn_ref_row(direction, hb, t):
    return (t // (2 * hb)) * (2 * hb) + (hb - 1 if direction == 0 else hb)


def _hgrn_tables():
    c = HG_CHUNK
    t = np.arange(c)[:, None]
    s = np.arange(c)[None, :]
    tri = np.stack([s <= t, s >= t]).astype(np.float32)
    wide, full = [], []
    for d in range(2):
        wide_d, full_d = [], []
        for hb in HG_LEVELS:
            same = (t // (2 * hb)) == (s // (2 * hb))
            own = (_hgrn_q_row(d, hb, t) & same & ~_hgrn_q_row(d, hb, s)).astype(np.float32)
            if hb >= HG_SUBLANES:
                wide_d.append(own[_hgrn_q_row(d, hb, np.arange(c))])
            else:
                full_d.append(own)
        wide.append(np.stack(wide_d))
        full.append(np.stack(full_d))
    return tri, np.stack(wide), np.stack(full)


def _hgrn_proj_kernel(x_ref, g_ref, w_ref, llb_ref, l1m_ref, q_ref, lf_ref, lb_ref, v_ref, gate_ref):
    h = _rms(x_ref[...], g_ref[...]).astype(BF16)
    llb = llb_ref[...]
    l1m = l1m_ref[...]

    def log2_decay(z):
        ls = jnp.minimum(z, 0.0) - jnp.log(1.0 + jnp.exp2(jnp.abs(z) * -LOG2E))
        y = l1m + ls
        return (jnp.maximum(llb, y) + jnp.log(1.0 + jnp.exp2(jnp.abs(llb - y) * -LOG2E))) * LOG2E

    finishes = ((q_ref, lambda y: (y * HG_DK ** -0.5).astype(BF16)), (lf_ref, log2_decay), (lb_ref, log2_decay),
                (v_ref, lambda y: y.astype(BF16)), (gate_ref, lambda y: y))
    rows = h.shape[0] // HG_PROJ_ROW_SPLITS

    def part(r0):
        hs = h[r0:r0 + rows]
        for j, (o_ref, finish) in enumerate(finishes):
            y = _dot(hs, w_ref[:, j * D_MODEL:(j + 1) * D_MODEL])
            yield
            o_ref[r0:r0 + rows, :] = finish(y)

    _interleave([part(r0) for r0 in range(0, h.shape[0], rows)])


def _hgrn_proj(x, g, w, llb, l1m):
    m = x.shape[0]
    tm = min(TOKEN_TILE, m)
    tile = lambda: pl.BlockSpec((tm, D_MODEL), lambda i: (i, 0))
    return pl.pallas_call(
        _hgrn_proj_kernel,
        grid=(m // tm,),
        in_specs=[tile(), _resident((1, D_MODEL)), _resident(w.shape),
                  _resident((1, D_MODEL)), _resident((1, D_MODEL))],
        out_specs=[tile()] * 5,
        out_shape=[jax.ShapeDtypeStruct((m, D_MODEL), dt) for dt in (BF16, F32, F32, BF16, F32)],
        compiler_params=_params("parallel"),
        name="hgrn_proj",
    )(x, g, w, llb, l1m)


def _hgrn_chunk(direction, q32, v, logf2, tri_ref, wide_ref, full_ref, cum_ref, st_ref):
    c = HG_CHUNK
    nv = c // HG_SUBLANES
    rows = lambda a, i: a[i * HG_SUBLANES:(i + 1) * HG_SUBLANES]
    f = jnp.exp2(logf2)
    k = 1.0 - f
    hi = logf2.astype(BF16)
    lo = (logf2 - hi.astype(F32)).astype(BF16)
    tri = tri_ref[direction]
    cum = _dot(tri, hi) + _dot(tri, lo)
    heads = HG_HEADS_PER_STEP
    hsl = [slice(h * HG_DK, (h + 1) * HG_DK) for h in range(heads)]
    for h in range(heads):
        cum_ref[direction, h] = cum[:, hsl[h]]
    yield
    ref_row = lambda r: jnp.concatenate([cum_ref[direction, h, r:r + 1, :] for h in range(heads)], axis=-1)
    ref_rows = lambda r: jnp.broadcast_to(ref_row(r), (HG_SUBLANES, q32.shape[1]))
    att = [[None] * nv for _ in range(heads)]

    def add_rows(h, i, piece):
        att[h][i] = piece if att[h][i] is None else att[h][i] + piece

    sub = lax.broadcasted_iota(jnp.int32, (c, q32.shape[1]), 0)
    sub8 = lax.broadcasted_iota(jnp.int32, (HG_SUBLANES, q32.shape[1]), 0)
    n_wide = 0
    for li, hb in enumerate(HG_LEVELS):
        if hb >= HG_SUBLANES:
            parts, q_idx = [], []
            for i in range(nv):
                t0 = i * HG_SUBLANES
                ref = ref_rows(int(_hgrn_ref_row(direction, hb, t0)))
                if bool(_hgrn_q_row(direction, hb, t0)):
                    parts.append(rows(q32, i) * jnp.exp2(rows(cum, i) - ref))
                    q_idx.append(i)
                else:
                    parts.append(rows(k, i) * jnp.exp2(ref - rows(cum, i)))
            w = jnp.concatenate(parts, axis=0).astype(BF16)
            wq = jnp.concatenate([parts[i] for i in q_idx], axis=0).astype(BF16)
            for h in range(heads):
                a = _dot_nt(wq[:, hsl[h]], w[:, hsl[h]]) * wide_ref[direction, n_wide]
                for j, i in enumerate(q_idx):
                    add_rows(h, i, rows(a, j))
            n_wide += 1
        else:
            q_rows = _hgrn_q_row(direction, hb, sub)
            if hb == 1:
                w = jnp.where(q_rows, q32 * f, k)
            else:
                pieces = []
                for i in range(nv):
                    t0 = i * HG_SUBLANES
                    if 2 * hb == HG_SUBLANES:
                        pieces.append(ref_rows(int(_hgrn_ref_row(direction, hb, t0))))
                    else:
                        lo_ref = ref_rows(int(_hgrn_ref_row(direction, hb, t0)))
                        hi_ref = ref_rows(int(_hgrn_ref_row(direction, hb, t0 + 2 * hb)))
                        pieces.append(jnp.where(sub8 < 2 * hb, lo_ref, hi_ref))
                d = cum - jnp.concatenate(pieces, axis=0)
                w = jnp.where(q_rows, q32, k) * jnp.exp2(jnp.where(q_rows, d, -d))
            w = w.astype(BF16)
            for h in range(heads):
                a = _dot_nt(w[:, hsl[h]], w[:, hsl[h]]) * full_ref[direction, li - n_wide]
                for i in range(nv):
                    add_rows(h, i, rows(a, i))
        yield

    last = c - 1 if direction == 0 else 0
    total = ref_row(last)
    qi = (q32 * jnp.exp2(cum)).astype(BF16)
    ki = (k * jnp.exp2(total - cum)).astype(BF16)
    diag_in = (q32 * k).astype(BF16)
    ones = jnp.ones((HG_DK, HG_DV), BF16)
    decay = jnp.exp2(total)
    outs = []
    for h in range(heads):
        vh = v[:, hsl[h]]
        st = st_ref[direction, h]
        o = _dot(jnp.concatenate(att[h], axis=0).astype(BF16), vh)
        o = o + _dot(diag_in[:, hsl[h]], ones) * vh.astype(F32)
        o = o + _dot_nt(qi[:, hsl[h]], st.astype(BF16))
        st_ref[direction, h] = st * decay[:, hsl[h]] + _dot_tn(vh, ki[:, hsl[h]])
        outs.append(o)
    return jnp.concatenate(outs, axis=-1)


def _hgrn_kernel(q_ref, lf_ref, lb_ref, v_ref, tri_ref, wide_ref, full_ref,
                 o_ref, ob_ref, cum_ref, st_ref):
    t = q_ref.shape[1]
    c = HG_CHUNK
    n = t // c
    st_ref[...] = jnp.zeros_like(st_ref)

    def chunk(direction, gate_ref, r0, cum_slot):
        return _hgrn_chunk(direction, q_ref[0, pl.ds(r0, c), :].astype(F32), v_ref[0, pl.ds(r0, c), :],
                           gate_ref[0, pl.ds(r0, c), :], tri_ref, wide_ref, full_ref, cum_slot, st_ref)

    def body(i, carry):
        gens, dests = [], []
        for u in range(HG_UNROLL):
            rf = pl.multiple_of((i * HG_UNROLL + u) * c, c)
            rb = pl.multiple_of((n - 1 - i * HG_UNROLL - u) * c, c)
            gens += [chunk(0, lf_ref, rf, cum_ref.at[u]), chunk(1, lb_ref, rb, cum_ref.at[u])]
            dests += [(o_ref.at[0], rf), (ob_ref, rb)]
        for (dst, r0), o in zip(dests, _interleave(gens)):
            dst[pl.ds(r0, c), :] = o
        return carry

    lax.fori_loop(0, n // HG_UNROLL, body, 0)
    o_ref[0] = o_ref[0] + ob_ref[...]


def _hgrn_scan(q, lf, lb, v):
    b, t, _ = q.shape
    assert t % (HG_CHUNK * HG_UNROLL) == 0
    tri, wide, full = _hgrn_tables()
    tri = jnp.asarray(tri, BF16)
    wide = jnp.asarray(wide, F32)
    full = jnp.asarray(full, F32)
    width = HG_HEADS_PER_STEP * HG_DK
    seq = lambda: pl.BlockSpec((1, t, width), lambda i, j: (i, 0, j))
    return pl.pallas_call(
        _hgrn_kernel,
        grid=(b, HG_HEADS // HG_HEADS_PER_STEP),
        in_specs=[seq(), seq(), seq(), seq(),
                  _resident(tri.shape), _resident(wide.shape), _resident(full.shape)],
        out_specs=seq(),
        out_shape=jax.ShapeDtypeStruct((b, t, D_MODEL), F32),
        scratch_shapes=[pltpu.VMEM((t, width), F32),
                        pltpu.VMEM((HG_UNROLL, 2, HG_HEADS_PER_STEP, HG_CHUNK, HG_DK), F32),
                        pltpu.VMEM((2, HG_HEADS_PER_STEP, HG_DV, HG_DK), F32)],
        compiler_params=_params("parallel", "parallel"),
        name="hgrn_scan",
    )(q, lf, lb, v, tri, wide, full)


def _hgrn_mixer(x, mix_g, w_in, g_norm, w_out, lb):
    b, t, _ = x.shape
    m = b * t
    lb = lb.astype(F32).reshape(1, D_MODEL)
    q, lf, lbw, v, gate = _hgrn_proj(x.reshape(m, D_MODEL), mix_g, w_in, jnp.log(lb), jnp.log1p(-lb))
    r3 = lambda a: a.reshape(b, t, D_MODEL)
    o = _hgrn_scan(r3(q), r3(lf), r3(lbw), r3(v))
    return "hgrn", (o.reshape(m, D_MODEL), gate, g_norm.reshape(1, HG_DV), w_out)


def _na_proj_kernel(x_ref, g_ref, w_ref, qg_ref, kg_ref, bd_ref, q_ref, k_ref, v_ref):
    h = _rms(x_ref[...], g_ref[...]).astype(BF16)
    bd = bd_ref[...]

    def head_norm(y, gain, mult):
        width = bd.shape[0]
        parts = []
        for j in range(D_MODEL // width):
            ys = y[:, j * width:(j + 1) * width]
            ss = _dot((ys * ys).astype(BF16), bd)
            parts.append(ys * lax.rsqrt(ss * (1.0 / NA_HEAD_DIM) + NORM_EPS) * (gain * mult))
        return jnp.concatenate(parts, axis=-1)

    q = _dot(h, w_ref[:, 0:D_MODEL])
    q_ref[...] = head_norm(q, qg_ref[...], NA_HEAD_DIM ** -0.5 * LOG2E).astype(BF16)
    k = _dot(h, w_ref[:, D_MODEL:2 * D_MODEL])
    k_ref[...] = head_norm(k, kg_ref[...], 1.0).astype(BF16)
    v_ref[...] = _dot(h, w_ref[:, 2 * D_MODEL:3 * D_MODEL]).astype(BF16)


def _na_proj(x, g, w, qg, kg):
    m = x.shape[0]
    tm = min(TOKEN_TILE, m)
    blk = np.arange(2 * LANES) // NA_HEAD_DIM
    bd = jnp.asarray(blk[:, None] == blk[None, :], BF16)
    tile = lambda: pl.BlockSpec((tm, D_MODEL), lambda i: (i, 0))
    return pl.pallas_call(
        _na_proj_kernel,
        grid=(m // tm,),
        in_specs=[tile(), _resident((1, D_MODEL)), _resident(w.shape),
                  _resident(qg.shape), _resident(kg.shape), _resident(bd.shape)],
        out_specs=[tile(), tile(), tile()],
        out_shape=[jax.ShapeDtypeStruct((m, D_MODEL), BF16)] * 3,
        compiler_params=_params("parallel"),
        name="na_proj",
    )(x, g, w, qg, kg, bd)


def _na_group_start(g, rows):
    return jnp.clip(g * NA_GROUP - NA_WIN_R // 2, 0, rows - NA_UNION) if isinstance(g, jax.Array) else \
        int(np.clip(g * NA_GROUP - NA_WIN_R // 2, 0, rows - NA_UNION))


def _na_attn_kernel(rows, q_ref, k_ref, v_ref, bias_ref, cm_ref, o_ref):
    gq = NA_GROUP * GRID_W
    uk = NA_UNION * GRID_W
    n_groups = rows // NA_GROUP
    lane = lax.broadcasted_iota(jnp.int32, (gq, LANES), 1)
    first = lane < NA_HEAD_DIM
    head_sel = (jnp.where(first[0:1], 1.0, 0.0).astype(BF16), jnp.where(first[0:1], 0.0, 1.0).astype(BF16))

    def group(g):
        kind = jnp.where(g == 0, 0, jnp.where(g == n_groups - 1, 2, 1))
        qs = pl.multiple_of(g * gq, gq)
        ks = pl.multiple_of(_na_group_start(g, rows) * GRID_W, NA_GROUP * GRID_W)
        q = q_ref[0, pl.ds(qs, gq), :]
        kb = k_ref[0, pl.ds(ks, uk), :]
        vb = v_ref[0, pl.ds(ks, uk), :]
        s = _dot_nt(jnp.concatenate([q * head_sel[0], q * head_sel[1]], axis=0), kb)
        yield
        cm = cm_ref[kind]
        probs = []
        for hh in range(2):
            sh = s[hh * gq:(hh + 1) * gq] * cm + bias_ref[hh, kind]
            probs.append(jnp.exp2((sh - jnp.max(sh, axis=-1, keepdims=True)).astype(BF16)))
            yield
        v_ext = jnp.concatenate([vb, jnp.ones_like(vb)], axis=-1)
        o2 = _dot(jnp.concatenate(probs, axis=0), v_ext)
        o2 = o2[:, :LANES] * (1.0 / o2[:, LANES:])
        o_ref[0, pl.ds(qs, gq), :] = jnp.where(first, o2[:gq], o2[gq:]).astype(o_ref.dtype)

    def body(i, carry):
        _interleave([group(i * NA_UNROLL + u) for u in range(NA_UNROLL)])
        return carry

    lax.fori_loop(0, n_groups // NA_UNROLL, body, 0)


def _na_attn(q, k, v, bias, cm):
    b, t, _ = q.shape
    rows = t // GRID_W
    seq = lambda: pl.BlockSpec((1, t, LANES), lambda j, i: (i, 0, j))
    return pl.pallas_call(
        functools.partial(_na_attn_kernel, rows),
        grid=(D_MODEL // LANES, b),
        in_specs=[seq(), seq(), seq(),
                  pl.BlockSpec((2,) + bias.shape[1:], lambda j, i: (j, 0, 0, 0)),
                  _resident(cm.shape)],
        out_specs=seq(),
        out_shape=jax.ShapeDtypeStruct((b, t, D_MODEL), BF16),
        compiler_params=_params("parallel", "parallel"),
        name="na_attn",
    )(q, k, v, bias, cm)


def _na_tables(rpb, rows):
    assert rows % NA_GROUP == 0 and rows >= NA_UNION + NA_GROUP
    cols = np.arange(GRID_W)
    col_start = np.clip(cols - NA_WIN_C // 2, 0, GRID_W - NA_WIN_C)
    col_mask = (cols[None, :] >= col_start[:, None]) & (cols[None, :] < col_start[:, None] + NA_WIN_C)
    col_off = np.clip(cols[None, :] - cols[:, None] + NA_WIN_C - 1, 0, 2 * NA_WIN_C - 2)
    n_groups = rows // NA_GROUP
    by_col = rpb.astype(F32)[:, :, col_off].transpose(0, 2, 1, 3)
    pad = NA_UNION
    by_col = jnp.pad(by_col, ((0, 0), (0, 0), (pad, pad), (0, 0)))
    valid, kinds = [], []
    for g in (0, 1, n_groups - 1):
        start = _na_group_start(g, rows)
        key = start + np.arange(NA_UNION)[None, :]
        r = g * NA_GROUP + np.arange(NA_GROUP)[:, None]
        r0 = np.clip(r - NA_WIN_R // 2, 0, rows - NA_WIN_R)
        valid.append((key >= r0) & (key < r0 + NA_WIN_R))
        per_row = []
        for a in range(NA_GROUP):
            off = start - (g * NA_GROUP + a) + NA_WIN_R - 1 + pad
            per_row.append(by_col[:, :, off:off + NA_UNION, :])
        kinds.append(jnp.stack(per_row, axis=1))
    valid = np.stack(valid)
    assert (valid.sum(-1) == NA_WIN_R).all()
    full_valid = valid[:, :, None, :, None] & col_mask[None, None, :, None, :]
    shape = (3, NA_GROUP * GRID_W, NA_UNION * GRID_W)
    tab = jnp.stack(kinds, axis=1) * LOG2E
    tab = jnp.where(full_valid[None], tab, NEG_INF * LOG2E).reshape((NA_HEADS,) + shape)
    return tab, jnp.asarray(full_valid.reshape(shape), F32)


def _na_mixer(x, mix_g, w_in, q_norm, k_norm, rpb, w_out):
    b, t, _ = x.shape
    m = b * t
    tile2 = lambda g: jnp.tile(g.astype(F32), 2 * LANES // NA_HEAD_DIM).reshape(1, 2 * LANES)
    q, k, v = _na_proj(x.reshape(m, D_MODEL), mix_g, w_in, tile2(q_norm), tile2(k_norm))
    bias, cm = _na_tables(rpb, t // GRID_W)
    r3 = lambda a: a.reshape(b, t, D_MODEL)
    o = _na_attn(r3(q), r3(k), r3(v), bias, cm)
    return "proj", (o.reshape(m, D_MODEL), w_out)


_MLA_PERM = np.concatenate([
    MLA_NOPE + np.arange(16),
    np.arange(48),
    MLA_NOPE + 16 + np.arange(16),
    48 + np.arange(16),
    -np.ones(32, np.int64),
]).astype(np.int64)


def _mla_prep_kernel(x_ref, g_ref, win_ref, qa_ref, kva_ref, wuq_ref, wuk_ref, wuv_ref,
                     swap_ref, bd_ref, aq_ref, bq_ref, ak_ref, bk_ref, q_ref, k_ref, v_ref):
    h = _rms(x_ref[...], g_ref[...]).astype(BF16)
    c = _dot(h, win_ref[...])
    cq = _rms(c[:, :MLA_Q_LORA], qa_ref[...]).astype(BF16)
    ckv = _rms(c[:, MLA_Q_LORA:MLA_Q_LORA + MLA_KV_LORA], kva_ref[...]).astype(BF16)
    k_rope = c[:, MLA_Q_LORA + MLA_KV_LORA:]
    two = lambda a: jnp.concatenate([a, a], axis=-1)
    aq, bq, ak = two(aq_ref[...]), two(bq_ref[...]), two(ak_ref[...])
    k_rope2 = two(k_rope)
    k_partner = two(pltpu.roll(k_rope, LANES // 2, 1) * bk_ref[...])
    bd = bd_ref[...]

    def inv_rms(y):
        ss = _dot((y * y).astype(BF16), bd)
        return lax.rsqrt(ss * (1.0 / MLA_QK) + NORM_EPS)

    def pair(p):
        sl = slice(p * 2 * LANES, (p + 1) * 2 * LANES)
        yq = _dot(cq, wuq_ref[:, sl])
        yk = _dot(ckv, wuk_ref[:, sl]) + k_rope2
        yield
        yq_partner = _dot(yq.astype(BF16), swap_ref[...])
        rq = inv_rms(yq)
        rk = inv_rms(yk)
        yield
        q_ref[:, sl] = ((yq * aq + yq_partner * bq) * rq).astype(BF16)
        k_ref[:, sl] = ((yk * ak + k_partner) * rk).astype(BF16)

    for p0 in range(0, MLA_HEADS // 2, MLA_PREP_UNROLL):
        _interleave([pair(p) for p in range(p0, p0 + MLA_PREP_UNROLL)])
    v_ref[...] = _dot(ckv, wuv_ref[...]).astype(BF16)


def _mla_prep(x, g, win, qa, kva, wuq, wuk, wuv, swap, bd, aq, bq, ak, bk, t):
    m = x.shape[0]
    tm = min(TOKEN_TILE, t)
    per_seq = t // tm
    wide = MLA_HEADS * LANES
    table = lambda: pl.BlockSpec((tm, LANES), lambda i: (i % per_seq, 0))
    return pl.pallas_call(
        _mla_prep_kernel,
        grid=(m // tm,),
        in_specs=[pl.BlockSpec((tm, D_MODEL), lambda i: (i, 0)),
                  _resident((1, D_MODEL)), _resident(win.shape),
                  _resident((1, MLA_Q_LORA)), _resident((1, MLA_KV_LORA)),
                  _resident(wuq.shape), _resident(wuk.shape), _resident(wuv.shape),
                  _resident(swap.shape), _resident(bd.shape),
                  table(), table(), table(), table()],
        out_specs=[pl.BlockSpec((tm, wide), lambda i: (i, 0)),
                   pl.BlockSpec((tm, wide), lambda i: (i, 0)),
                   pl.BlockSpec((tm, D_MODEL), lambda i: (i, 0))],
        out_shape=[jax.ShapeDtypeStruct((m, wide), BF16),
                   jax.ShapeDtypeStruct((m, wide), BF16),
                   jax.ShapeDtypeStruct((m, D_MODEL), BF16)],
        compiler_params=_params("parallel"),
        name="mla_prep",
    )(x, g, win, qa, kva, wuq, wuk, wuv, swap, bd, aq, bq, ak, bk)


def _mla_attn_kernel(q_ref, k_ref, v_ref, o_ref):
    sub = MLA_Q_SUBTILE
    first = lax.broadcasted_iota(jnp.int32, (sub, LANES), 1) < MLA_V
    v_ext = jnp.concatenate([v_ref[0], jnp.ones((v_ref.shape[1], LANES), BF16)], axis=-1)

    def head(r0, hh):
        sl = slice(hh * LANES, (hh + 1) * LANES)
        s = _dot_nt(q_ref[0, r0:r0 + sub, sl], k_ref[0, :, sl])
        yield
        p = jnp.exp2((s - jnp.max(s, axis=-1, keepdims=True)).astype(BF16))
        yield
        o = _dot(p, v_ext)
        return o[:, :LANES] * (1.0 / o[:, LANES:])

    starts = range(0, q_ref.shape[1], sub)
    outs = _interleave([head(r0, hh) for r0 in starts for hh in range(2)])
    for n, r0 in enumerate(starts):
        o_ref[0, r0:r0 + sub, :] = jnp.where(first, outs[2 * n], outs[2 * n + 1]).astype(o_ref.dtype)


def _mla_attn(q, k, v):
    b, t, _ = q.shape
    tq = min(MLA_Q_TILE, t)
    return pl.pallas_call(
        _mla_attn_kernel,
        grid=(b, MLA_HEADS // 2, t // tq),
        in_specs=[pl.BlockSpec((1, tq, 2 * LANES), lambda i, j, l: (i, l, j)),
                  pl.BlockSpec((1, t, 2 * LANES), lambda i, j, l: (i, 0, j)),
                  pl.BlockSpec((1, t, LANES), lambda i, j, l: (i, 0, j))],
        out_specs=pl.BlockSpec((1, tq, LANES), lambda i, j, l: (i, l, j)),
        out_shape=jax.ShapeDtypeStruct((b, t, D_MODEL), BF16),
        compiler_params=_params("parallel", "parallel", "arbitrary"),
        name="mla_attn",
    )(q, k, v)


def _mla_weights(w_in, w_uq, w_ukv, q_norm, k_norm, t):
    valid = _MLA_PERM >= 0
    src = np.where(valid, _MLA_PERM, 0)
    is_rope = valid & (_MLA_PERM >= MLA_NOPE)
    is_nope = valid & (_MLA_PERM < MLA_NOPE)

    def place(w, per_head, lanes_ok, base=0):
        wh = w.reshape(w.shape[0], MLA_HEADS, per_head)[:, :, base + np.where(lanes_ok, src, 0)]
        return jnp.where(lanes_ok[None, None, :], wh, 0.0).reshape(w.shape[0], MLA_HEADS * LANES)

    wuq = place(w_uq, MLA_QK, valid)
    wuk = place(w_ukv, MLA_NOPE + MLA_V, is_nope)
    wuv = w_ukv.reshape(MLA_KV_LORA, MLA_HEADS, MLA_NOPE + MLA_V)[:, :, MLA_NOPE:].reshape(
        MLA_KV_LORA, MLA_HEADS * MLA_V)
    rope_cols = w_in[:, MLA_Q_LORA + MLA_KV_LORA:]
    rope_placed = jnp.where(is_rope[None, :], rope_cols[:, np.where(is_rope, src - MLA_NOPE, 0)], 0.0)
    win = jnp.concatenate([w_in[:, :MLA_Q_LORA + MLA_KV_LORA], rope_placed], axis=1)
    gain = lambda g: jnp.where(valid, g.astype(F32)[src], 0.0).reshape(1, LANES)
    partner = (np.arange(LANES) + LANES // 2) % LANES
    gain_partner = lambda g: jnp.where(is_rope, gain(g)[0, partner], 0.0).reshape(1, LANES)

    half = MLA_ROPE // 2
    inv_freq = ROPE_THETA ** (-jnp.arange(half, dtype=F32) / half)
    ang = jnp.arange(t).astype(F32)[:, None] * inv_freq[None, :]
    cos = jnp.ones((t, LANES), F32)
    cos = cos.at[:, 0:half].set(jnp.cos(ang)).at[:, 64:64 + half].set(jnp.cos(ang))
    sin = jnp.zeros((t, LANES), F32)
    sin = sin.at[:, 0:half].set(-jnp.sin(ang)).at[:, 64:64 + half].set(jnp.sin(ang))
    q_mult = MLA_QK ** -0.5 * LOG2E
    tables = (gain(q_norm) * cos * q_mult, gain_partner(q_norm) * sin * q_mult,
              gain(k_norm) * cos, gain_partner(k_norm) * sin)

    lane2 = np.arange(2 * LANES)
    rope2 = np.tile(is_rope, 2)
    swap = (lane2[:, None] == (lane2[None, :] // LANES) * LANES + np.tile(partner, 2)[None, :]) & rope2[None, :]
    bd = (lane2[:, None] // LANES) == (lane2[None, :] // LANES)
    return (win.astype(BF16), wuq.astype(BF16), wuk.astype(BF16), wuv.astype(BF16),
            jnp.asarray(swap, BF16), jnp.asarray(bd, BF16)) + tables


def _mla_mixer(x, mix_g, w_in, q_a_norm, w_uq, kv_a_norm, w_ukv, q_norm, k_norm, w_out):
    b, t, _ = x.shape
    m = b * t
    prepared = _mla_weights(w_in, w_uq, w_ukv, q_norm, k_norm, t)
    q, k, v = _mla_prep(x.reshape(m, D_MODEL), mix_g, prepared[0], q_a_norm.astype(F32).reshape(1, -1),
                        kv_a_norm.astype(F32).reshape(1, -1), *prepared[1:], t)
    o = _mla_attn(q.reshape(b, t, -1), k.reshape(b, t, -1), v.reshape(b, t, -1))
    return "proj", (o.reshape(m, D_MODEL), w_out)


def kernel(x, ffn1_norm, ffn1_w_gu, ffn1_w_down, mix_norm, ffn2_norm, ffn2_w_gu, ffn2_w_down,
           hg_lb_logits, hg_w_in, hg_g_norm, hg_w_out,
           na_w_in, na_q_norm, na_k_norm, na_rpb, na_w_out,
           mla_w_in, mla_q_a_norm, mla_w_uq, mla_kv_a_norm, mla_w_ukv, mla_q_norm, mla_k_norm, mla_w_out):
    b, t, d = x.shape
    m = b * t
    gam = jnp.cumsum(jax.nn.softmax(hg_lb_logits.astype(F32), axis=0), axis=0)
    lb_all = gam - gam[0:1]
    row = lambda g: g.astype(F32).reshape(1, -1)
    bf = lambda w: w.astype(BF16)
    ia = ib = ic = 0
    for layer in range(DEPTH):
        x = _ffn(x.reshape(m, d), row(ffn1_norm[layer]), bf(ffn1_w_gu[layer]),
                 bf(ffn1_w_down[layer])).reshape(b, t, d)
        g = row(mix_norm[layer])
        kind = layer % N_MIXERS
        if kind == 0:
            mixer = _hgrn_mixer(x, g, bf(hg_w_in[ia]), hg_g_norm[ia].astype(F32), bf(hg_w_out[ia]), lb_all[layer])
            ia += 1
        elif kind == 1:
            mixer = _na_mixer(x, g, bf(na_w_in[ib]), na_q_norm[ib], na_k_norm[ib], na_rpb[ib], bf(na_w_out[ib]))
            ib += 1
        else:
            mixer = _mla_mixer(x, g, mla_w_in[ic], mla_q_a_norm[ic], mla_w_uq[ic], mla_kv_a_norm[ic],
                               mla_w_ukv[ic], mla_q_norm[ic], mla_k_norm[ic], bf(mla_w_out[ic]))
            ic += 1
        x = _ffn(x.reshape(m, d), row(ffn2_norm[layer]), bf(ffn2_w_gu[layer]),
                 bf(ffn2_w_down[layer]), *mixer).reshape(b, t, d)
    return x
```

```python
import functools

import numpy as np
import jax
import jax.numpy as jnp
from jax import lax
from jax.experimental import pallas as pl
from jax.experimental.pallas import tpu as pltpu

D_MODEL = 1024
DEPTH = 4
N_MIXERS = 3
GRID_W = 64
D_FF = 2816
NORM_EPS = 1e-6
HG_HEADS = 8
HG_DK = 128
HG_DV = 128
NA_HEADS = 16
NA_HEAD_DIM = 64
NA_WIN_R = 8
NA_WIN_C = 16
MLA_HEADS = 16
MLA_Q_LORA = 768
MLA_KV_LORA = 256
MLA_NOPE = 64
MLA_ROPE = 32
MLA_V = 64
MLA_QK = MLA_NOPE + MLA_ROPE
ROPE_THETA = 10000.0
NEG_INF = -1e30
LOG2E = 1.4426950408889634

LANES = 128
VMEM_LIMIT = 56 * 1024 * 1024
TOKEN_TILE = 512
HG_CHUNK = 128
HG_LEVELS = (64, 32, 16, 8, 4, 2, 1)
HG_SUBLANES = 8
HG_HEADS_PER_STEP = 2
HG_UNROLL = 8
MLA_Q_TILE = 512
MLA_Q_SUBTILE = 512
MLA_PREP_UNROLL = 2
HG_PROJ_ROW_SPLITS = 4
NA_GROUP = 4
NA_UNION = NA_WIN_R + NA_GROUP
NA_UNROLL = 4
F32 = jnp.float32
BF16 = jnp.bfloat16


def _params(*sem):
    return pltpu.CompilerParams(dimension_semantics=sem, vmem_limit_bytes=VMEM_LIMIT)


def _resident(shape):
    nd = len(shape)
    return pl.BlockSpec(shape, lambda *_: (0,) * nd, pipeline_mode=pl.Buffered(1))


def _rms(x, g):
    ms = jnp.mean(x * x, axis=-1, keepdims=True)
    return x * lax.rsqrt(ms + NORM_EPS) * g


def _silu(x):
    return x * (1.0 / (1.0 + jnp.exp(-x)))


def _dot(a, b):
    return jnp.dot(a, b, preferred_element_type=F32)


def _dot_nt(a, b):
    return lax.dot_general(a, b, (((1,), (1,)), ((), ())), preferred_element_type=F32)


def _interleave(gens):
    results = [None] * len(gens)
    pending = set(range(len(gens)))
    while pending:
        for i in sorted(pending):
            try:
                next(gens[i])
            except StopIteration as stop:
                results[i] = stop.value
                pending.discard(i)
    return results


def _dot_tn(a, b):
    return lax.dot_general(a, b, (((0,), (0,)), ((), ())), preferred_element_type=F32)


FF_CHUNKS = ((0, 1536), (1536, 2816))
FFN_ROW_SPLITS = 2


def _ffn_body(x, g, wgu_ref, wd_ref):
    h = _rms(x, g).astype(BF16)
    acc = None
    for s, e in FF_CHUNKS:
        gate = _dot(h, wgu_ref[:, s:e])
        up = _dot(h, wgu_ref[:, D_FF + s:D_FF + e])
        yield
        a = (_silu(gate) * up).astype(BF16)
        d = _dot(a, wd_ref[s:e, :])
        acc = d if acc is None else acc + d
        yield
    return x + 0.5 * acc


def _hgrn_gated(o_ref, gate_ref, gn, rs):
    parts = []
    for h in range(HG_HEADS):
        sl = slice(h * HG_DV, (h + 1) * HG_DV)
        parts.append(_rms(o_ref[rs, sl], gn) * _silu(gate_ref[rs, sl]))
    return jnp.concatenate(parts, axis=-1).astype(BF16)


def _ffn_kernel(mixer, *refs):
    *mixer_refs, x_ref, g_ref, wgu_ref, wd_ref, o_ref = refs
    rows = x_ref.shape[0] // FFN_ROW_SPLITS

    def part(r0):
        rs = slice(r0, r0 + rows)
        x = x_ref[rs, :]
        if mixer == "proj":
            a_ref, w_ref = mixer_refs
            x = x + _dot(a_ref[rs, :], w_ref[...])
        elif mixer == "hgrn":
            s_ref, gate_ref, gn_ref, w_ref = mixer_refs
            x = x + _dot(_hgrn_gated(s_ref, gate_ref, gn_ref[...], rs), w_ref[...])
        yield
        o_ref[rs, :] = yield from _ffn_body(x, g_ref[...], wgu_ref, wd_ref)

    _interleave([part(r0) for r0 in range(0, x_ref.shape[0], rows)])


def _ffn(x, g, wgu, wd, mixer=None, mixer_args=()):
    m = x.shape[0]
    tm = min(TOKEN_TILE, m)
    tile = lambda a: pl.BlockSpec((tm, a.shape[1]), lambda i: (i, 0))
    n_tiled = {None: 0, "proj": 1, "hgrn": 2}[mixer]
    mixer_specs = [tile(a) if i < n_tiled else _resident(a.shape) for i, a in enumerate(mixer_args)]
    return pl.pallas_call(
        functools.partial(_ffn_kernel, mixer),
        grid=(m // tm,),
        in_specs=mixer_specs + [tile(x), _resident((1, D_MODEL)), _resident(wgu.shape), _resident(wd.shape)],
        out_specs=tile(x),
        out_shape=jax.ShapeDtypeStruct((m, D_MODEL), F32),
        compiler_params=_params("parallel"),
        name="ffn" if mixer is None else "ffn_" + mixer,
    )(*mixer_args, x, g, wgu, wd)


def _hgrn_q_row(direction, hb, t):
    return ((t % (2 * hb)) >= hb) != (direction == 1)


def _hgrn_ref_row(direction, hb, t):
    return (t // (2 * hb)) * (2 * hb) + (hb - 1 if direction == 0 else hb)


def _hgrn_tables():
    c = HG_CHUNK
    t = np.arange(c)[:, None]
    s = np.arange(c)[None, :]
    tri = np.stack([s <= t, s >= t]).astype(np.float32)
    wide, full = [], []
    for d in range(2):
        wide_d, full_d = [], []
        for hb in HG_LEVELS:
            same = (t // (2 * hb)) == (s // (2 * hb))
            own = (_hgrn_q_row(d, hb, t) & same & ~_hgrn_q_row(d, hb, s)).astype(np.float32)
            if hb >= HG_SUBLANES:
                wide_d.append(own[_hgrn_q_row(d, hb, np.arange(c))])
            else:
                full_d.append(own)
        wide.append(np.stack(wide_d))
        full.append(np.stack(full_d))
    return tri, np.stack(wide), np.stack(full)


def _hgrn_proj_kernel(x_ref, g_ref, w_ref, llb_ref, l1m_ref, q_ref, lf_ref, lb_ref, v_ref, gate_ref):
    h = _rms(x_ref[...], g_ref[...]).astype(BF16)
    llb = llb_ref[...]
    l1m = l1m_ref[...]

    def log2_decay(z):
        ls = jnp.minimum(z, 0.0) - jnp.log(1.0 + jnp.exp2(jnp.abs(z) * -LOG2E))
        y = l1m + ls
        return (jnp.maximum(llb, y) + jnp.log(1.0 + jnp.exp2(jnp.abs(llb - y) * -LOG2E))) * LOG2E

    finishes = ((q_ref, lambda y: (y * HG_DK ** -0.5).astype(BF16)), (lf_ref, log2_decay), (lb_ref, log2_decay),
                (v_ref, lambda y: y.astype(BF16)), (gate_ref, lambda y: y))
    rows = h.shape[0] // HG_PROJ_ROW_SPLITS

    def part(r0):
        hs = h[r0:r0 + rows]
        for j, (o_ref, finish) in enumerate(finishes):
            y = _dot(hs, w_ref[:, j * D_MODEL:(j + 1) * D_MODEL])
            yield
            o_ref[r0:r0 + rows, :] = finish(y)

    _interleave([part(r0) for r0 in range(0, h.shape[0], rows)])


def _hgrn_proj(x, g, w, llb, l1m):
    m = x.shape[0]
    tm = min(TOKEN_TILE, m)
    tile = lambda: pl.BlockSpec((tm, D_MODEL), lambda i: (i, 0))
    return pl.pallas_call(
        _hgrn_proj_kernel,
        grid=(m // tm,),
        in_specs=[tile(), _resident((1, D_MODEL)), _resident(w.shape),
                  _resident((1, D_MODEL)), _resident((1, D_MODEL))],
        out_specs=[tile()] * 5,
        out_shape=[jax.ShapeDtypeStruct((m, D_MODEL), dt) for dt in (BF16, F32, F32, BF16, F32)],
        compiler_params=_params("parallel"),
        name="hgrn_proj",
    )(x, g, w, llb, l1m)


def _hgrn_chunk(direction, q32, v, logf2, tri_ref, wide_ref, full_ref, cum_ref, st_ref):
    c = HG_CHUNK
    nv = c // HG_SUBLANES
    rows = lambda a, i: a[i * HG_SUBLANES:(i + 1) * HG_SUBLANES]
    f = jnp.exp2(logf2)
    k = 1.0 - f
    hi = logf2.astype(BF16)
    lo = (logf2 - hi.astype(F32)).astype(BF16)
    tri = tri_ref[direction]
    cum = _dot(tri, hi) + _dot(tri, lo)
    heads = HG_HEADS_PER_STEP
    hsl = [slice(h * HG_DK, (h + 1) * HG_DK) for h in range(heads)]
    for h in range(heads):
        cum_ref[direction, h] = cum[:, hsl[h]]
    yield
    ref_row = lambda r: jnp.concatenate([cum_ref[direction, h, r:r + 1, :] for h in range(heads)], axis=-1)
    ref_rows = lambda r: jnp.broadcast_to(ref_row(r), (HG_SUBLANES, q32.shape[1]))
    att = [[None] * nv for _ in range(heads)]

    def add_rows(h, i, piece):
        att[h][i] = piece if att[h][i] is None else att[h][i] + piece

    sub = lax.broadcasted_iota(jnp.int32, (c, q32.shape[1]), 0)
    sub8 = lax.broadcasted_iota(jnp.int32, (HG_SUBLANES, q32.shape[1]), 0)
    n_wide = 0
    for li, hb in enumerate(HG_LEVELS):
        if hb >= HG_SUBLANES:
            parts, q_idx = [], []
            for i in range(nv):
                t0 = i * HG_SUBLANES
                ref = ref_rows(int(_hgrn_ref_row(direction, hb, t0)))
                if bool(_hgrn_q_row(direction, hb, t0)):
                    parts.append(rows(q32, i) * jnp.exp2(rows(cum, i) - ref))
                    q_idx.append(i)
                else:
                    parts.append(rows(k, i) * jnp.exp2(ref - rows(cum, i)))
            w = jnp.concatenate(parts, axis=0).astype(BF16)
            wq = jnp.concatenate([parts[i] for i in q_idx], axis=0).astype(BF16)
            for h in range(heads):
                a = _dot_nt(wq[:, hsl[h]], w[:, hsl[h]]) * wide_ref[direction, n_wide]
                for j, i in enumerate(q_idx):
                    add_rows(h, i, rows(a, j))
            n_wide += 1
        else:
            q_rows = _hgrn_q_row(direction, hb, sub)
            if hb == 1:
                w = jnp.where(q_rows, q32 * f, k)
            else:
                pieces = []
                for i in range(nv):
                    t0 = i * HG_SUBLANES
                    if 2 * hb == HG_SUBLANES:
                        pieces.append(ref_rows(int(_hgrn_ref_row(direction, hb, t0))))
                    else:
                        lo_ref = ref_rows(int(_hgrn_ref_row(direction, hb, t0)))
                        hi_ref = ref_rows(int(_hgrn_ref_row(direction, hb, t0 + 2 * hb)))
                        pieces.append(jnp.where(sub8 < 2 * hb, lo_ref, hi_ref))
                d = cum - jnp.concatenate(pieces, axis=0)
                w = jnp.where(q_rows, q32, k) * jnp.exp2(jnp.where(q_rows, d, -d))
            w = w.astype(BF16)
            for h in range(heads):
                a = _dot_nt(w[:, hsl[h]], w[:, hsl[h]]) * full_ref[direction, li - n_wide]
                for i in range(nv):
                    add_rows(h, i, rows(a, i))
        yield

    last = c - 1 if direction == 0 else 0
    total = ref_row(last)
    qi = (q32 * jnp.exp2(cum)).astype(BF16)
    ki = (k * jnp.exp2(total - cum)).astype(BF16)
    diag_in = (q32 * k).astype(BF16)
    ones = jnp.ones((HG_DK, HG_DV), BF16)
    decay = jnp.exp2(total)
    outs = []
    for h in range(heads):
        vh = v[:, hsl[h]]
        st = st_ref[direction, h]
        o = _dot(jnp.concatenate(att[h], axis=0).astype(BF16), vh)
        o = o + _dot(diag_in[:, hsl[h]], ones) * vh.astype(F32)
        o = o + _dot_nt(qi[:, hsl[h]], st.astype(BF16))
        st_ref[direction, h] = st * decay[:, hsl[h]] + _dot_tn(vh, ki[:, hsl[h]])
        outs.append(o)
    return jnp.concatenate(outs, axis=-1)


def _hgrn_kernel(q_ref, lf_ref, lb_ref, v_ref, tri_ref, wide_ref, full_ref,
                 o_ref, ob_ref, cum_ref, st_ref):
    t = q_ref.shape[1]
    c = HG_CHUNK
    n = t // c
    st_ref[...] = jnp.zeros_like(st_ref)

    def chunk(direction, gate_ref, r0, cum_slot):
        return _hgrn_chunk(direction, q_ref[0, pl.ds(r0, c), :].astype(F32), v_ref[0, pl.ds(r0, c), :],
                           gate_ref[0, pl.ds(r0, c), :], tri_ref, wide_ref, full_ref, cum_slot, st_ref)

    def body(i, carry):
        gens, dests = [], []
        for u in range(HG_UNROLL):
            rf = pl.multiple_of((i * HG_UNROLL + u) * c, c)
            rb = pl.multiple_of((n - 1 - i * HG_UNROLL - u) * c, c)
            gens += [chunk(0, lf_ref, rf, cum_ref.at[u]), chunk(1, lb_ref, rb, cum_ref.at[u])]
            dests += [(o_ref.at[0], rf), (ob_ref, rb)]
        for (dst, r0), o in zip(dests, _interleave(gens)):
            dst[pl.ds(r0, c), :] = o
        return carry

    lax.fori_loop(0, n // HG_UNROLL, body, 0)
    o_ref[0] = o_ref[0] + ob_ref[...]


def _hgrn_scan(q, lf, lb, v):
    b, t, _ = q.shape
    assert t % (HG_CHUNK * HG_UNROLL) == 0
    tri, wide, full = _hgrn_tables()
    tri = jnp.asarray(tri, BF16)
    wide = jnp.asarray(wide, F32)
    full = jnp.asarray(full, F32)
    width = HG_HEADS_PER_STEP * HG_DK
    seq = lambda: pl.BlockSpec((1, t, width), lambda i, j: (i, 0, j))
    return pl.pallas_call(
        _hgrn_kernel,
        grid=(b, HG_HEADS // HG_HEADS_PER_STEP),
        in_specs=[seq(), seq(), seq(), seq(),
                  _resident(tri.shape), _resident(wide.shape), _resident(full.shape)],
        out_specs=seq(),
        out_shape=jax.ShapeDtypeStruct((b, t, D_MODEL), F32),
        scratch_shapes=[pltpu.VMEM((t, width), F32),
                        pltpu.VMEM((HG_UNROLL, 2, HG_HEADS_PER_STEP, HG_CHUNK, HG_DK), F32),
                        pltpu.VMEM((2, HG_HEADS_PER_STEP, HG_DV, HG_DK), F32)],
        compiler_params=_params("parallel", "parallel"),
        name="hgrn_scan",
    )(q, lf, lb, v, tri, wide, full)


def _hgrn_mixer(x, mix_g, w_in, g_norm, w_out, lb):
    b, t, _ = x.shape
    m = b * t
    lb = lb.astype(F32).reshape(1, D_MODEL)
    q, lf, lbw, v, gate = _hgrn_proj(x.reshape(m, D_MODEL), mix_g, w_in, jnp.log(lb), jnp.log1p(-lb))
    r3 = lambda a: a.reshape(b, t, D_MODEL)
    o = _hgrn_scan(r3(q), r3(lf), r3(lbw), r3(v))
    return "hgrn", (o.reshape(m, D_MODEL), gate, g_norm.reshape(1, HG_DV), w_out)


def _na_proj_kernel(x_ref, g_ref, w_ref, qg_ref, kg_ref, bd_ref, q_ref, k_ref, v_ref):
    h = _rms(x_ref[...], g_ref[...]).astype(BF16)
    bd = bd_ref[...]

    def head_norm(y, gain, mult):
        width = bd.shape[0]
        parts = []
        for j in range(D_MODEL // width):
            ys = y[:, j * width:(j + 1) * width]
            ss = _dot((ys * ys).astype(BF16), bd)
            parts.append(ys * lax.rsqrt(ss * (1.0 / NA_HEAD_DIM) + NORM_EPS) * (gain * mult))
        return jnp.concatenate(parts, axis=-1)

    q = _dot(h, w_ref[:, 0:D_MODEL])
    q_ref[...] = head_norm(q, qg_ref[...], NA_HEAD_DIM ** -0.5 * LOG2E).astype(BF16)
    k = _dot(h, w_ref[:, D_MODEL:2 * D_MODEL])
    k_ref[...] = head_norm(k, kg_ref[...], 1.0).astype(BF16)
    v_ref[...] = _dot(h, w_ref[:, 2 * D_MODEL:3 * D_MODEL]).astype(BF16)


def _na_proj(x, g, w, qg, kg):
    m = x.shape[0]
    tm = min(TOKEN_TILE, m)
    blk = np.arange(2 * LANES) // NA_HEAD_DIM
    bd = jnp.asarray(blk[:, None] == blk[None, :], BF16)
    tile = lambda: pl.BlockSpec((tm, D_MODEL), lambda i: (i, 0))
    return pl.pallas_call(
        _na_proj_kernel,
        grid=(m // tm,),
        in_specs=[tile(), _resident((1, D_MODEL)), _resident(w.shape),
                  _resident(qg.shape), _resident(kg.shape), _resident(bd.shape)],
        out_specs=[tile(), tile(), tile()],
        out_shape=[jax.ShapeDtypeStruct((m, D_MODEL), BF16)] * 3,
        compiler_params=_params("parallel"),
        name="na_proj",
    )(x, g, w, qg, kg, bd)


def _na_group_start(g, rows):
    return jnp.clip(g * NA_GROUP - NA_WIN_R // 2, 0, rows - NA_UNION) if isinstance(g, jax.Array) else \
        int(np.clip(g * NA_GROUP - NA_WIN_R // 2, 0, rows - NA_UNION))


def _na_attn_kernel(rows, q_ref, k_ref, v_ref, bias_ref, cm_ref, o_ref):
    gq = NA_GROUP * GRID_W
    uk = NA_UNION * GRID_W
    n_groups = rows // NA_GROUP
    lane = lax.broadcasted_iota(jnp.int32, (gq, LANES), 1)
    first = lane < NA_HEAD_DIM
    head_sel = (jnp.where(first[0:1], 1.0, 0.0).astype(BF16), jnp.where(first[0:1], 0.0, 1.0).astype(BF16))

    def group(g):
        kind = jnp.where(g == 0, 0, jnp.where(g == n_groups - 1, 2, 1))
        qs = pl.multiple_of(g * gq, gq)
        ks = pl.multiple_of(_na_group_start(g, rows) * GRID_W, NA_GROUP * GRID_W)
        q = q_ref[0, pl.ds(qs, gq), :]
        kb = k_ref[0, pl.ds(ks, uk), :]
        vb = v_ref[0, pl.ds(ks, uk), :]
        s = _dot_nt(jnp.concatenate([q * head_sel[0], q * head_sel[1]], axis=0), kb)
        yield
        cm = cm_ref[kind]
        probs = []
        for hh in range(2):
            sh = s[hh * gq:(hh + 1) * gq] * cm + bias_ref[hh, kind]
            probs.append(jnp.exp2((sh - jnp.max(sh, axis=-1, keepdims=True)).astype(BF16)))
            yield
        v_ext = jnp.concatenate([vb, jnp.ones_like(vb)], axis=-1)
        o2 = _dot(jnp.concatenate(probs, axis=0), v_ext)
        o2 = o2[:, :LANES] * (1.0 / o2[:, LANES:])
        o_ref[0, pl.ds(qs, gq), :] = jnp.where(first, o2[:gq], o2[gq:]).astype(o_ref.dtype)

    def body(i, carry):
        _interleave([group(i * NA_UNROLL + u) for u in range(NA_UNROLL)])
        return carry

    lax.fori_loop(0, n_groups // NA_UNROLL, body, 0)


def _na_attn(q, k, v, bias, cm):
    b, t, _ = q.shape
    rows = t // GRID_W
    seq = lambda: pl.BlockSpec((1, t, LANES), lambda j, i: (i, 0, j))
    return pl.pallas_call(
        functools.partial(_na_attn_kernel, rows),
        grid=(D_MODEL // LANES, b),
        in_specs=[seq(), seq(), seq(),
                  pl.BlockSpec((2,) + bias.shape[1:], lambda j, i: (j, 0, 0, 0)),
                  _resident(cm.shape)],
        out_specs=seq(),
        out_shape=jax.ShapeDtypeStruct((b, t, D_MODEL), BF16),
        compiler_params=_params("parallel", "parallel"),
        name="na_attn",
    )(q, k, v, bias, cm)


def _na_tables(rpb, rows):
    assert rows % NA_GROUP == 0 and rows >= NA_UNION + NA_GROUP
    cols = np.arange(GRID_W)
    col_start = np.clip(cols - NA_WIN_C // 2, 0, GRID_W - NA_WIN_C)
    col_mask = (cols[None, :] >= col_start[:, None]) & (cols[None, :] < col_start[:, None] + NA_WIN_C)
    col_off = np.clip(cols[None, :] - cols[:, None] + NA_WIN_C - 1, 0, 2 * NA_WIN_C - 2)
    n_groups = rows // NA_GROUP
    by_col = rpb.astype(F32)[:, :, col_off].transpose(0, 2, 1, 3)
    pad = NA_UNION
    by_col = jnp.pad(by_col, ((0, 0), (0, 0), (pad, pad), (0, 0)))
    valid, kinds = [], []
    for g in (0, 1, n_groups - 1):
        start = _na_group_start(g, rows)
        key = start + np.arange(NA_UNION)[None, :]
        r = g * NA_GROUP + np.arange(NA_GROUP)[:, None]
        r0 = np.clip(r - NA_WIN_R // 2, 0, rows - NA_WIN_R)
        valid.append((key >= r0) & (key < r0 + NA_WIN_R))
        per_row = []
        for a in range(NA_GROUP):
            off = start - (g * NA_GROUP + a) + NA_WIN_R - 1 + pad
            per_row.append(by_col[:, :, off:off + NA_UNION, :])
        kinds.append(jnp.stack(per_row, axis=1))
    valid = np.stack(valid)
    assert (valid.sum(-1) == NA_WIN_R).all()
    full_valid = valid[:, :, None, :, None] & col_mask[None, None, :, None, :]
    shape = (3, NA_GROUP * GRID_W, NA_UNION * GRID_W)
    tab = jnp.stack(kinds, axis=1) * LOG2E
    tab = jnp.where(full_valid[None], tab, NEG_INF * LOG2E).reshape((NA_HEADS,) + shape)
    return tab, jnp.asarray(full_valid.reshape(shape), F32)


def _na_mixer(x, mix_g, w_in, q_norm, k_norm, rpb, w_out):
    b, t, _ = x.shape
    m = b * t
    tile2 = lambda g: jnp.tile(g.astype(F32), 2 * LANES // NA_HEAD_DIM).reshape(1, 2 * LANES)
    q, k, v = _na_proj(x.reshape(m, D_MODEL), mix_g, w_in, tile2(q_norm), tile2(k_norm))
    bias, cm = _na_tables(rpb, t // GRID_W)
    r3 = lambda a: a.reshape(b, t, D_MODEL)
    o = _na_attn(r3(q), r3(k), r3(v), bias, cm)
    return "proj", (o.reshape(m, D_MODEL), w_out)


_MLA_PERM = np.concatenate([
    MLA_NOPE + np.arange(16),
    np.arange(48),
    MLA_NOPE + 16 + np.arange(16),
    48 + np.arange(16),
    -np.ones(32, np.int64),
]).astype(np.int64)


def _mla_prep_kernel(x_ref, g_ref, win_ref, qa_ref, kva_ref, wuq_ref, wuk_ref, wuv_ref,
                     swap_ref, bd_ref, aq_ref, bq_ref, ak_ref, bk_ref, q_ref, k_ref, v_ref):
    h = _rms(x_ref[...], g_ref[...]).astype(BF16)
    c = _dot(h, win_ref[...])
    cq = _rms(c[:, :MLA_Q_LORA], qa_ref[...]).astype(BF16)
    ckv = _rms(c[:, MLA_Q_LORA:MLA_Q_LORA + MLA_KV_LORA], kva_ref[...]).astype(BF16)
    k_rope = c[:, MLA_Q_LORA + MLA_KV_LORA:]
    two = lambda a: jnp.concatenate([a, a], axis=-1)
    aq, bq, ak = two(aq_ref[...]), two(bq_ref[...]), two(ak_ref[...])
    k_rope2 = two(k_rope)
    k_partner = two(pltpu.roll(k_rope, LANES // 2, 1) * bk_ref[...])
    bd = bd_ref[...]

    def inv_rms(y):
        ss = _dot((y * y).astype(BF16), bd)
        return lax.rsqrt(ss * (1.0 / MLA_QK) + NORM_EPS)

    def pair(p):
        sl = slice(p * 2 * LANES, (p + 1) * 2 * LANES)
        yq = _dot(cq, wuq_ref[:, sl])
        yk = _dot(ckv, wuk_ref[:, sl]) + k_rope2
        yield
        yq_partner = _dot(yq.astype(BF16), swap_ref[...])
        rq = inv_rms(yq)
        rk = inv_rms(yk)
        yield
        q_ref[:, sl] = ((yq * aq + yq_partner * bq) * rq).astype(BF16)
        k_ref[:, sl] = ((yk * ak + k_partner) * rk).astype(BF16)

    for p0 in range(0, MLA_HEADS // 2, MLA_PREP_UNROLL):
        _interleave([pair(p) for p in range(p0, p0 + MLA_PREP_UNROLL)])
    v_ref[...] = _dot(ckv, wuv_ref[...]).astype(BF16)


def _mla_prep(x, g, win, qa, kva, wuq, wuk, wuv, swap, bd, aq, bq, ak, bk, t):
    m = x.shape[0]
    tm = min(TOKEN_TILE, t)
    per_seq = t // tm
    wide = MLA_HEADS * LANES
    table = lambda: pl.BlockSpec((tm, LANES), lambda i: (i % per_seq, 0))
    return pl.pallas_call(
        _mla_prep_kernel,
        grid=(m // tm,),
        in_specs=[pl.BlockSpec((tm, D_MODEL), lambda i: (i, 0)),
                  _resident((1, D_MODEL)), _resident(win.shape),
                  _resident((1, MLA_Q_LORA)), _resident((1, MLA_KV_LORA)),
                  _resident(wuq.shape), _resident(wuk.shape), _resident(wuv.shape),
                  _resident(swap.shape), _resident(bd.shape),
                  table(), table(), table(), table()],
        out_specs=[pl.BlockSpec((tm, wide), lambda i: (i, 0)),
                   pl.BlockSpec((tm, wide), lambda i: (i, 0)),
                   pl.BlockSpec((tm, D_MODEL), lambda i: (i, 0))],
        out_shape=[jax.ShapeDtypeStruct((m, wide), BF16),
                   jax.ShapeDtypeStruct((m, wide), BF16),
                   jax.ShapeDtypeStruct((m, D_MODEL), BF16)],
        compiler_params=_params("parallel"),
        name="mla_prep",
    )(x, g, win, qa, kva, wuq, wuk, wuv, swap, bd, aq, bq, ak, bk)


def _mla_attn_kernel(q_ref, k_ref, v_ref, o_ref):
    sub = MLA_Q_SUBTILE
    first = lax.broadcasted_iota(jnp.int32, (sub, LANES), 1) < MLA_V
    v_ext = jnp.concatenate([v_ref[0], jnp.ones((v_ref.shape[1], LANES), BF16)], axis=-1)

    def head(r0, hh):
        sl = slice(hh * LANES, (hh + 1) * LANES)
        s = _dot_nt(q_ref[0, r0:r0 + sub, sl], k_ref[0, :, sl])
        yield
        p = jnp.exp2((s - jnp.max(s, axis=-1, keepdims=True)).astype(BF16))
        yield
        o = _dot(p, v_ext)
        return o[:, :LANES] * (1.0 / o[:, LANES:])

    starts = range(0, q_ref.shape[1], sub)
    outs = _interleave([head(r0, hh) for r0 in starts for hh in range(2)])
    for n, r0 in enumerate(starts):
        o_ref[0, r0:r0 + sub, :] = jnp.where(first, outs[2 * n], outs[2 * n + 1]).astype(o_ref.dtype)


def _mla_attn(q, k, v):
    b, t, _ = q.shape
    tq = min(MLA_Q_TILE, t)
    return pl.pallas_call(
        _mla_attn_kernel,
        grid=(b, MLA_HEADS // 2, t // tq),
        in_specs=[pl.BlockSpec((1, tq, 2 * LANES), lambda i, j, l: (i, l, j)),
                  pl.BlockSpec((1, t, 2 * LANES), lambda i, j, l: (i, 0, j)),
                  pl.BlockSpec((1, t, LANES), lambda i, j, l: (i, 0, j))],
        out_specs=pl.BlockSpec((1, tq, LANES), lambda i, j, l: (i, l, j)),
        out_shape=jax.ShapeDtypeStruct((b, t, D_MODEL), BF16),
        compiler_params=_params("parallel", "parallel", "arbitrary"),
        name="mla_attn",
    )(q, k, v)


def _mla_weights(w_in, w_uq, w_ukv, q_norm, k_norm, t):
    valid = _MLA_PERM >= 0
    src = np.where(valid, _MLA_PERM, 0)
    is_rope = valid & (_MLA_PERM >= MLA_NOPE)
    is_nope = valid & (_MLA_PERM < MLA_NOPE)

    def place(w, per_head, lanes_ok, base=0):
        wh = w.reshape(w.shape[0], MLA_HEADS, per_head)[:, :, base + np.where(lanes_ok, src, 0)]
        return jnp.where(lanes_ok[None, None, :], wh, 0.0).reshape(w.shape[0], MLA_HEADS * LANES)

    wuq = place(w_uq, MLA_QK, valid)
    wuk = place(w_ukv, MLA_NOPE + MLA_V, is_nope)
    wuv = w_ukv.reshape(MLA_KV_LORA, MLA_HEADS, MLA_NOPE + MLA_V)[:, :, MLA_NOPE:].reshape(
        MLA_KV_LORA, MLA_HEADS * MLA_V)
    rope_cols = w_in[:, MLA_Q_LORA + MLA_KV_LORA:]
    rope_placed = jnp.where(is_rope[None, :], rope_cols[:, np.where(is_rope, src - MLA_NOPE, 0)], 0.0)
    win = jnp.concatenate([w_in[:, :MLA_Q_LORA + MLA_KV_LORA], rope_placed], axis=1)
    gain = lambda g: jnp.where(valid, g.astype(F32)[src], 0.0).reshape(1, LANES)
    partner = (np.arange(LANES) + LANES // 2) % LANES
    gain_partner = lambda g: jnp.where(is_rope, gain(g)[0, partner], 0.0).reshape(1, LANES)

    half = MLA_ROPE // 2
    inv_freq = ROPE_THETA ** (-jnp.arange(half, dtype=F32) / half)
    ang = jnp.arange(t).astype(F32)[:, None] * inv_freq[None, :]
    cos = jnp.ones((t, LANES), F32)
    cos = cos.at[:, 0:half].set(jnp.cos(ang)).at[:, 64:64 + half].set(jnp.cos(ang))
    sin = jnp.zeros((t, LANES), F32)
    sin = sin.at[:, 0:half].set(-jnp.sin(ang)).at[:, 64:64 + half].set(jnp.sin(ang))
    q_mult = MLA_QK ** -0.5 * LOG2E
    tables = (gain(q_norm) * cos * q_mult, gain_partner(q_norm) * sin * q_mult,
              gain(k_norm) * cos, gain_partner(k_norm) * sin)

    lane2 = np.arange(2 * LANES)
    rope2 = np.tile(is_rope, 2)
    swap = (lane2[:, None] == (lane2[None, :] // LANES) * LANES + np.tile(partner, 2)[None, :]) & rope2[None, :]
    bd = (lane2[:, None] // LANES) == (lane2[None, :] // LANES)
    return (win.astype(BF16), wuq.astype(BF16), wuk.astype(BF16), wuv.astype(BF16),
            jnp.asarray(swap, BF16), jnp.asarray(bd, BF16)) + tables


def _mla_mixer(x, mix_g, w_in, q_a_norm, w_uq, kv_a_norm, w_ukv, q_norm, k_norm, w_out):
    b, t, _ = x.shape
    m = b * t
    prepared = _mla_weights(w_in, w_uq, w_ukv, q_norm, k_norm, t)
    q, k, v = _mla_prep(x.reshape(m, D_MODEL), mix_g, prepared[0], q_a_norm.astype(F32).reshape(1, -1),
                        kv_a_norm.astype(F32).reshape(1, -1), *prepared[1:], t)
    o = _mla_attn(q.reshape(b, t, -1), k.reshape(b, t, -1), v.reshape(b, t, -1))
    return "proj", (o.reshape(m, D_MODEL), w_out)


def kernel(x, ffn1_norm, ffn1_w_gu, ffn1_w_down, mix_norm, ffn2_norm, ffn2_w_gu, ffn2_w_down,
           hg_lb_logits, hg_w_in, hg_g_norm, hg_w_out,
           na_w_in, na_q_norm, na_k_norm, na_rpb, na_w_out,
           mla_w_in, mla_q_a_norm, mla_w_uq, mla_kv_a_norm, mla_w_ukv, mla_q_norm, mla_k_norm, mla_w_out):
    b, t, d = x.shape
    m = b * t
    gam = jnp.cumsum(jax.nn.softmax(hg_lb_logits.astype(F32), axis=0), axis=0)
    lb_all = gam - gam[0:1]
    row = lambda g: g.astype(F32).reshape(1, -1)
    bf = lambda w: w.astype(BF16)
    ia = ib = ic = 0
    for layer in range(DEPTH):
        x = _ffn(x.reshape(m, d), row(ffn1_norm[layer]), bf(ffn1_w_gu[layer]),
                 bf(ffn1_w_down[layer])).reshape(b, t, d)
        g = row(mix_norm[layer])
        kind = layer % N_MIXERS
        if kind == 0:
            mixer = _hgrn_mixer(x, g, bf(hg_w_in[ia]), hg_g_norm[ia].astype(F32), bf(hg_w_out[ia]), lb_all[layer])
            ia += 1
        elif kind == 1:
            mixer = _na_mixer(x, g, bf(na_w_in[ib]), na_q_norm[ib], na_k_norm[ib], na_rpb[ib], bf(na_w_out[ib]))
            ib += 1
        else:
            mixer = _mla_mixer(x, g, mla_w_in[ic], mla_q_a_norm[ic], mla_w_uq[ic], mla_kv_a_norm[ic],
                               mla_w_ukv[ic], mla_q_norm[ic], mla_k_norm[ic], bf(mla_w_out[ic]))
            ic += 1
        x = _ffn(x.reshape(m, d), row(ffn2_norm[layer]), bf(ffn2_w_gu[layer]),
                 bf(ffn2_w_down[layer]), *mixer).reshape(b, t, d)
    return x
```

```python
import functools

import numpy as np
import jax
import jax.numpy as jnp
from jax import lax
from jax.experimental import pallas as pl
from jax.experimental.pallas import tpu as pltpu

D_MODEL = 1024
DEPTH = 4
N_MIXERS = 3
GRID_W = 64
D_FF = 2816
NORM_EPS = 1e-6
HG_HEADS = 8
HG_DK = 128
HG_DV = 128
NA_HEADS = 16
NA_HEAD_DIM = 64
NA_WIN_R = 8
NA_WIN_C = 16
MLA_HEADS = 16
MLA_Q_LORA = 768
MLA_KV_LORA = 256
MLA_NOPE = 64
MLA_ROPE = 32
MLA_V = 64
MLA_QK = MLA_NOPE + MLA_ROPE
ROPE_THETA = 10000.0
NEG_INF = -1e30
LOG2E = 1.4426950408889634

LANES = 128
VMEM_LIMIT = 56 * 1024 * 1024
TOKEN_TILE = 512
HG_CHUNK = 128
HG_LEVELS = (64, 32, 16, 8, 4, 2, 1)
HG_SUBLANES = 8
HG_HEADS_PER_STEP = 2
HG_UNROLL = 8
MLA_Q_TILE = 1024
MLA_Q_SUBTILE = 512
MLA_PREP_UNROLL = 2
HG_PROJ_ROW_SPLITS = 4
NA_GROUP = 4
NA_UNION = NA_WIN_R + NA_GROUP
NA_UNROLL = 4
F32 = jnp.float32
BF16 = jnp.bfloat16


def _params(*sem):
    return pltpu.CompilerParams(dimension_semantics=sem, vmem_limit_bytes=VMEM_LIMIT)


def _resident(shape):
    nd = len(shape)
    return pl.BlockSpec(shape, lambda *_: (0,) * nd, pipeline_mode=pl.Buffered(1))


def _rms(x, g):
    ms = jnp.mean(x * x, axis=-1, keepdims=True)
    return x * lax.rsqrt(ms + NORM_EPS) * g


def _silu(x):
    return x * (1.0 / (1.0 + jnp.exp(-x)))


def _dot(a, b):
    return jnp.dot(a, b, preferred_element_type=F32)


def _dot_nt(a, b):
    return lax.dot_general(a, b, (((1,), (1,)), ((), ())), preferred_element_type=F32)


def _interleave(gens):
    results = [None] * len(gens)
    pending = set(range(len(gens)))
    while pending:
        for i in sorted(pending):
            try:
                next(gens[i])
            except StopIteration as stop:
                results[i] = stop.value
                pending.discard(i)
    return results


def _dot_tn(a, b):
    return lax.dot_general(a, b, (((0,), (0,)), ((), ())), preferred_element_type=F32)


FF_CHUNKS = ((0, 1536), (1536, 2816))
FFN_ROW_SPLITS = 2


def _ffn_body(x, g, wgu_ref, wd_ref):
    h = _rms(x, g).astype(BF16)
    acc = None
    for s, e in FF_CHUNKS:
        gate = _dot(h, wgu_ref[:, s:e])
        up = _dot(h, wgu_ref[:, D_FF + s:D_FF + e])
        yield
        a = (_silu(gate) * up).astype(BF16)
        d = _dot(a, wd_ref[s:e, :])
        acc = d if acc is None else acc + d
        yield
    return x + 0.5 * acc


def _hgrn_gated(o_ref, gate_ref, gn, rs):
    parts = []
    for h in range(HG_HEADS):
        sl = slice(h * HG_DV, (h + 1) * HG_DV)
        parts.append(_rms(o_ref[rs, sl], gn) * _silu(gate_ref[rs, sl]))
    return jnp.concatenate(parts, axis=-1).astype(BF16)


def _ffn_kernel(mixer, *refs):
    *mixer_refs, x_ref, g_ref, wgu_ref, wd_ref, o_ref = refs
    rows = x_ref.shape[0] // FFN_ROW_SPLITS

    def part(r0):
        rs = slice(r0, r0 + rows)
        x = x_ref[rs, :]
        if mixer == "proj":
            a_ref, w_ref = mixer_refs
            x = x + _dot(a_ref[rs, :], w_ref[...])
        elif mixer == "hgrn":
            s_ref, gate_ref, gn_ref, w_ref = mixer_refs
            x = x + _dot(_hgrn_gated(s_ref, gate_ref, gn_ref[...], rs), w_ref[...])
        yield
        o_ref[rs, :] = yield from _ffn_body(x, g_ref[...], wgu_ref, wd_ref)

    _interleave([part(r0) for r0 in range(0, x_ref.shape[0], rows)])


def _ffn(x, g, wgu, wd, mixer=None, mixer_args=()):
    m = x.shape[0]
    tm = min(TOKEN_TILE, m)
    tile = lambda a: pl.BlockSpec((tm, a.shape[1]), lambda i: (i, 0))
    n_tiled = {None: 0, "proj": 1, "hgrn": 2}[mixer]
    mixer_specs = [tile(a) if i < n_tiled else _resident(a.shape) for i, a in enumerate(mixer_args)]
    return pl.pallas_call(
        functools.partial(_ffn_kernel, mixer),
        grid=(m // tm,),
        in_specs=mixer_specs + [tile(x), _resident((1, D_MODEL)), _resident(wgu.shape), _resident(wd.shape)],
        out_specs=tile(x),
        out_shape=jax.ShapeDtypeStruct((m, D_MODEL), F32),
        compiler_params=_params("parallel"),
        name="ffn" if mixer is None else "ffn_" + mixer,
    )(*mixer_args, x, g, wgu, wd)


def _hgrn_q_row(direction, hb, t):
    return ((t % (2 * hb)) >= hb) != (direction == 1)


def _hgrn_ref_row(direction, hb, t):
    return (t // (2 * hb)) * (2 * hb) + (hb - 1 if direction == 0 else hb)


def _hgrn_tables():
    c = HG_CHUNK
    t = np.arange(c)[:, None]
    s = np.arange(c)[None, :]
    tri = np.stack([s <= t, s >= t]).astype(np.float32)
    wide, full = [], []
    for d in range(2):
        wide_d, full_d = [], []
        for hb in HG_LEVELS:
            same = (t // (2 * hb)) == (s // (2 * hb))
            own = (_hgrn_q_row(d, hb, t) & same & ~_hgrn_q_row(d, hb, s)).astype(np.float32)
            if hb >= HG_SUBLANES:
                wide_d.append(own[_hgrn_q_row(d, hb, np.arange(c))])
            else:
                full_d.append(own)
        wide.append(np.stack(wide_d))
        full.append(np.stack(full_d))
    return tri, np.stack(wide), np.stack(full)


def _hgrn_proj_kernel(x_ref, g_ref, w_ref, llb_ref, l1m_ref, q_ref, lf_ref, lb_ref, v_ref, gate_ref):
    h = _rms(x_ref[...], g_ref[...]).astype(BF16)
    llb = llb_ref[...]
    l1m = l1m_ref[...]

    def log2_decay(z):
        ls = jnp.minimum(z, 0.0) - jnp.log(1.0 + jnp.exp2(jnp.abs(z) * -LOG2E))
        y = l1m + ls
        return (jnp.maximum(llb, y) + jnp.log(1.0 + jnp.exp2(jnp.abs(llb - y) * -LOG2E))) * LOG2E

    finishes = ((q_ref, lambda y: (y * HG_DK ** -0.5).astype(BF16)), (lf_ref, log2_decay), (lb_ref, log2_decay),
                (v_ref, lambda y: y.astype(BF16)), (gate_ref, lambda y: y))
    rows = h.shape[0] // HG_PROJ_ROW_SPLITS

    def part(r0):
        hs = h[r0:r0 + rows]
        for j, (o_ref, finish) in enumerate(finishes):
            y = _dot(hs, w_ref[:, j * D_MODEL:(j + 1) * D_MODEL])
            yield
            o_ref[r0:r0 + rows, :] = finish(y)

    _interleave([part(r0) for r0 in range(0, h.shape[0], rows)])


def _hgrn_proj(x, g, w, llb, l1m):
    m = x.shape[0]
    tm = min(TOKEN_TILE, m)
    tile = lambda: pl.BlockSpec((tm, D_MODEL), lambda i: (i, 0))
    return pl.pallas_call(
        _hgrn_proj_kernel,
        grid=(m // tm,),
        in_specs=[tile(), _resident((1, D_MODEL)), _resident(w.shape),
                  _resident((1, D_MODEL)), _resident((1, D_MODEL))],
        out_specs=[tile()] * 5,
        out_shape=[jax.ShapeDtypeStruct((m, D_MODEL), dt) for dt in (BF16, F32, F32, BF16, F32)],
        compiler_params=_params("parallel"),
        name="hgrn_proj",
    )(x, g, w, llb, l1m)


def _hgrn_chunk(direction, q32, v, logf2, tri_ref, wide_ref, full_ref, cum_ref, st_ref):
    c = HG_CHUNK
    nv = c // HG_SUBLANES
    rows = lambda a, i: a[i * HG_SUBLANES:(i + 1) * HG_SUBLANES]
    f = jnp.exp2(logf2)
    k = 1.0 - f
    hi = logf2.astype(BF16)
    lo = (logf2 - hi.astype(F32)).astype(BF16)
    tri = tri_ref[direction]
    cum = _dot(tri, hi) + _dot(tri, lo)
    heads = HG_HEADS_PER_STEP
    hsl = [slice(h * HG_DK, (h + 1) * HG_DK) for h in range(heads)]
    for h in range(heads):
        cum_ref[direction, h] = cum[:, hsl[h]]
    yield
    ref_row = lambda r: jnp.concatenate([cum_ref[direction, h, r:r + 1, :] for h in range(heads)], axis=-1)
    ref_rows = lambda r: jnp.broadcast_to(ref_row(r), (HG_SUBLANES, q32.shape[1]))
    att = [[None] * nv for _ in range(heads)]

    def add_rows(h, i, piece):
        att[h][i] = piece if att[h][i] is None else att[h][i] + piece

    sub = lax.broadcasted_iota(jnp.int32, (c, q32.shape[1]), 0)
    sub8 = lax.broadcasted_iota(jnp.int32, (HG_SUBLANES, q32.shape[1]), 0)
    n_wide = 0
    for li, hb in enumerate(HG_LEVELS):
        if hb >= HG_SUBLANES:
            parts, q_idx = [], []
            for i in range(nv):
                t0 = i * HG_SUBLANES
                ref = ref_rows(int(_hgrn_ref_row(direction, hb, t0)))
                if bool(_hgrn_q_row(direction, hb, t0)):
                    parts.append(rows(q32, i) * jnp.exp2(rows(cum, i) - ref))
                    q_idx.append(i)
                else:
                    parts.append(rows(k, i) * jnp.exp2(ref - rows(cum, i)))
            w = jnp.concatenate(parts, axis=0).astype(BF16)
            wq = jnp.concatenate([parts[i] for i in q_idx], axis=0).astype(BF16)
            for h in range(heads):
                a = _dot_nt(wq[:, hsl[h]], w[:, hsl[h]]) * wide_ref[direction, n_wide]
                for j, i in enumerate(q_idx):
                    add_rows(h, i, rows(a, j))
            n_wide += 1
        else:
            q_rows = _hgrn_q_row(direction, hb, sub)
            if hb == 1:
                w = jnp.where(q_rows, q32 * f, k)
            else:
                pieces = []
                for i in range(nv):
                    t0 = i * HG_SUBLANES
                    if 2 * hb == HG_SUBLANES:
                        pieces.append(ref_rows(int(_hgrn_ref_row(direction, hb, t0))))
                    else:
                        lo_ref = ref_rows(int(_hgrn_ref_row(direction, hb, t0)))
                        hi_ref = ref_rows(int(_hgrn_ref_row(direction, hb, t0 + 2 * hb)))
                        pieces.append(jnp.where(sub8 < 2 * hb, lo_ref, hi_ref))
                d = cum - jnp.concatenate(pieces, axis=0)
                w = jnp.where(q_rows, q32, k) * jnp.exp2(jnp.where(q_rows, d, -d))
            w = w.astype(BF16)
            for h in range(heads):
                a = _dot_nt(w[:, hsl[h]], w[:, hsl[h]]) * full_ref[direction, li - n_wide]
                for i in range(nv):
                    add_rows(h, i, rows(a, i))
        yield

    last = c - 1 if direction == 0 else 0
    total = ref_row(last)
    qi = (q32 * jnp.exp2(cum)).astype(BF16)
    ki = (k * jnp.exp2(total - cum)).astype(BF16)
    diag_in = (q32 * k).astype(BF16)
    ones = jnp.ones((HG_DK, HG_DV), BF16)
    decay = jnp.exp2(total)
    outs = []
    for h in range(heads):
        vh = v[:, hsl[h]]
        st = st_ref[direction, h]
        o = _dot(jnp.concatenate(att[h], axis=0).astype(BF16), vh)
        o = o + _dot(diag_in[:, hsl[h]], ones) * vh.astype(F32)
        o = o + _dot_nt(qi[:, hsl[h]], st.astype(BF16))
        st_ref[direction, h] = st * decay[:, hsl[h]] + _dot_tn(vh, ki[:, hsl[h]])
        outs.append(o)
    return jnp.concatenate(outs, axis=-1)


def _hgrn_kernel(q_ref, lf_ref, lb_ref, v_ref, tri_ref, wide_ref, full_ref,
                 o_ref, ob_ref, cum_ref, st_ref):
    t = q_ref.shape[1]
    c = HG_CHUNK
    n = t // c
    st_ref[...] = jnp.zeros_like(st_ref)

    def chunk(direction, gate_ref, r0, cum_slot):
        return _hgrn_chunk(direction, q_ref[0, pl.ds(r0, c), :].astype(F32), v_ref[0, pl.ds(r0, c), :],
                           gate_ref[0, pl.ds(r0, c), :], tri_ref, wide_ref, full_ref, cum_slot, st_ref)

    def body(i, carry):
        gens, dests = [], []
        for u in range(HG_UNROLL):
            rf = pl.multiple_of((i * HG_UNROLL + u) * c, c)
            rb = pl.multiple_of((n - 1 - i * HG_UNROLL - u) * c, c)
            gens += [chunk(0, lf_ref, rf, cum_ref.at[u]), chunk(1, lb_ref, rb, cum_ref.at[u])]
            dests += [(o_ref.at[0], rf), (ob_ref, rb)]
        for (dst, r0), o in zip(dests, _interleave(gens)):
            dst[pl.ds(r0, c), :] = o
        return carry

    lax.fori_loop(0, n // HG_UNROLL, body, 0)
    o_ref[0] = o_ref[0] + ob_ref[...]


def _hgrn_scan(q, lf, lb, v):
    b, t, _ = q.shape
    assert t % (HG_CHUNK * HG_UNROLL) == 0
    tri, wide, full = _hgrn_tables()
    tri = jnp.asarray(tri, BF16)
    wide = jnp.asarray(wide, F32)
    full = jnp.asarray(full, F32)
    width = HG_HEADS_PER_STEP * HG_DK
    seq = lambda: pl.BlockSpec((1, t, width), lambda i, j: (i, 0, j))
    return pl.pallas_call(
        _hgrn_kernel,
        grid=(b, HG_HEADS // HG_HEADS_PER_STEP),
        in_specs=[seq(), seq(), seq(), seq(),
                  _resident(tri.shape), _resident(wide.shape), _resident(full.shape)],
        out_specs=seq(),
        out_shape=jax.ShapeDtypeStruct((b, t, D_MODEL), F32),
        scratch_shapes=[pltpu.VMEM((t, width), F32),
                        pltpu.VMEM((HG_UNROLL, 2, HG_HEADS_PER_STEP, HG_CHUNK, HG_DK), F32),
                        pltpu.VMEM((2, HG_HEADS_PER_STEP, HG_DV, HG_DK), F32)],
        compiler_params=_params("parallel", "parallel"),
        name="hgrn_scan",
    )(q, lf, lb, v, tri, wide, full)


def _hgrn_mixer(x, mix_g, w_in, g_norm, w_out, lb):
    b, t, _ = x.shape
    m = b * t
    lb = lb.astype(F32).reshape(1, D_MODEL)
    q, lf, lbw, v, gate = _hgrn_proj(x.reshape(m, D_MODEL), mix_g, w_in, jnp.log(lb), jnp.log1p(-lb))
    r3 = lambda a: a.reshape(b, t, D_MODEL)
    o = _hgrn_scan(r3(q), r3(lf), r3(lbw), r3(v))
    return "hgrn", (o.reshape(m, D_MODEL), gate, g_norm.reshape(1, HG_DV), w_out)


def _na_proj_kernel(x_ref, g_ref, w_ref, qg_ref, kg_ref, bd_ref, q_ref, k_ref, v_ref):
    h = _rms(x_ref[...], g_ref[...]).astype(BF16)
    bd = bd_ref[...]

    def head_norm(y, gain, mult):
        width = bd.shape[0]
        parts = []
        for j in range(D_MODEL // width):
            ys = y[:, j * width:(j + 1) * width]
            ss = _dot((ys * ys).astype(BF16), bd)
            parts.append(ys * lax.rsqrt(ss * (1.0 / NA_HEAD_DIM) + NORM_EPS) * (gain * mult))
        return jnp.concatenate(parts, axis=-1)

    q = _dot(h, w_ref[:, 0:D_MODEL])
    q_ref[...] = head_norm(q, qg_ref[...], NA_HEAD_DIM ** -0.5 * LOG2E).astype(BF16)
    k = _dot(h, w_ref[:, D_MODEL:2 * D_MODEL])
    k_ref[...] = head_norm(k, kg_ref[...], 1.0).astype(BF16)
    v_ref[...] = _dot(h, w_ref[:, 2 * D_MODEL:3 * D_MODEL]).astype(BF16)


def _na_proj(x, g, w, qg, kg):
    m = x.shape[0]
    tm = min(TOKEN_TILE, m)
    blk = np.arange(2 * LANES) // NA_HEAD_DIM
    bd = jnp.asarray(blk[:, None] == blk[None, :], BF16)
    tile = lambda: pl.BlockSpec((tm, D_MODEL), lambda i: (i, 0))
    return pl.pallas_call(
        _na_proj_kernel,
        grid=(m // tm,),
        in_specs=[tile(), _resident((1, D_MODEL)), _resident(w.shape),
                  _resident(qg.shape), _resident(kg.shape), _resident(bd.shape)],
        out_specs=[tile(), tile(), tile()],
        out_shape=[jax.ShapeDtypeStruct((m, D_MODEL), BF16)] * 3,
        compiler_params=_params("parallel"),
        name="na_proj",
    )(x, g, w, qg, kg, bd)


def _na_group_start(g, rows):
    return jnp.clip(g * NA_GROUP - NA_WIN_R // 2, 0, rows - NA_UNION) if isinstance(g, jax.Array) else \
        int(np.clip(g * NA_GROUP - NA_WIN_R // 2, 0, rows - NA_UNION))


def _na_attn_kernel(rows, q_ref, k_ref, v_ref, bias_ref, cm_ref, o_ref):
    gq = NA_GROUP * GRID_W
    uk = NA_UNION * GRID_W
    n_groups = rows // NA_GROUP
    lane = lax.broadcasted_iota(jnp.int32, (gq, LANES), 1)
    first = lane < NA_HEAD_DIM
    head_sel = (jnp.where(first[0:1], 1.0, 0.0).astype(BF16), jnp.where(first[0:1], 0.0, 1.0).astype(BF16))

    def group(g):
        kind = jnp.where(g == 0, 0, jnp.where(g == n_groups - 1, 2, 1))
        qs = pl.multiple_of(g * gq, gq)
        ks = pl.multiple_of(_na_group_start(g, rows) * GRID_W, NA_GROUP * GRID_W)
        q = q_ref[0, pl.ds(qs, gq), :]
        kb = k_ref[0, pl.ds(ks, uk), :]
        vb = v_ref[0, pl.ds(ks, uk), :]
        s = _dot_nt(jnp.concatenate([q * head_sel[0], q * head_sel[1]], axis=0), kb)
        yield
        cm = cm_ref[kind]
        probs = []
        for hh in range(2):
            sh = s[hh * gq:(hh + 1) * gq] * cm + bias_ref[hh, kind]
            probs.append(jnp.exp2((sh - jnp.max(sh, axis=-1, keepdims=True)).astype(BF16)))
            yield
        v_ext = jnp.concatenate([vb, jnp.ones_like(vb)], axis=-1)
        o2 = _dot(jnp.concatenate(probs, axis=0), v_ext)
        o2 = o2[:, :LANES] * (1.0 / o2[:, LANES:])
        o_ref[0, pl.ds(qs, gq), :] = jnp.where(first, o2[:gq], o2[gq:]).astype(o_ref.dtype)

    def body(i, carry):
        _interleave([group(i * NA_UNROLL + u) for u in range(NA_UNROLL)])
        return carry

    lax.fori_loop(0, n_groups // NA_UNROLL, body, 0)


def _na_attn(q, k, v, bias, cm):
    b, t, _ = q.shape
    rows = t // GRID_W
    seq = lambda: pl.BlockSpec((1, t, LANES), lambda j, i: (i, 0, j))
    return pl.pallas_call(
        functools.partial(_na_attn_kernel, rows),
        grid=(D_MODEL // LANES, b),
        in_specs=[seq(), seq(), seq(),
                  pl.BlockSpec((2,) + bias.shape[1:], lambda j, i: (j, 0, 0, 0)),
                  _resident(cm.shape)],
        out_specs=seq(),
        out_shape=jax.ShapeDtypeStruct((b, t, D_MODEL), BF16),
        compiler_params=_params("parallel", "parallel"),
        name="na_attn",
    )(q, k, v, bias, cm)


def _na_tables(rpb, rows):
    assert rows % NA_GROUP == 0 and rows >= NA_UNION + NA_GROUP
    cols = np.arange(GRID_W)
    col_start = np.clip(cols - NA_WIN_C // 2, 0, GRID_W - NA_WIN_C)
    col_mask = (cols[None, :] >= col_start[:, None]) & (cols[None, :] < col_start[:, None] + NA_WIN_C)
    col_off = np.clip(cols[None, :] - cols[:, None] + NA_WIN_C - 1, 0, 2 * NA_WIN_C - 2)
    n_groups = rows // NA_GROUP
    by_col = rpb.astype(F32)[:, :, col_off].transpose(0, 2, 1, 3)
    pad = NA_UNION
    by_col = jnp.pad(by_col, ((0, 0), (0, 0), (pad, pad), (0, 0)))
    by_col = by_col.reshape(NA_HEADS, GRID_W, -1) * LOG2E
    valid, kinds = [], []
    for g in (0, 1, n_groups - 1):
        start = _na_group_start(g, rows)
        key = start + np.arange(NA_UNION)[None, :]
        r = g * NA_GROUP + np.arange(NA_GROUP)[:, None]
        r0 = np.clip(r - NA_WIN_R // 2, 0, rows - NA_WIN_R)
        valid.append((key >= r0) & (key < r0 + NA_WIN_R))
        per_row = []
        for a in range(NA_GROUP):
            off = start - (g * NA_GROUP + a) + NA_WIN_R - 1 + pad
            per_row.append(by_col[:, :, off * GRID_W:(off + NA_UNION) * GRID_W])
        kinds.append(jnp.stack(per_row, axis=1))
    valid = np.stack(valid)
    assert (valid.sum(-1) == NA_WIN_R).all()
    full_valid = valid[:, :, None, :, None] & col_mask[None, None, :, None, :]
    shape = (3, NA_GROUP * GRID_W, NA_UNION * GRID_W)
    full_valid = full_valid.reshape(shape)
    tab = jnp.stack(kinds, axis=1).reshape((NA_HEADS,) + shape)
    tab = jnp.where(full_valid[None], tab, NEG_INF * LOG2E)
    return tab, jnp.asarray(full_valid, F32)


def _na_mixer(x, mix_g, w_in, q_norm, k_norm, rpb, w_out):
    b, t, _ = x.shape
    m = b * t
    tile2 = lambda g: jnp.tile(g.astype(F32), 2 * LANES // NA_HEAD_DIM).reshape(1, 2 * LANES)
    q, k, v = _na_proj(x.reshape(m, D_MODEL), mix_g, w_in, tile2(q_norm), tile2(k_norm))
    bias, cm = _na_tables(rpb, t // GRID_W)
    r3 = lambda a: a.reshape(b, t, D_MODEL)
    o = _na_attn(r3(q), r3(k), r3(v), bias, cm)
    return "proj", (o.reshape(m, D_MODEL), w_out)


_MLA_PERM = np.concatenate([
    MLA_NOPE + np.arange(16),
    np.arange(48),
    MLA_NOPE + 16 + np.arange(16),
    48 + np.arange(16),
    -np.ones(32, np.int64),
]).astype(np.int64)


def _mla_prep_kernel(x_ref, g_ref, win_ref, qa_ref, kva_ref, wuq_ref, wuk_ref, wuv_ref,
                     swap_ref, bd_ref, aq_ref, bq_ref, ak_ref, bk_ref, q_ref, k_ref, v_ref):
    h = _rms(x_ref[...], g_ref[...]).astype(BF16)
    c = _dot(h, win_ref[...])
    cq = _rms(c[:, :MLA_Q_LORA], qa_ref[...]).astype(BF16)
    ckv = _rms(c[:, MLA_Q_LORA:MLA_Q_LORA + MLA_KV_LORA], kva_ref[...]).astype(BF16)
    k_rope = c[:, MLA_Q_LORA + MLA_KV_LORA:]
    two = lambda a: jnp.concatenate([a, a], axis=-1)
    aq, bq, ak = two(aq_ref[...]), two(bq_ref[...]), two(ak_ref[...])
    k_rope2 = two(k_rope)
    k_partner = two(pltpu.roll(k_rope, LANES // 2, 1) * bk_ref[...])
    bd = bd_ref[...]

    def inv_rms(y):
        ss = _dot((y * y).astype(BF16), bd)
        return lax.rsqrt(ss * (1.0 / MLA_QK) + NORM_EPS)

    def pair(p):
        sl = slice(p * 2 * LANES, (p + 1) * 2 * LANES)
        yq = _dot(cq, wuq_ref[:, sl])
        yk = _dot(ckv, wuk_ref[:, sl]) + k_rope2
        yield
        yq_partner = _dot(yq.astype(BF16), swap_ref[...])
        rq = inv_rms(yq)
        rk = inv_rms(yk)
        yield
        q_ref[:, sl] = ((yq * aq + yq_partner * bq) * rq).astype(BF16)
        k_ref[:, sl] = ((yk * ak + k_partner) * rk).astype(BF16)

    for p0 in range(0, MLA_HEADS // 2, MLA_PREP_UNROLL):
        _interleave([pair(p) for p in range(p0, p0 + MLA_PREP_UNROLL)])
    v_ref[...] = _dot(ckv, wuv_ref[...]).astype(BF16)


def _mla_prep(x, g, win, qa, kva, wuq, wuk, wuv, swap, bd, aq, bq, ak, bk, t):
    m = x.shape[0]
    tm = min(TOKEN_TILE, t)
    per_seq = t // tm
    wide = MLA_HEADS * LANES
    table = lambda: pl.BlockSpec((tm, LANES), lambda i: (i % per_seq, 0))
    return pl.pallas_call(
        _mla_prep_kernel,
        grid=(m // tm,),
        in_specs=[pl.BlockSpec((tm, D_MODEL), lambda i: (i, 0)),
                  _resident((1, D_MODEL)), _resident(win.shape),
                  _resident((1, MLA_Q_LORA)), _resident((1, MLA_KV_LORA)),
                  _resident(wuq.shape), _resident(wuk.shape), _resident(wuv.shape),
                  _resident(swap.shape), _resident(bd.shape),
                  table(), table(), table(), table()],
        out_specs=[pl.BlockSpec((tm, wide), lambda i: (i, 0)),
                   pl.BlockSpec((tm, wide), lambda i: (i, 0)),
                   pl.BlockSpec((tm, D_MODEL), lambda i: (i, 0))],
        out_shape=[jax.ShapeDtypeStruct((m, wide), BF16),
                   jax.ShapeDtypeStruct((m, wide), BF16),
                   jax.ShapeDtypeStruct((m, D_MODEL), BF16)],
        compiler_params=_params("parallel"),
        name="mla_prep",
    )(x, g, win, qa, kva, wuq, wuk, wuv, swap, bd, aq, bq, ak, bk)


def _mla_attn_kernel(q_ref, k_ref, v_ref, o_ref):
    sub = MLA_Q_SUBTILE
    first = lax.broadcasted_iota(jnp.int32, (sub, LANES), 1) < MLA_V
    v_ext = jnp.concatenate([v_ref[0], jnp.ones((v_ref.shape[1], LANES), BF16)], axis=-1)

    def head(r0, hh):
        sl = slice(hh * LANES, (hh + 1) * LANES)
        s = _dot_nt(q_ref[0, r0:r0 + sub, sl], k_ref[0, :, sl])
        yield
        p = jnp.exp2((s - jnp.max(s, axis=-1, keepdims=True)).astype(BF16))
        yield
        o = _dot(p, v_ext)
        return o[:, :LANES] * (1.0 / o[:, LANES:])

    starts = range(0, q_ref.shape[1], sub)
    outs = _interleave([head(r0, hh) for r0 in starts for hh in range(2)])
    for n, r0 in enumerate(starts):
        o_ref[0, r0:r0 + sub, :] = jnp.where(first, outs[2 * n], outs[2 * n + 1]).astype(o_ref.dtype)


def _mla_attn(q, k, v):
    b, t, _ = q.shape
    tq = min(MLA_Q_TILE, t)
    return pl.pallas_call(
        _mla_attn_kernel,
        grid=(b, MLA_HEADS // 2, t // tq),
        in_specs=[pl.BlockSpec((1, tq, 2 * LANES), lambda i, j, l: (i, l, j)),
                  pl.BlockSpec((1, t, 2 * LANES), lambda i, j, l: (i, 0, j)),
                  pl.BlockSpec((1, t, LANES), lambda i, j, l: (i, 0, j))],
        out_specs=pl.BlockSpec((1, tq, LANES), lambda i, j, l: (i, l, j)),
        out_shape=jax.ShapeDtypeStruct((b, t, D_MODEL), BF16),
        compiler_params=_params("parallel", "parallel", "arbitrary"),
        name="mla_attn",
    )(q, k, v)


def _mla_weights(w_in, w_uq, w_ukv, q_norm, k_norm, t):
    valid = _MLA_PERM >= 0
    src = np.where(valid, _MLA_PERM, 0)
    is_rope = valid & (_MLA_PERM >= MLA_NOPE)
    is_nope = valid & (_MLA_PERM < MLA_NOPE)

    def place(w, per_head, lanes_ok, base=0):
        wh = w.reshape(w.shape[0], MLA_HEADS, per_head)[:, :, base + np.where(lanes_ok, src, 0)]
        return jnp.where(lanes_ok[None, None, :], wh, 0.0).reshape(w.shape[0], MLA_HEADS * LANES)

    wuq = place(w_uq, MLA_QK, valid)
    wuk = place(w_ukv, MLA_NOPE + MLA_V, is_nope)
    wuv = w_ukv.reshape(MLA_KV_LORA, MLA_HEADS, MLA_NOPE + MLA_V)[:, :, MLA_NOPE:].reshape(
        MLA_KV_LORA, MLA_HEADS * MLA_V)
    rope_cols = w_in[:, MLA_Q_LORA + MLA_KV_LORA:]
    rope_placed = jnp.where(is_rope[None, :], rope_cols[:, np.where(is_rope, src - MLA_NOPE, 0)], 0.0)
    win = jnp.concatenate([w_in[:, :MLA_Q_LORA + MLA_KV_LORA], rope_placed], axis=1)
    gain = lambda g: jnp.where(valid, g.astype(F32)[src], 0.0).reshape(1, LANES)
    partner = (np.arange(LANES) + LANES // 2) % LANES
    gain_partner = lambda g: jnp.where(is_rope, gain(g)[0, partner], 0.0).reshape(1, LANES)

    half = MLA_ROPE // 2
    inv_freq = ROPE_THETA ** (-jnp.arange(half, dtype=F32) / half)
    ang = jnp.arange(t).astype(F32)[:, None] * inv_freq[None, :]
    rest = LANES // 2 - half
    one, zero = jnp.ones((t, rest), F32), jnp.zeros((t, rest), F32)
    cos = jnp.concatenate([jnp.cos(ang), one, jnp.cos(ang), one], axis=1)
    sin = jnp.concatenate([-jnp.sin(ang), zero, jnp.sin(ang), zero], axis=1)
    q_mult = MLA_QK ** -0.5 * LOG2E
    tables = (gain(q_norm) * cos * q_mult, gain_partner(q_norm) * sin * q_mult,
              gain(k_norm) * cos, gain_partner(k_norm) * sin)

    lane2 = np.arange(2 * LANES)
    rope2 = np.tile(is_rope, 2)
    swap = (lane2[:, None] == (lane2[None, :] // LANES) * LANES + np.tile(partner, 2)[None, :]) & rope2[None, :]
    bd = (lane2[:, None] // LANES) == (lane2[None, :] // LANES)
    return (win.astype(BF16), wuq.astype(BF16), wuk.astype(BF16), wuv.astype(BF16),
            jnp.asarray(swap, BF16), jnp.asarray(bd, BF16)) + tables


def _mla_mixer(x, mix_g, w_in, q_a_norm, w_uq, kv_a_norm, w_ukv, q_norm, k_norm, w_out):
    b, t, _ = x.shape
    m = b * t
    prepared = _mla_weights(w_in, w_uq, w_ukv, q_norm, k_norm, t)
    q, k, v = _mla_prep(x.reshape(m, D_MODEL), mix_g, prepared[0], q_a_norm.astype(F32).reshape(1, -1),
                        kv_a_norm.astype(F32).reshape(1, -1), *prepared[1:], t)
    o = _mla_attn(q.reshape(b, t, -1), k.reshape(b, t, -1), v.reshape(b, t, -1))
    return "proj", (o.reshape(m, D_MODEL), w_out)


def kernel(x, ffn1_norm, ffn1_w_gu, ffn1_w_down, mix_norm, ffn2_norm, ffn2_w_gu, ffn2_w_down,
           hg_lb_logits, hg_w_in, hg_g_norm, hg_w_out,
           na_w_in, na_q_norm, na_k_norm, na_rpb, na_w_out,
           mla_w_in, mla_q_a_norm, mla_w_uq, mla_kv_a_norm, mla_w_ukv, mla_q_norm, mla_k_norm, mla_w_out):
    b, t, d = x.shape
    m = b * t
    gam = jnp.cumsum(jax.nn.softmax(hg_lb_logits.astype(F32), axis=0), axis=0)
    lb_all = gam - gam[0:1]
    row = lambda g: g.astype(F32).reshape(1, -1)
    bf = lambda w: w.astype(BF16)
    ia = ib = ic = 0
    for layer in range(DEPTH):
        x = _ffn(x.reshape(m, d), row(ffn1_norm[layer]), bf(ffn1_w_gu[layer]),
                 bf(ffn1_w_down[layer])).reshape(b, t, d)
        g = row(mix_norm[layer])
        kind = layer % N_MIXERS
        if kind == 0:
            mixer = _hgrn_mixer(x, g, bf(hg_w_in[ia]), hg_g_norm[ia].astype(F32), bf(hg_w_out[ia]), lb_all[layer])
            ia += 1
        elif kind == 1:
            mixer = _na_mixer(x, g, bf(na_w_in[ib]), na_q_norm[ib], na_k_norm[ib], na_rpb[ib], bf(na_w_out[ib]))
            ib += 1
        else:
            mixer = _mla_mixer(x, g, mla_w_in[ic], mla_q_a_norm[ic], mla_w_uq[ic], mla_kv_a_norm[ic],
                               mla_w_ukv[ic], mla_q_norm[ic], mla_k_norm[ic], bf(mla_w_out[ic]))
            ic += 1
        x = _ffn(x.reshape(m, d), row(ffn2_norm[layer]), bf(ffn2_w_gu[layer]),
                 bf(ffn2_w_down[layer]), *mixer).reshape(b, t, d)
    return x
```

```python
import functools

import numpy as np
import jax
import jax.numpy as jnp
from jax import lax
from jax.experimental import pallas as pl
from jax.experimental.pallas import tpu as pltpu

D_MODEL = 1024
DEPTH = 4
N_MIXERS = 3
GRID_W = 64
D_FF = 2816
NORM_EPS = 1e-6
HG_HEADS = 8
HG_DK = 128
HG_DV = 128
NA_HEADS = 16
NA_HEAD_DIM = 64
NA_WIN_R = 8
NA_WIN_C = 16
MLA_HEADS = 16
MLA_Q_LORA = 768
MLA_KV_LORA = 256
MLA_NOPE = 64
MLA_ROPE = 32
MLA_V = 64
MLA_QK = MLA_NOPE + MLA_ROPE
ROPE_THETA = 10000.0
NEG_INF = -1e30
LOG2E = 1.4426950408889634

LANES = 128
VMEM_LIMIT = 56 * 1024 * 1024
TOKEN_TILE = 512
HG_CHUNK = 128
HG_LEVELS = (64, 32, 16, 8, 4, 2, 1)
HG_SUBLANES = 8
HG_HEADS_PER_STEP = 2
HG_UNROLL = 8
MLA_Q_TILE = 1024
MLA_Q_SUBTILE = 512
MLA_PREP_UNROLL = 2
HG_PROJ_ROW_SPLITS = 4
NA_GROUP = 4
NA_UNION = NA_WIN_R + NA_GROUP
NA_UNROLL = 4
F32 = jnp.float32
BF16 = jnp.bfloat16


def _params(*sem):
    return pltpu.CompilerParams(dimension_semantics=sem, vmem_limit_bytes=VMEM_LIMIT)


def _resident(shape):
    nd = len(shape)
    return pl.BlockSpec(shape, lambda *_: (0,) * nd, pipeline_mode=pl.Buffered(1))


def _rms(x, g):
    ms = jnp.mean(x * x, axis=-1, keepdims=True)
    return x * lax.rsqrt(ms + NORM_EPS) * g


def _silu(x):
    return x * (1.0 / (1.0 + jnp.exp(-x)))


def _dot(a, b):
    return jnp.dot(a, b, preferred_element_type=F32)


def _dot_nt(a, b):
    return lax.dot_general(a, b, (((1,), (1,)), ((), ())), preferred_element_type=F32)


def _interleave(gens):
    results = [None] * len(gens)
    pending = set(range(len(gens)))
    while pending:
        for i in sorted(pending):
            try:
                next(gens[i])
            except StopIteration as stop:
                results[i] = stop.value
                pending.discard(i)
    return results


def _dot_tn(a, b):
    return lax.dot_general(a, b, (((0,), (0,)), ((), ())), preferred_element_type=F32)


FF_CHUNKS = ((0, 1536), (1536, 2816))
FFN_ROW_SPLITS = 2


def _ffn_body(x, g, wgu_ref, wd_ref):
    h = _rms(x, g).astype(BF16)
    acc = None
    for s, e in FF_CHUNKS:
        gate = _dot(h, wgu_ref[:, s:e])
        up = _dot(h, wgu_ref[:, D_FF + s:D_FF + e])
        yield
        a = (_silu(gate) * up).astype(BF16)
        d = _dot(a, wd_ref[s:e, :])
        acc = d if acc is None else acc + d
        yield
    return x + 0.5 * acc


def _hgrn_gated(o_ref, gate_ref, gn, rs):
    parts = []
    for h in range(HG_HEADS):
        sl = slice(h * HG_DV, (h + 1) * HG_DV)
        parts.append(_rms(o_ref[rs, sl], gn) * _silu(gate_ref[rs, sl]))
    return jnp.concatenate(parts, axis=-1).astype(BF16)


def _ffn_kernel(mixer, *refs):
    *mixer_refs, x_ref, g_ref, wgu_ref, wd_ref, o_ref = refs
    rows = x_ref.shape[0] // FFN_ROW_SPLITS

    def part(r0):
        rs = slice(r0, r0 + rows)
        x = x_ref[rs, :]
        if mixer == "proj":
            a_ref, w_ref = mixer_refs
            x = x + _dot(a_ref[rs, :], w_ref[...])
        elif mixer == "hgrn":
            s_ref, gate_ref, gn_ref, w_ref = mixer_refs
            x = x + _dot(_hgrn_gated(s_ref, gate_ref, gn_ref[...], rs), w_ref[...])
        yield
        o_ref[rs, :] = yield from _ffn_body(x, g_ref[...], wgu_ref, wd_ref)

    _interleave([part(r0) for r0 in range(0, x_ref.shape[0], rows)])


def _ffn(x, g, wgu, wd, mixer=None, mixer_args=()):
    m = x.shape[0]
    tm = min(TOKEN_TILE, m)
    tile = lambda a: pl.BlockSpec((tm, a.shape[1]), lambda i: (i, 0))
    n_tiled = {None: 0, "proj": 1, "hgrn": 2}[mixer]
    mixer_specs = [tile(a) if i < n_tiled else _resident(a.shape) for i, a in enumerate(mixer_args)]
    return pl.pallas_call(
        functools.partial(_ffn_kernel, mixer),
        grid=(m // tm,),
        in_specs=mixer_specs + [tile(x), _resident((1, D_MODEL)), _resident(wgu.shape), _resident(wd.shape)],
        out_specs=tile(x),
        out_shape=jax.ShapeDtypeStruct((m, D_MODEL), F32),
        compiler_params=_params("parallel"),
        name="ffn" if mixer is None else "ffn_" + mixer,
    )(*mixer_args, x, g, wgu, wd)


def _hgrn_q_row(direction, hb, t):
    return ((t % (2 * hb)) >= hb) != (direction == 1)


def _hgrn_ref_row(direction, hb, t):
    return (t // (2 * hb)) * (2 * hb) + (hb - 1 if direction == 0 else hb)


def _hgrn_tables():
    c = HG_CHUNK
    t = np.arange(c)[:, None]
    s = np.arange(c)[None, :]
    tri = np.stack([s <= t, s >= t]).astype(np.float32)
    wide, full = [], []
    for d in range(2):
        wide_d, full_d = [], []
        for hb in HG_LEVELS:
            same = (t // (2 * hb)) == (s // (2 * hb))
            own = (_hgrn_q_row(d, hb, t) & same & ~_hgrn_q_row(d, hb, s)).astype(np.float32)
            if hb >= HG_SUBLANES:
                wide_d.append(own[_hgrn_q_row(d, hb, np.arange(c))])
            else:
                full_d.append(own)
        wide.append(np.stack(wide_d))
        full.append(np.stack(full_d))
    return tri, np.stack(wide), np.stack(full)


def _hgrn_proj_kernel(x_ref, g_ref, w_ref, llb_ref, l1m_ref, q_ref, lf_ref, lb_ref, v_ref, gate_ref):
    h = _rms(x_ref[...], g_ref[...]).astype(BF16)
    llb = llb_ref[...]
    l1m = l1m_ref[...]

    def log2_decay(z):
        ls = jnp.minimum(z, 0.0) - jnp.log(1.0 + jnp.exp2(jnp.abs(z) * -LOG2E))
        y = l1m + ls
        return (jnp.maximum(llb, y) + jnp.log(1.0 + jnp.exp2(jnp.abs(llb - y) * -LOG2E))) * LOG2E

    finishes = ((q_ref, lambda y: (y * HG_DK ** -0.5).astype(BF16)), (lf_ref, log2_decay), (lb_ref, log2_decay),
                (v_ref, lambda y: y.astype(BF16)), (gate_ref, lambda y: y))
    rows = h.shape[0] // HG_PROJ_ROW_SPLITS

    def part(r0):
        hs = h[r0:r0 + rows]
        for j, (o_ref, finish) in enumerate(finishes):
            y = _dot(hs, w_ref[:, j * D_MODEL:(j + 1) * D_MODEL])
            yield
            o_ref[r0:r0 + rows, :] = finish(y)

    _interleave([part(r0) for r0 in range(0, h.shape[0], rows)])


def _hgrn_proj(x, g, w, llb, l1m):
    m = x.shape[0]
    tm = min(TOKEN_TILE, m)
    tile = lambda: pl.BlockSpec((tm, D_MODEL), lambda i: (i, 0))
    return pl.pallas_call(
        _hgrn_proj_kernel,
        grid=(m // tm,),
        in_specs=[tile(), _resident((1, D_MODEL)), _resident(w.shape),
                  _resident((1, D_MODEL)), _resident((1, D_MODEL))],
        out_specs=[tile()] * 5,
        out_shape=[jax.ShapeDtypeStruct((m, D_MODEL), dt) for dt in (BF16, F32, F32, BF16, F32)],
        compiler_params=_params("parallel"),
        name="hgrn_proj",
    )(x, g, w, llb, l1m)


def _hgrn_chunk(direction, q32, v, logf2, tri_ref, wide_ref, full_ref, cum_ref, st_ref):
    c = HG_CHUNK
    nv = c // HG_SUBLANES
    rows = lambda a, i: a[i * HG_SUBLANES:(i + 1) * HG_SUBLANES]
    f = jnp.exp2(logf2)
    k = 1.0 - f
    hi = logf2.astype(BF16)
    lo = (logf2 - hi.astype(F32)).astype(BF16)
    tri = tri_ref[direction]
    cum = _dot(tri, hi) + _dot(tri, lo)
    heads = HG_HEADS_PER_STEP
    hsl = [slice(h * HG_DK, (h + 1) * HG_DK) for h in range(heads)]
    for h in range(heads):
        cum_ref[direction, h] = cum[:, hsl[h]]
    yield
    ref_row = lambda r: jnp.concatenate([cum_ref[direction, h, r:r + 1, :] for h in range(heads)], axis=-1)
    ref_rows = lambda r: jnp.broadcast_to(ref_row(r), (HG_SUBLANES, q32.shape[1]))
    att = [[None] * nv for _ in range(heads)]

    def add_rows(h, i, piece):
        att[h][i] = piece if att[h][i] is None else att[h][i] + piece

    sub = lax.broadcasted_iota(jnp.int32, (c, q32.shape[1]), 0)
    sub8 = lax.broadcasted_iota(jnp.int32, (HG_SUBLANES, q32.shape[1]), 0)
    n_wide = 0
    for li, hb in enumerate(HG_LEVELS):
        if hb >= HG_SUBLANES:
            parts, q_idx = [], []
            for i in range(nv):
                t0 = i * HG_SUBLANES
                ref = ref_rows(int(_hgrn_ref_row(direction, hb, t0)))
                if bool(_hgrn_q_row(direction, hb, t0)):
                    parts.append(rows(q32, i) * jnp.exp2(rows(cum, i) - ref))
                    q_idx.append(i)
                else:
                    parts.append(rows(k, i) * jnp.exp2(ref - rows(cum, i)))
            w = jnp.concatenate(parts, axis=0).astype(BF16)
            wq = jnp.concatenate([parts[i] for i in q_idx], axis=0).astype(BF16)
            for h in range(heads):
                a = _dot_nt(wq[:, hsl[h]], w[:, hsl[h]]) * wide_ref[direction, n_wide]
                for j, i in enumerate(q_idx):
                    add_rows(h, i, rows(a, j))
            n_wide += 1
        else:
            q_rows = _hgrn_q_row(direction, hb, sub)
            if hb == 1:
                w = jnp.where(q_rows, q32 * f, k)
            else:
                pieces = []
                for i in range(nv):
                    t0 = i * HG_SUBLANES
                    if 2 * hb == HG_SUBLANES:
                        pieces.append(ref_rows(int(_hgrn_ref_row(direction, hb, t0))))
                    else:
                        lo_ref = ref_rows(int(_hgrn_ref_row(direction, hb, t0)))
                        hi_ref = ref_rows(int(_hgrn_ref_row(direction, hb, t0 + 2 * hb)))
                        pieces.append(jnp.where(sub8 < 2 * hb, lo_ref, hi_ref))
                d = cum - jnp.concatenate(pieces, axis=0)
                w = jnp.where(q_rows, q32, k) * jnp.exp2(jnp.where(q_rows, d, -d))
            w = w.astype(BF16)
            for h in range(heads):
                a = _dot_nt(w[:, hsl[h]], w[:, hsl[h]]) * full_ref[direction, li - n_wide]
                for i in range(nv):
                    add_rows(h, i, rows(a, i))
        yield

    last = c - 1 if direction == 0 else 0
    total = ref_row(last)
    qi = (q32 * jnp.exp2(cum)).astype(BF16)
    ki = (k * jnp.exp2(total - cum)).astype(BF16)
    qk = q32 * k
    decay = jnp.exp2(total)
    outs = []
    for h in range(heads):
        vh = v[:, hsl[h]]
        st = st_ref[direction, h]
        o = _dot(jnp.concatenate(att[h], axis=0).astype(BF16), vh)
        o = o + jnp.sum(qk[:, hsl[h]], axis=-1, keepdims=True) * vh.astype(F32)
        o = o + _dot_nt(qi[:, hsl[h]], st.astype(BF16))
        st_ref[direction, h] = st * decay[:, hsl[h]] + _dot_tn(vh, ki[:, hsl[h]])
        outs.append(o)
    return jnp.concatenate(outs, axis=-1)


def _hgrn_kernel(q_ref, lf_ref, lb_ref, v_ref, tri_ref, wide_ref, full_ref,
                 o_ref, ob_ref, cum_ref, st_ref):
    t = q_ref.shape[1]
    c = HG_CHUNK
    n = t // c
    st_ref[...] = jnp.zeros_like(st_ref)

    def chunk(direction, gate_ref, r0, cum_slot):
        return _hgrn_chunk(direction, q_ref[0, pl.ds(r0, c), :].astype(F32), v_ref[0, pl.ds(r0, c), :],
                           gate_ref[0, pl.ds(r0, c), :], tri_ref, wide_ref, full_ref, cum_slot, st_ref)

    def body(i, carry):
        gens, dests = [], []
        for u in range(HG_UNROLL):
            rf = pl.multiple_of((i * HG_UNROLL + u) * c, c)
            rb = pl.multiple_of((n - 1 - i * HG_UNROLL - u) * c, c)
            gens += [chunk(0, lf_ref, rf, cum_ref.at[u]), chunk(1, lb_ref, rb, cum_ref.at[u])]
            dests += [(o_ref.at[0], rf), (ob_ref, rb)]
        for (dst, r0), o in zip(dests, _interleave(gens)):
            dst[pl.ds(r0, c), :] = o
        return carry

    lax.fori_loop(0, n // HG_UNROLL, body, 0)
    o_ref[0] = o_ref[0] + ob_ref[...]


def _hgrn_scan(q, lf, lb, v):
    b, t, _ = q.shape
    assert t % (HG_CHUNK * HG_UNROLL) == 0
    tri, wide, full = _hgrn_tables()
    tri = jnp.asarray(tri, BF16)
    wide = jnp.asarray(wide, F32)
    full = jnp.asarray(full, F32)
    width = HG_HEADS_PER_STEP * HG_DK
    seq = lambda: pl.BlockSpec((1, t, width), lambda i, j: (i, 0, j))
    return pl.pallas_call(
        _hgrn_kernel,
        grid=(b, HG_HEADS // HG_HEADS_PER_STEP),
        in_specs=[seq(), seq(), seq(), seq(),
                  _resident(tri.shape), _resident(wide.shape), _resident(full.shape)],
        out_specs=seq(),
        out_shape=jax.ShapeDtypeStruct((b, t, D_MODEL), F32),
        scratch_shapes=[pltpu.VMEM((t, width), F32),
                        pltpu.VMEM((HG_UNROLL, 2, HG_HEADS_PER_STEP, HG_CHUNK, HG_DK), F32),
                        pltpu.VMEM((2, HG_HEADS_PER_STEP, HG_DV, HG_DK), F32)],
        compiler_params=_params("parallel", "parallel"),
        name="hgrn_scan",
    )(q, lf, lb, v, tri, wide, full)


def _hgrn_mixer(x, mix_g, w_in, g_norm, w_out, lb):
    b, t, _ = x.shape
    m = b * t
    lb = lb.astype(F32).reshape(1, D_MODEL)
    q, lf, lbw, v, gate = _hgrn_proj(x.reshape(m, D_MODEL), mix_g, w_in, jnp.log(lb), jnp.log1p(-lb))
    r3 = lambda a: a.reshape(b, t, D_MODEL)
    o = _hgrn_scan(r3(q), r3(lf), r3(lbw), r3(v))
    return "hgrn", (o.reshape(m, D_MODEL), gate, g_norm.reshape(1, HG_DV), w_out)


def _na_proj_kernel(x_ref, g_ref, w_ref, qg_ref, kg_ref, bd_ref, q_ref, k_ref, v_ref):
    h = _rms(x_ref[...], g_ref[...]).astype(BF16)
    bd = bd_ref[...]

    def head_norm(y, gain, mult):
        width = bd.shape[0]
        parts = []
        for j in range(D_MODEL // width):
            ys = y[:, j * width:(j + 1) * width]
            ss = _dot((ys * ys).astype(BF16), bd)
            parts.append(ys * lax.rsqrt(ss * (1.0 / NA_HEAD_DIM) + NORM_EPS) * (gain * mult))
        return jnp.concatenate(parts, axis=-1)

    q = _dot(h, w_ref[:, 0:D_MODEL])
    q_ref[...] = head_norm(q, qg_ref[...], NA_HEAD_DIM ** -0.5 * LOG2E).astype(BF16)
    k = _dot(h, w_ref[:, D_MODEL:2 * D_MODEL])
    k_ref[...] = head_norm(k, kg_ref[...], 1.0).astype(BF16)
    v_ref[...] = _dot(h, w_ref[:, 2 * D_MODEL:3 * D_MODEL]).astype(BF16)


def _na_proj(x, g, w, qg, kg):
    m = x.shape[0]
    tm = min(TOKEN_TILE, m)
    blk = np.arange(2 * LANES) // NA_HEAD_DIM
    bd = jnp.asarray(blk[:, None] == blk[None, :], BF16)
    tile = lambda: pl.BlockSpec((tm, D_MODEL), lambda i: (i, 0))
    return pl.pallas_call(
        _na_proj_kernel,
        grid=(m // tm,),
        in_specs=[tile(), _resident((1, D_MODEL)), _resident(w.shape),
                  _resident(qg.shape), _resident(kg.shape), _resident(bd.shape)],
        out_specs=[tile(), tile(), tile()],
        out_shape=[jax.ShapeDtypeStruct((m, D_MODEL), BF16)] * 3,
        compiler_params=_params("parallel"),
        name="na_proj",
    )(x, g, w, qg, kg, bd)


def _na_group_start(g, rows):
    return jnp.clip(g * NA_GROUP - NA_WIN_R // 2, 0, rows - NA_UNION) if isinstance(g, jax.Array) else \
        int(np.clip(g * NA_GROUP - NA_WIN_R // 2, 0, rows - NA_UNION))


def _na_attn_kernel(rows, q_ref, k_ref, v_ref, bias_ref, cm_ref, o_ref):
    gq = NA_GROUP * GRID_W
    uk = NA_UNION * GRID_W
    n_groups = rows // NA_GROUP
    lane = lax.broadcasted_iota(jnp.int32, (gq, LANES), 1)
    first = lane < NA_HEAD_DIM
    head_sel = (jnp.where(first[0:1], 1.0, 0.0).astype(BF16), jnp.where(first[0:1], 0.0, 1.0).astype(BF16))

    def group(g):
        kind = jnp.where(g == 0, 0, jnp.where(g == n_groups - 1, 2, 1))
        qs = pl.multiple_of(g * gq, gq)
        ks = pl.multiple_of(_na_group_start(g, rows) * GRID_W, NA_GROUP * GRID_W)
        q = q_ref[0, pl.ds(qs, gq), :]
        kb = k_ref[0, pl.ds(ks, uk), :]
        vb = v_ref[0, pl.ds(ks, uk), :]
        s = _dot_nt(jnp.concatenate([q * head_sel[0], q * head_sel[1]], axis=0), kb)
        yield
        cm = cm_ref[kind]
        probs = []
        for hh in range(2):
            sh = s[hh * gq:(hh + 1) * gq] * cm + bias_ref[hh, kind]
            probs.append(jnp.exp2((sh - jnp.max(sh, axis=-1, keepdims=True)).astype(BF16)))
            yield
        v_ext = jnp.concatenate([vb, jnp.ones_like(vb)], axis=-1)
        o2 = _dot(jnp.concatenate(probs, axis=0), v_ext)
        o2 = o2[:, :LANES] * (1.0 / o2[:, LANES:])
        o_ref[0, pl.ds(qs, gq), :] = jnp.where(first, o2[:gq], o2[gq:]).astype(o_ref.dtype)

    def body(i, carry):
        _interleave([group(i * NA_UNROLL + u) for u in range(NA_UNROLL)])
        return carry

    lax.fori_loop(0, n_groups // NA_UNROLL, body, 0)


def _na_attn(q, k, v, bias, cm):
    b, t, _ = q.shape
    rows = t // GRID_W
    seq = lambda: pl.BlockSpec((1, t, LANES), lambda j, i: (i, 0, j))
    return pl.pallas_call(
        functools.partial(_na_attn_kernel, rows),
        grid=(D_MODEL // LANES, b),
        in_specs=[seq(), seq(), seq(),
                  pl.BlockSpec((2,) + bias.shape[1:], lambda j, i: (j, 0, 0, 0)),
                  _resident(cm.shape)],
        out_specs=seq(),
        out_shape=jax.ShapeDtypeStruct((b, t, D_MODEL), BF16),
        compiler_params=_params("parallel", "parallel"),
        name="na_attn",
    )(q, k, v, bias, cm)


def _na_tables(rpb, rows):
    assert rows % NA_GROUP == 0 and rows >= NA_UNION + NA_GROUP
    cols = np.arange(GRID_W)
    col_start = np.clip(cols - NA_WIN_C // 2, 0, GRID_W - NA_WIN_C)
    col_mask = (cols[None, :] >= col_start[:, None]) & (cols[None, :] < col_start[:, None] + NA_WIN_C)
    col_off = np.clip(cols[None, :] - cols[:, None] + NA_WIN_C - 1, 0, 2 * NA_WIN_C - 2)
    n_groups = rows // NA_GROUP
    by_col = rpb.astype(F32)[:, :, col_off].transpose(0, 2, 1, 3)
    pad = NA_UNION
    by_col = jnp.pad(by_col, ((0, 0), (0, 0), (pad, pad), (0, 0)))
    by_col = by_col.reshape(NA_HEADS, GRID_W, -1) * LOG2E
    valid, kinds = [], []
    for g in (0, 1, n_groups - 1):
        start = _na_group_start(g, rows)
        key = start + np.arange(NA_UNION)[None, :]
        r = g * NA_GROUP + np.arange(NA_GROUP)[:, None]
        r0 = np.clip(r - NA_WIN_R // 2, 0, rows - NA_WIN_R)
        valid.append((key >= r0) & (key < r0 + NA_WIN_R))
        per_row = []
        for a in range(NA_GROUP):
            off = start - (g * NA_GROUP + a) + NA_WIN_R - 1 + pad
            per_row.append(by_col[:, :, off * GRID_W:(off + NA_UNION) * GRID_W])
        kinds.append(jnp.stack(per_row, axis=1))
    valid = np.stack(valid)
    assert (valid.sum(-1) == NA_WIN_R).all()
    full_valid = valid[:, :, None, :, None] & col_mask[None, None, :, None, :]
    shape = (3, NA_GROUP * GRID_W, NA_UNION * GRID_W)
    full_valid = full_valid.reshape(shape)
    tab = jnp.stack(kinds, axis=1).reshape((NA_HEADS,) + shape)
    tab = jnp.where(full_valid[None], tab, NEG_INF * LOG2E)
    return tab, jnp.asarray(full_valid, F32)


def _na_mixer(x, mix_g, w_in, q_norm, k_norm, rpb, w_out):
    b, t, _ = x.shape
    m = b * t
    tile2 = lambda g: jnp.tile(g.astype(F32), 2 * LANES // NA_HEAD_DIM).reshape(1, 2 * LANES)
    q, k, v = _na_proj(x.reshape(m, D_MODEL), mix_g, w_in, tile2(q_norm), tile2(k_norm))
    bias, cm = _na_tables(rpb, t // GRID_W)
    r3 = lambda a: a.reshape(b, t, D_MODEL)
    o = _na_attn(r3(q), r3(k), r3(v), bias, cm)
    return "proj", (o.reshape(m, D_MODEL), w_out)


_MLA_PERM = np.concatenate([
    MLA_NOPE + np.arange(16),
    np.arange(48),
    MLA_NOPE + 16 + np.arange(16),
    48 + np.arange(16),
    -np.ones(32, np.int64),
]).astype(np.int64)


def _mla_prep_kernel(x_ref, g_ref, win_ref, qa_ref, kva_ref, wuq_ref, wuk_ref, wuv_ref,
                     bd_ref, aq_ref, bq_ref, ak_ref, bk_ref, q_ref, k_ref, v_ref):
    h = _rms(x_ref[...], g_ref[...]).astype(BF16)
    c = _dot(h, win_ref[...])
    cq = _rms(c[:, :MLA_Q_LORA], qa_ref[...]).astype(BF16)
    ckv = _rms(c[:, MLA_Q_LORA:MLA_Q_LORA + MLA_KV_LORA], kva_ref[...]).astype(BF16)
    k_rope = c[:, MLA_Q_LORA + MLA_KV_LORA:]
    two = lambda a: jnp.concatenate([a, a], axis=-1)
    aq, bq, ak = two(aq_ref[...]), two(bq_ref[...]), two(ak_ref[...])
    k_rope2 = two(k_rope)
    k_partner = two(pltpu.roll(k_rope, LANES // 2, 1) * bk_ref[...])
    bd = bd_ref[...]

    def inv_rms(y):
        ss = _dot((y * y).astype(BF16), bd)
        return lax.rsqrt(ss * (1.0 / MLA_QK) + NORM_EPS)

    def pair(p):
        sl = slice(p * 2 * LANES, (p + 1) * 2 * LANES)
        yq = _dot(cq, wuq_ref[:, sl])
        yk = _dot(ckv, wuk_ref[:, sl]) + k_rope2
        yield
        yq_partner = jnp.concatenate([pltpu.roll(yq[:, :LANES], LANES // 2, 1),
                                      pltpu.roll(yq[:, LANES:], LANES // 2, 1)], axis=-1)
        rq = inv_rms(yq)
        rk = inv_rms(yk)
        yield
        q_ref[:, sl] = ((yq * aq + yq_partner * bq) * rq).astype(BF16)
        k_ref[:, sl] = ((yk * ak + k_partner) * rk).astype(BF16)

    for p0 in range(0, MLA_HEADS // 2, MLA_PREP_UNROLL):
        _interleave([pair(p) for p in range(p0, p0 + MLA_PREP_UNROLL)])
    v_ref[...] = _dot(ckv, wuv_ref[...]).astype(BF16)


def _mla_prep(x, g, win, qa, kva, wuq, wuk, wuv, bd, aq, bq, ak, bk, t):
    m = x.shape[0]
    tm = min(TOKEN_TILE, t)
    per_seq = t // tm
    wide = MLA_HEADS * LANES
    table = lambda: pl.BlockSpec((tm, LANES), lambda i: (i % per_seq, 0))
    return pl.pallas_call(
        _mla_prep_kernel,
        grid=(m // tm,),
        in_specs=[pl.BlockSpec((tm, D_MODEL), lambda i: (i, 0)),
                  _resident((1, D_MODEL)), _resident(win.shape),
                  _resident((1, MLA_Q_LORA)), _resident((1, MLA_KV_LORA)),
                  _resident(wuq.shape), _resident(wuk.shape), _resident(wuv.shape),
                  _resident(bd.shape),
                  table(), table(), table(), table()],
        out_specs=[pl.BlockSpec((tm, wide), lambda i: (i, 0)),
                   pl.BlockSpec((tm, wide), lambda i: (i, 0)),
                   pl.BlockSpec((tm, D_MODEL), lambda i: (i, 0))],
        out_shape=[jax.ShapeDtypeStruct((m, wide), BF16),
                   jax.ShapeDtypeStruct((m, wide), BF16),
                   jax.ShapeDtypeStruct((m, D_MODEL), BF16)],
        compiler_params=_params("parallel"),
        name="mla_prep",
    )(x, g, win, qa, kva, wuq, wuk, wuv, bd, aq, bq, ak, bk)


def _mla_attn_kernel(q_ref, k_ref, v_ref, o_ref):
    sub = MLA_Q_SUBTILE
    first = lax.broadcasted_iota(jnp.int32, (sub, LANES), 1) < MLA_V
    v_ext = jnp.concatenate([v_ref[0], jnp.ones((v_ref.shape[1], LANES), BF16)], axis=-1)

    def head(r0, hh):
        sl = slice(hh * LANES, (hh + 1) * LANES)
        s = _dot_nt(q_ref[0, r0:r0 + sub, sl], k_ref[0, :, sl])
        yield
        p = jnp.exp2((s - jnp.max(s, axis=-1, keepdims=True)).astype(BF16))
        yield
        o = _dot(p, v_ext)
        return o[:, :LANES] * (1.0 / o[:, LANES:])

    starts = range(0, q_ref.shape[1], sub)
    outs = _interleave([head(r0, hh) for r0 in starts for hh in range(2)])
    for n, r0 in enumerate(starts):
        o_ref[0, r0:r0 + sub, :] = jnp.where(first, outs[2 * n], outs[2 * n + 1]).astype(o_ref.dtype)


def _mla_attn(q, k, v):
    b, t, _ = q.shape
    tq = min(MLA_Q_TILE, t)
    return pl.pallas_call(
        _mla_attn_kernel,
        grid=(b, MLA_HEADS // 2, t // tq),
        in_specs=[pl.BlockSpec((1, tq, 2 * LANES), lambda i, j, l: (i, l, j)),
                  pl.BlockSpec((1, t, 2 * LANES), lambda i, j, l: (i, 0, j)),
                  pl.BlockSpec((1, t, LANES), lambda i, j, l: (i, 0, j))],
        out_specs=pl.BlockSpec((1, tq, LANES), lambda i, j, l: (i, l, j)),
        out_shape=jax.ShapeDtypeStruct((b, t, D_MODEL), BF16),
        compiler_params=_params("parallel", "parallel", "arbitrary"),
        name="mla_attn",
    )(q, k, v)


def _mla_weights(w_in, w_uq, w_ukv, q_norm, k_norm, t):
    valid = _MLA_PERM >= 0
    src = np.where(valid, _MLA_PERM, 0)
    is_rope = valid & (_MLA_PERM >= MLA_NOPE)
    is_nope = valid & (_MLA_PERM < MLA_NOPE)

    def place(w, per_head, lanes_ok, base=0):
        wh = w.reshape(w.shape[0], MLA_HEADS, per_head)[:, :, base + np.where(lanes_ok, src, 0)]
        return jnp.where(lanes_ok[None, None, :], wh, 0.0).reshape(w.shape[0], MLA_HEADS * LANES)

    wuq = place(w_uq, MLA_QK, valid)
    wuk = place(w_ukv, MLA_NOPE + MLA_V, is_nope)
    wuv = w_ukv.reshape(MLA_KV_LORA, MLA_HEADS, MLA_NOPE + MLA_V)[:, :, MLA_NOPE:].reshape(
        MLA_KV_LORA, MLA_HEADS * MLA_V)
    rope_cols = w_in[:, MLA_Q_LORA + MLA_KV_LORA:]
    rope_placed = jnp.where(is_rope[None, :], rope_cols[:, np.where(is_rope, src - MLA_NOPE, 0)], 0.0)
    win = jnp.concatenate([w_in[:, :MLA_Q_LORA + MLA_KV_LORA], rope_placed], axis=1)
    gain = lambda g: jnp.where(valid, g.astype(F32)[src], 0.0).reshape(1, LANES)
    partner = (np.arange(LANES) + LANES // 2) % LANES
    gain_partner = lambda g: jnp.where(is_rope, gain(g)[0, partner], 0.0).reshape(1, LANES)

    half = MLA_ROPE // 2
    inv_freq = ROPE_THETA ** (-jnp.arange(half, dtype=F32) / half)
    ang = jnp.arange(t).astype(F32)[:, None] * inv_freq[None, :]
    rest = LANES // 2 - half
    one, zero = jnp.ones((t, rest), F32), jnp.zeros((t, rest), F32)
    cos = jnp.concatenate([jnp.cos(ang), one, jnp.cos(ang), one], axis=1)
    sin = jnp.concatenate([-jnp.sin(ang), zero, jnp.sin(ang), zero], axis=1)
    q_mult = MLA_QK ** -0.5 * LOG2E
    tables = (gain(q_norm) * cos * q_mult, gain_partner(q_norm) * sin * q_mult,
              gain(k_norm) * cos, gain_partner(k_norm) * sin)

    lane2 = np.arange(2 * LANES)
    bd = (lane2[:, None] // LANES) == (lane2[None, :] // LANES)
    return (win.astype(BF16), wuq.astype(BF16), wuk.astype(BF16), wuv.astype(BF16), jnp.asarray(bd, BF16)) + tables


def _mla_mixer(x, mix_g, w_in, q_a_norm, w_uq, kv_a_norm, w_ukv, q_norm, k_norm, w_out):
    b, t, _ = x.shape
    m = b * t
    prepared = _mla_weights(w_in, w_uq, w_ukv, q_norm, k_norm, t)
    q, k, v = _mla_prep(x.reshape(m, D_MODEL), mix_g, prepared[0], q_a_norm.astype(F32).reshape(1, -1),
                        kv_a_norm.astype(F32).reshape(1, -1), *prepared[1:], t)
    o = _mla_attn(q.reshape(b, t, -1), k.reshape(b, t, -1), v.reshape(b, t, -1))
    return "proj", (o.reshape(m, D_MODEL), w_out)


def kernel(x, ffn1_norm, ffn1_w_gu, ffn1_w_down, mix_norm, ffn2_norm, ffn2_w_gu, ffn2_w_down,
           hg_lb_logits, hg_w_in, hg_g_norm, hg_w_out,
           na_w_in, na_q_norm, na_k_norm, na_rpb, na_w_out,
           mla_w_in, mla_q_a_norm, mla_w_uq, mla_kv_a_norm, mla_w_ukv, mla_q_norm, mla_k_norm, mla_w_out):
    b, t, d = x.shape
    m = b * t
    gam = jnp.cumsum(jax.nn.softmax(hg_lb_logits.astype(F32), axis=0), axis=0)
    lb_all = gam - gam[0:1]
    row = lambda g: g.astype(F32).reshape(1, -1)
    bf = lambda w: w.astype(BF16)
    ia = ib = ic = 0
    for layer in range(DEPTH):
        x = _ffn(x.reshape(m, d), row(ffn1_norm[layer]), bf(ffn1_w_gu[layer]),
                 bf(ffn1_w_down[layer])).reshape(b, t, d)
        g = row(mix_norm[layer])
        kind = layer % N_MIXERS
        if kind == 0:
            mixer = _hgrn_mixer(x, g, bf(hg_w_in[ia]), hg_g_norm[ia].astype(F32), bf(hg_w_out[ia]), lb_all[layer])
            ia += 1
        elif kind == 1:
            mixer = _na_mixer(x, g, bf(na_w_in[ib]), na_q_norm[ib], na_k_norm[ib], na_rpb[ib], bf(na_w_out[ib]))
            ib += 1
        else:
            mixer = _mla_mixer(x, g, mla_w_in[ic], mla_q_a_norm[ic], mla_w_uq[ic], mla_kv_a_norm[ic],
                               mla_w_ukv[ic], mla_q_norm[ic], mla_k_norm[ic], bf(mla_w_out[ic]))
            ic += 1
        x = _ffn(x.reshape(m, d), row(ffn2_norm[layer]), bf(ffn2_w_gu[layer]),
                 bf(ffn2_w_down[layer]), *mixer).reshape(b, t, d)
    return x
```

```python
import functools

import numpy as np
import jax
import jax.numpy as jnp
from jax import lax
from jax.experimental import pallas as pl
from jax.experimental.pallas import tpu as pltpu

D_MODEL = 1024
DEPTH = 4
N_MIXERS = 3
GRID_W = 64
D_FF = 2816
NORM_EPS = 1e-6
HG_HEADS = 8
HG_DK = 128
HG_DV = 128
NA_HEADS = 16
NA_HEAD_DIM = 64
NA_WIN_R = 8
NA_WIN_C = 16
MLA_HEADS = 16
MLA_Q_LORA = 768
MLA_KV_LORA = 256
MLA_NOPE = 64
MLA_ROPE = 32
MLA_V = 64
MLA_QK = MLA_NOPE + MLA_ROPE
ROPE_THETA = 10000.0
NEG_INF = -1e30
LOG2E = 1.4426950408889634

LANES = 128
VMEM_LIMIT = 56 * 1024 * 1024
TOKEN_TILE = 512
HG_CHUNK = 128
HG_LEVELS = (64, 32, 16, 8, 4, 2, 1)
HG_SUBLANES = 8
HG_HEADS_PER_STEP = 2
HG_UNROLL = 8
MLA_Q_TILE = 1024
MLA_Q_SUBTILE = 1024
MLA_PREP_UNROLL = 2
HG_PROJ_ROW_SPLITS = 4
NA_GROUP = 4
NA_UNION = NA_WIN_R + NA_GROUP
NA_UNROLL = 8
F32 = jnp.float32
BF16 = jnp.bfloat16


def _params(*sem):
    return pltpu.CompilerParams(dimension_semantics=sem, vmem_limit_bytes=VMEM_LIMIT)


def _resident(shape):
    nd = len(shape)
    return pl.BlockSpec(shape, lambda *_: (0,) * nd, pipeline_mode=pl.Buffered(1))


def _rms(x, g):
    ms = jnp.mean(x * x, axis=-1, keepdims=True)
    return x * lax.rsqrt(ms + NORM_EPS) * g


def _silu(x):
    return x * (1.0 / (1.0 + jnp.exp(-x)))


def _dot(a, b):
    return jnp.dot(a, b, preferred_element_type=F32)


def _dot_nt(a, b):
    return lax.dot_general(a, b, (((1,), (1,)), ((), ())), preferred_element_type=F32)


def _interleave(gens):
    results = [None] * len(gens)
    pending = set(range(len(gens)))
    while pending:
        for i in sorted(pending):
            try:
                next(gens[i])
            except StopIteration as stop:
                results[i] = stop.value
                pending.discard(i)
    return results


def _dot_tn(a, b):
    return lax.dot_general(a, b, (((0,), (0,)), ((), ())), preferred_element_type=F32)


FF_CHUNKS = ((0, 1536), (1536, 2816))
FFN_ROW_SPLITS = 4


def _ffn_body(x, g, wgu_ref, wd_ref):
    h = _rms(x, g).astype(BF16)
    acc = None
    for s, e in FF_CHUNKS:
        gate = _dot(h, wgu_ref[:, s:e])
        up = _dot(h, wgu_ref[:, D_FF + s:D_FF + e])
        yield
        a = (_silu(gate) * up).astype(BF16)
        d = _dot(a, wd_ref[s:e, :])
        acc = d if acc is None else acc + d
        yield
    return x + 0.5 * acc


def _hgrn_gated(o_ref, gate_ref, gn, rs):
    parts = []
    for h in range(HG_HEADS):
        sl = slice(h * HG_DV, (h + 1) * HG_DV)
        parts.append(_rms(o_ref[rs, sl], gn) * _silu(gate_ref[rs, sl]))
    return jnp.concatenate(parts, axis=-1).astype(BF16)


def _ffn_kernel(mixer, *refs):
    *mixer_refs, x_ref, g_ref, wgu_ref, wd_ref, o_ref = refs
    rows = x_ref.shape[0] // FFN_ROW_SPLITS

    def part(r0):
        rs = slice(r0, r0 + rows)
        x = x_ref[rs, :]
        if mixer == "proj":
            a_ref, w_ref = mixer_refs
            x = x + _dot(a_ref[rs, :], w_ref[...])
        elif mixer == "hgrn":
            s_ref, gate_ref, gn_ref, w_ref = mixer_refs
            x = x + _dot(_hgrn_gated(s_ref, gate_ref, gn_ref[...], rs), w_ref[...])
        yield
        o_ref[rs, :] = yield from _ffn_body(x, g_ref[...], wgu_ref, wd_ref)

    _interleave([part(r0) for r0 in range(0, x_ref.shape[0], rows)])


def _ffn(x, g, wgu, wd, mixer=None, mixer_args=()):
    m = x.shape[0]
    tm = min(TOKEN_TILE, m)
    tile = lambda a: pl.BlockSpec((tm, a.shape[1]), lambda i: (i, 0))
    n_tiled = {None: 0, "proj": 1, "hgrn": 2}[mixer]
    mixer_specs = [tile(a) if i < n_tiled else _resident(a.shape) for i, a in enumerate(mixer_args)]
    return pl.pallas_call(
        functools.partial(_ffn_kernel, mixer),
        grid=(m // tm,),
        in_specs=mixer_specs + [tile(x), _resident((1, D_MODEL)), _resident(wgu.shape), _resident(wd.shape)],
        out_specs=tile(x),
        out_shape=jax.ShapeDtypeStruct((m, D_MODEL), F32),
        compiler_params=_params("parallel"),
        name="ffn" if mixer is None else "ffn_" + mixer,
    )(*mixer_args, x, g, wgu, wd)


def _hgrn_q_row(direction, hb, t):
    return ((t % (2 * hb)) >= hb) != (direction == 1)


def _hgrn_ref_row(direction, hb, t):
    return (t // (2 * hb)) * (2 * hb) + (hb - 1 if direction == 0 else hb)


def _hgrn_tables():
    c = HG_CHUNK
    t = np.arange(c)[:, None]
    s = np.arange(c)[None, :]
    tri = np.stack([s <= t, s >= t]).astype(np.float32)
    wide, full = [], []
    for d in range(2):
        wide_d, full_d = [], []
        for hb in HG_LEVELS:
            same = (t // (2 * hb)) == (s // (2 * hb))
            own = (_hgrn_q_row(d, hb, t) & same & ~_hgrn_q_row(d, hb, s)).astype(np.float32)
            if hb >= HG_SUBLANES:
                wide_d.append(own[_hgrn_q_row(d, hb, np.arange(c))])
            else:
                full_d.append(own)
        wide.append(np.stack(wide_d))
        full.append(np.stack(full_d))
    return tri, np.stack(wide), np.stack(full)


def _hgrn_proj_kernel(x_ref, g_ref, w_ref, llb_ref, l1m_ref, q_ref, lf_ref, lb_ref, v_ref, gate_ref):
    h = _rms(x_ref[...], g_ref[...]).astype(BF16)
    llb = llb_ref[...]
    l1m = l1m_ref[...]

    def log2_decay(z):
        ls = jnp.minimum(z, 0.0) - jnp.log(1.0 + jnp.exp2(jnp.abs(z) * -LOG2E))
        y = l1m + ls
        return (jnp.maximum(llb, y) + jnp.log(1.0 + jnp.exp2(jnp.abs(llb - y) * -LOG2E))) * LOG2E

    finishes = ((q_ref, lambda y: (y * HG_DK ** -0.5).astype(BF16)), (lf_ref, log2_decay), (lb_ref, log2_decay),
                (v_ref, lambda y: y.astype(BF16)), (gate_ref, lambda y: y))
    rows = h.shape[0] // HG_PROJ_ROW_SPLITS

    def part(r0):
        hs = h[r0:r0 + rows]
        for j, (o_ref, finish) in enumerate(finishes):
            y = _dot(hs, w_ref[:, j * D_MODEL:(j + 1) * D_MODEL])
            yield
            o_ref[r0:r0 + rows, :] = finish(y)

    _interleave([part(r0) for r0 in range(0, h.shape[0], rows)])


def _hgrn_proj(x, g, w, llb, l1m):
    m = x.shape[0]
    tm = min(TOKEN_TILE, m)
    tile = lambda: pl.BlockSpec((tm, D_MODEL), lambda i: (i, 0))
    return pl.pallas_call(
        _hgrn_proj_kernel,
        grid=(m // tm,),
        in_specs=[tile(), _resident((1, D_MODEL)), _resident(w.shape),
                  _resident((1, D_MODEL)), _resident((1, D_MODEL))],
        out_specs=[tile()] * 5,
        out_shape=[jax.ShapeDtypeStruct((m, D_MODEL), dt) for dt in (BF16, F32, F32, BF16, F32)],
        compiler_params=_params("parallel"),
        name="hgrn_proj",
    )(x, g, w, llb, l1m)


def _hgrn_chunk(direction, q32, v, logf2, tri_ref, wide_ref, full_ref, cum_ref, st_ref):
    c = HG_CHUNK
    nv = c // HG_SUBLANES
    rows = lambda a, i: a[i * HG_SUBLANES:(i + 1) * HG_SUBLANES]
    f = jnp.exp2(logf2)
    k = 1.0 - f
    hi = logf2.astype(BF16)
    lo = (logf2 - hi.astype(F32)).astype(BF16)
    tri = tri_ref[direction]
    cum = _dot(tri, hi) + _dot(tri, lo)
    heads = HG_HEADS_PER_STEP
    hsl = [slice(h * HG_DK, (h + 1) * HG_DK) for h in range(heads)]
    for h in range(heads):
        cum_ref[direction, h] = cum[:, hsl[h]]
    yield
    ref_row = lambda r: jnp.concatenate([cum_ref[direction, h, r:r + 1, :] for h in range(heads)], axis=-1)
    ref_rows = lambda r: jnp.broadcast_to(ref_row(r), (HG_SUBLANES, q32.shape[1]))
    att = [[None] * nv for _ in range(heads)]

    def add_rows(h, i, piece):
        att[h][i] = piece if att[h][i] is None else att[h][i] + piece

    sub = lax.broadcasted_iota(jnp.int32, (c, q32.shape[1]), 0)
    sub8 = lax.broadcasted_iota(jnp.int32, (HG_SUBLANES, q32.shape[1]), 0)
    n_wide = 0
    for li, hb in enumerate(HG_LEVELS):
        if hb >= HG_SUBLANES:
            parts, q_idx = [], []
            for i in range(nv):
                t0 = i * HG_SUBLANES
                ref = ref_rows(int(_hgrn_ref_row(direction, hb, t0)))
                if bool(_hgrn_q_row(direction, hb, t0)):
                    parts.append(rows(q32, i) * jnp.exp2(rows(cum, i) - ref))
                    q_idx.append(i)
                else:
                    parts.append(rows(k, i) * jnp.exp2(ref - rows(cum, i)))
            w = jnp.concatenate(parts, axis=0).astype(BF16)
            wq = jnp.concatenate([parts[i] for i in q_idx], axis=0).astype(BF16)
            for h in range(heads):
                a = _dot_nt(wq[:, hsl[h]], w[:, hsl[h]]) * wide_ref[direction, n_wide]
                for j, i in enumerate(q_idx):
                    add_rows(h, i, rows(a, j))
            n_wide += 1
        else:
            q_rows = _hgrn_q_row(direction, hb, sub)
            if hb == 1:
                w = jnp.where(q_rows, q32 * f, k)
            else:
                pieces = []
                for i in range(nv):
                    t0 = i * HG_SUBLANES
                    if 2 * hb == HG_SUBLANES:
                        pieces.append(ref_rows(int(_hgrn_ref_row(direction, hb, t0))))
                    else:
                        lo_ref = ref_rows(int(_hgrn_ref_row(direction, hb, t0)))
                        hi_ref = ref_rows(int(_hgrn_ref_row(direction, hb, t0 + 2 * hb)))
                        pieces.append(jnp.where(sub8 < 2 * hb, lo_ref, hi_ref))
                d = cum - jnp.concatenate(pieces, axis=0)
                w = jnp.where(q_rows, q32, k) * jnp.exp2(jnp.where(q_rows, d, -d))
            w = w.astype(BF16)
            for h in range(heads):
                a = _dot_nt(w[:, hsl[h]], w[:, hsl[h]]) * full_ref[direction, li - n_wide]
                for i in range(nv):
                    add_rows(h, i, rows(a, i))
        yield

    last = c - 1 if direction == 0 else 0
    total = ref_row(last)
    qi = (q32 * jnp.exp2(cum)).astype(BF16)
    ki = (k * jnp.exp2(total - cum)).astype(BF16)
    qk = q32 * k
    decay = jnp.exp2(total)
    outs = []
    for h in range(heads):
        vh = v[:, hsl[h]]
        st = st_ref[direction, h]
        o = _dot(jnp.concatenate(att[h], axis=0).astype(BF16), vh)
        o = o + jnp.sum(qk[:, hsl[h]], axis=-1, keepdims=True) * vh.astype(F32)
        o = o + _dot_nt(qi[:, hsl[h]], st.astype(BF16))
        st_ref[direction, h] = st * decay[:, hsl[h]] + _dot_tn(vh, ki[:, hsl[h]])
        outs.append(o)
    return jnp.concatenate(outs, axis=-1)


def _hgrn_kernel(q_ref, lf_ref, lb_ref, v_ref, tri_ref, wide_ref, full_ref,
                 o_ref, ob_ref, cum_ref, st_ref):
    t = q_ref.shape[1]
    c = HG_CHUNK
    n = t // c
    st_ref[...] = jnp.zeros_like(st_ref)

    def chunk(direction, gate_ref, r0, cum_slot):
        return _hgrn_chunk(direction, q_ref[0, pl.ds(r0, c), :].astype(F32), v_ref[0, pl.ds(r0, c), :],
                           gate_ref[0, pl.ds(r0, c), :], tri_ref, wide_ref, full_ref, cum_slot, st_ref)

    def body(i, carry):
        gens, dests = [], []
        for u in range(HG_UNROLL):
            rf = pl.multiple_of((i * HG_UNROLL + u) * c, c)
            rb = pl.multiple_of((n - 1 - i * HG_UNROLL - u) * c, c)
            gens += [chunk(0, lf_ref, rf, cum_ref.at[u]), chunk(1, lb_ref, rb, cum_ref.at[u])]
            dests += [(o_ref.at[0], rf), (ob_ref, rb)]
        for (dst, r0), o in zip(dests, _interleave(gens)):
            dst[pl.ds(r0, c), :] = o
        return carry

    lax.fori_loop(0, n // HG_UNROLL, body, 0)
    o_ref[0] = o_ref[0] + ob_ref[...]


def _hgrn_scan(q, lf, lb, v):
    b, t, _ = q.shape
    assert t % (HG_CHUNK * HG_UNROLL) == 0
    tri, wide, full = _hgrn_tables()
    tri = jnp.asarray(tri, BF16)
    wide = jnp.asarray(wide, F32)
    full = jnp.asarray(full, F32)
    width = HG_HEADS_PER_STEP * HG_DK
    seq = lambda: pl.BlockSpec((1, t, width), lambda i, j: (i, 0, j))
    return pl.pallas_call(
        _hgrn_kernel,
        grid=(b, HG_HEADS // HG_HEADS_PER_STEP),
        in_specs=[seq(), seq(), seq(), seq(),
                  _resident(tri.shape), _resident(wide.shape), _resident(full.shape)],
        out_specs=seq(),
        out_shape=jax.ShapeDtypeStruct((b, t, D_MODEL), F32),
        scratch_shapes=[pltpu.VMEM((t, width), F32),
                        pltpu.VMEM((HG_UNROLL, 2, HG_HEADS_PER_STEP, HG_CHUNK, HG_DK), F32),
                        pltpu.VMEM((2, HG_HEADS_PER_STEP, HG_DV, HG_DK), F32)],
        compiler_params=_params("parallel", "parallel"),
        name="hgrn_scan",
    )(q, lf, lb, v, tri, wide, full)


def _hgrn_mixer(x, mix_g, w_in, g_norm, w_out, lb):
    b, t, _ = x.shape
    m = b * t
    lb = lb.astype(F32).reshape(1, D_MODEL)
    q, lf, lbw, v, gate = _hgrn_proj(x.reshape(m, D_MODEL), mix_g, w_in, jnp.log(lb), jnp.log1p(-lb))
    r3 = lambda a: a.reshape(b, t, D_MODEL)
    o = _hgrn_scan(r3(q), r3(lf), r3(lbw), r3(v))
    return "hgrn", (o.reshape(m, D_MODEL), gate, g_norm.reshape(1, HG_DV), w_out)


def _na_proj_kernel(x_ref, g_ref, w_ref, qg_ref, kg_ref, bd_ref, q_ref, k_ref, v_ref):
    h = _rms(x_ref[...], g_ref[...]).astype(BF16)
    bd = bd_ref[...]

    def head_norm(y, gain, mult):
        width = bd.shape[0]
        parts = []
        for j in range(D_MODEL // width):
            ys = y[:, j * width:(j + 1) * width]
            ss = _dot((ys * ys).astype(BF16), bd)
            parts.append(ys * lax.rsqrt(ss * (1.0 / NA_HEAD_DIM) + NORM_EPS) * (gain * mult))
        return jnp.concatenate(parts, axis=-1)

    q = _dot(h, w_ref[:, 0:D_MODEL])
    q_ref[...] = head_norm(q, qg_ref[...], NA_HEAD_DIM ** -0.5 * LOG2E).astype(BF16)
    k = _dot(h, w_ref[:, D_MODEL:2 * D_MODEL])
    k_ref[...] = head_norm(k, kg_ref[...], 1.0).astype(BF16)
    v_ref[...] = _dot(h, w_ref[:, 2 * D_MODEL:3 * D_MODEL]).astype(BF16)


def _na_proj(x, g, w, qg, kg):
    m = x.shape[0]
    tm = min(TOKEN_TILE, m)
    blk = np.arange(2 * LANES) // NA_HEAD_DIM
    bd = jnp.asarray(blk[:, None] == blk[None, :], BF16)
    tile = lambda: pl.BlockSpec((tm, D_MODEL), lambda i: (i, 0))
    return pl.pallas_call(
        _na_proj_kernel,
        grid=(m // tm,),
        in_specs=[tile(), _resident((1, D_MODEL)), _resident(w.shape),
                  _resident(qg.shape), _resident(kg.shape), _resident(bd.shape)],
        out_specs=[tile(), tile(), tile()],
        out_shape=[jax.ShapeDtypeStruct((m, D_MODEL), BF16)] * 3,
        compiler_params=_params("parallel"),
        name="na_proj",
    )(x, g, w, qg, kg, bd)


def _na_group_start(g, rows):
    return jnp.clip(g * NA_GROUP - NA_WIN_R // 2, 0, rows - NA_UNION) if isinstance(g, jax.Array) else \
        int(np.clip(g * NA_GROUP - NA_WIN_R // 2, 0, rows - NA_UNION))


def _na_attn_kernel(rows, q_ref, k_ref, v_ref, bias_ref, cm_ref, o_ref):
    gq = NA_GROUP * GRID_W
    uk = NA_UNION * GRID_W
    n_groups = rows // NA_GROUP
    lane = lax.broadcasted_iota(jnp.int32, (gq, LANES), 1)
    first = lane < NA_HEAD_DIM
    head_sel = (jnp.where(first[0:1], 1.0, 0.0).astype(BF16), jnp.where(first[0:1], 0.0, 1.0).astype(BF16))

    def group(g):
        kind = jnp.where(g == 0, 0, jnp.where(g == n_groups - 1, 2, 1))
        qs = pl.multiple_of(g * gq, gq)
        ks = pl.multiple_of(_na_group_start(g, rows) * GRID_W, NA_GROUP * GRID_W)
        q = q_ref[0, pl.ds(qs, gq), :]
        kb = k_ref[0, pl.ds(ks, uk), :]
        vb = v_ref[0, pl.ds(ks, uk), :]
        s = _dot_nt(jnp.concatenate([q * head_sel[0], q * head_sel[1]], axis=0), kb)
        yield
        cm = cm_ref[kind]
        probs = []
        for hh in range(2):
            sh = s[hh * gq:(hh + 1) * gq] * cm + bias_ref[hh, kind]
            probs.append(jnp.exp2((sh - jnp.max(sh, axis=-1, keepdims=True)).astype(BF16)))
            yield
        v_ext = jnp.concatenate([vb, jnp.ones_like(vb)], axis=-1)
        o2 = _dot(jnp.concatenate(probs, axis=0), v_ext)
        o2 = o2[:, :LANES] * (1.0 / o2[:, LANES:])
        o_ref[0, pl.ds(qs, gq), :] = jnp.where(first, o2[:gq], o2[gq:]).astype(o_ref.dtype)

    def body(i, carry):
        _interleave([group(i * NA_UNROLL + u) for u in range(NA_UNROLL)])
        return carry

    lax.fori_loop(0, n_groups // NA_UNROLL, body, 0)


def _na_attn(q, k, v, bias, cm):
    b, t, _ = q.shape
    rows = t // GRID_W
    seq = lambda: pl.BlockSpec((1, t, LANES), lambda j, i: (i, 0, j))
    return pl.pallas_call(
        functools.partial(_na_attn_kernel, rows),
        grid=(D_MODEL // LANES, b),
        in_specs=[seq(), seq(), seq(),
                  pl.BlockSpec((2,) + bias.shape[1:], lambda j, i: (j, 0, 0, 0)),
                  _resident(cm.shape)],
        out_specs=seq(),
        out_shape=jax.ShapeDtypeStruct((b, t, D_MODEL), BF16),
        compiler_params=_params("parallel", "parallel"),
        name="na_attn",
    )(q, k, v, bias, cm)


def _na_tables(rpb, rows):
    assert rows % NA_GROUP == 0 and rows >= NA_UNION + NA_GROUP
    cols = np.arange(GRID_W)
    col_start = np.clip(cols - NA_WIN_C // 2, 0, GRID_W - NA_WIN_C)
    col_mask = (cols[None, :] >= col_start[:, None]) & (cols[None, :] < col_start[:, None] + NA_WIN_C)
    col_off = np.clip(cols[None, :] - cols[:, None] + NA_WIN_C - 1, 0, 2 * NA_WIN_C - 2)
    n_groups = rows // NA_GROUP
    by_col = rpb.astype(F32)[:, :, col_off].transpose(0, 2, 1, 3)
    pad = NA_UNION
    by_col = jnp.pad(by_col, ((0, 0), (0, 0), (pad, pad), (0, 0)))
    by_col = by_col.reshape(NA_HEADS, GRID_W, -1) * LOG2E
    valid, kinds = [], []
    for g in (0, 1, n_groups - 1):
        start = _na_group_start(g, rows)
        key = start + np.arange(NA_UNION)[None, :]
        r = g * NA_GROUP + np.arange(NA_GROUP)[:, None]
        r0 = np.clip(r - NA_WIN_R // 2, 0, rows - NA_WIN_R)
        valid.append((key >= r0) & (key < r0 + NA_WIN_R))
        per_row = []
        for a in range(NA_GROUP):
            off = start - (g * NA_GROUP + a) + NA_WIN_R - 1 + pad
            per_row.append(by_col[:, :, off * GRID_W:(off + NA_UNION) * GRID_W])
        kinds.append(jnp.stack(per_row, axis=1))
    valid = np.stack(valid)
    assert (valid.sum(-1) == NA_WIN_R).all()
    full_valid = valid[:, :, None, :, None] & col_mask[None, None, :, None, :]
    shape = (3, NA_GROUP * GRID_W, NA_UNION * GRID_W)
    full_valid = full_valid.reshape(shape)
    tab = jnp.stack(kinds, axis=1).reshape((NA_HEADS,) + shape)
    tab = jnp.where(full_valid[None], tab, NEG_INF * LOG2E)
    return tab, jnp.asarray(full_valid, F32)


def _na_mixer(x, mix_g, w_in, q_norm, k_norm, rpb, w_out):
    b, t, _ = x.shape
    m = b * t
    tile2 = lambda g: jnp.tile(g.astype(F32), 2 * LANES // NA_HEAD_DIM).reshape(1, 2 * LANES)
    q, k, v = _na_proj(x.reshape(m, D_MODEL), mix_g, w_in, tile2(q_norm), tile2(k_norm))
    bias, cm = _na_tables(rpb, t // GRID_W)
    r3 = lambda a: a.reshape(b, t, D_MODEL)
    o = _na_attn(r3(q), r3(k), r3(v), bias, cm)
    return "proj", (o.reshape(m, D_MODEL), w_out)


_MLA_PERM = np.concatenate([
    MLA_NOPE + np.arange(16),
    np.arange(48),
    MLA_NOPE + 16 + np.arange(16),
    48 + np.arange(16),
    -np.ones(32, np.int64),
]).astype(np.int64)


def _mla_prep_kernel(x_ref, g_ref, win_ref, qa_ref, kva_ref, wuq_ref, wuk_ref, wuv_ref,
                     bd_ref, aq_ref, bq_ref, ak_ref, bk_ref, q_ref, k_ref, v_ref):
    h = _rms(x_ref[...], g_ref[...]).astype(BF16)
    c = _dot(h, win_ref[...])
    cq = _rms(c[:, :MLA_Q_LORA], qa_ref[...]).astype(BF16)
    ckv = _rms(c[:, MLA_Q_LORA:MLA_Q_LORA + MLA_KV_LORA], kva_ref[...]).astype(BF16)
    k_rope = c[:, MLA_Q_LORA + MLA_KV_LORA:]
    two = lambda a: jnp.concatenate([a, a], axis=-1)
    aq, bq, ak = two(aq_ref[...]), two(bq_ref[...]), two(ak_ref[...])
    k_rope2 = two(k_rope)
    k_partner = two(pltpu.roll(k_rope, LANES // 2, 1) * bk_ref[...])
    bd = bd_ref[...]

    def inv_rms(y):
        ss = _dot((y * y).astype(BF16), bd)
        return lax.rsqrt(ss * (1.0 / MLA_QK) + NORM_EPS)

    def pair(p):
        sl = slice(p * 2 * LANES, (p + 1) * 2 * LANES)
        yq = _dot(cq, wuq_ref[:, sl])
        yk = _dot(ckv, wuk_ref[:, sl]) + k_rope2
        yield
        yq_partner = jnp.concatenate([pltpu.roll(yq[:, :LANES], LANES // 2, 1),
                                      pltpu.roll(yq[:, LANES:], LANES // 2, 1)], axis=-1)
        rq = inv_rms(yq)
        rk = inv_rms(yk)
        yield
        q_ref[:, sl] = ((yq * aq + yq_partner * bq) * rq).astype(BF16)
        k_ref[:, sl] = ((yk * ak + k_partner) * rk).astype(BF16)

    for p0 in range(0, MLA_HEADS // 2, MLA_PREP_UNROLL):
        _interleave([pair(p) for p in range(p0, p0 + MLA_PREP_UNROLL)])
    v_ref[...] = _dot(ckv, wuv_ref[...]).astype(BF16)


def _mla_prep(x, g, win, qa, kva, wuq, wuk, wuv, bd, aq, bq, ak, bk, t):
    m = x.shape[0]
    tm = min(TOKEN_TILE, t)
    per_seq = t // tm
    wide = MLA_HEADS * LANES
    table = lambda: pl.BlockSpec((tm, LANES), lambda i: (i % per_seq, 0))
    return pl.pallas_call(
        _mla_prep_kernel,
        grid=(m // tm,),
        in_specs=[pl.BlockSpec((tm, D_MODEL), lambda i: (i, 0)),
                  _resident((1, D_MODEL)), _resident(win.shape),
                  _resident((1, MLA_Q_LORA)), _resident((1, MLA_KV_LORA)),
                  _resident(wuq.shape), _resident(wuk.shape), _resident(wuv.shape),
                  _resident(bd.shape),
                  table(), table(), table(), table()],
        out_specs=[pl.BlockSpec((tm, wide), lambda i: (i, 0)),
                   pl.BlockSpec((tm, wide), lambda i: (i, 0)),
                   pl.BlockSpec((tm, D_MODEL), lambda i: (i, 0))],
        out_shape=[jax.ShapeDtypeStruct((m, wide), BF16),
                   jax.ShapeDtypeStruct((m, wide), BF16),
                   jax.ShapeDtypeStruct((m, D_MODEL), BF16)],
        compiler_params=_params("parallel"),
        name="mla_prep",
    )(x, g, win, qa, kva, wuq, wuk, wuv, bd, aq, bq, ak, bk)


def _mla_attn_kernel(q_ref, k_ref, v_ref, o_ref):
    sub = MLA_Q_SUBTILE
    first = lax.broadcasted_iota(jnp.int32, (sub, LANES), 1) < MLA_V
    v_ext = jnp.concatenate([v_ref[0], jnp.ones((v_ref.shape[1], LANES), BF16)], axis=-1)

    def head(r0, hh):
        sl = slice(hh * LANES, (hh + 1) * LANES)
        s = _dot_nt(q_ref[0, r0:r0 + sub, sl], k_ref[0, :, sl])
        yield
        p = jnp.exp2((s - jnp.max(s, axis=-1, keepdims=True)).astype(BF16))
        yield
        o = _dot(p, v_ext)
        return o[:, :LANES] * (1.0 / o[:, LANES:])

    starts = range(0, q_ref.shape[1], sub)
    outs = _interleave([head(r0, hh) for r0 in starts for hh in range(2)])
    for n, r0 in enumerate(starts):
        o_ref[0, r0:r0 + sub, :] = jnp.where(first, outs[2 * n], outs[2 * n + 1]).astype(o_ref.dtype)


def _mla_attn(q, k, v):
    b, t, _ = q.shape
    tq = min(MLA_Q_TILE, t)
    return pl.pallas_call(
        _mla_attn_kernel,
        grid=(b, MLA_HEADS // 2, t // tq),
        in_specs=[pl.BlockSpec((1, tq, 2 * LANES), lambda i, j, l: (i, l, j)),
                  pl.BlockSpec((1, t, 2 * LANES), lambda i, j, l: (i, 0, j)),
                  pl.BlockSpec((1, t, LANES), lambda i, j, l: (i, 0, j))],
        out_specs=pl.BlockSpec((1, tq, LANES), lambda i, j, l: (i, l, j)),
        out_shape=jax.ShapeDtypeStruct((b, t, D_MODEL), BF16),
        compiler_params=_params("parallel", "parallel", "arbitrary"),
        name="mla_attn",
    )(q, k, v)


def _mla_weights(w_in, w_uq, w_ukv, q_norm, k_norm, t):
    valid = _MLA_PERM >= 0
    src = np.where(valid, _MLA_PERM, 0)
    is_rope = valid & (_MLA_PERM >= MLA_NOPE)
    is_nope = valid & (_MLA_PERM < MLA_NOPE)

    def place(w, per_head, lanes_ok, base=0):
        wh = w.reshape(w.shape[0], MLA_HEADS, per_head)[:, :, base + np.where(lanes_ok, src, 0)]
        return jnp.where(lanes_ok[None, None, :], wh, 0.0).reshape(w.shape[0], MLA_HEADS * LANES)

    wuq = place(w_uq, MLA_QK, valid)
    wuk = place(w_ukv, MLA_NOPE + MLA_V, is_nope)
    wuv = w_ukv.reshape(MLA_KV_LORA, MLA_HEADS, MLA_NOPE + MLA_V)[:, :, MLA_NOPE:].reshape(
        MLA_KV_LORA, MLA_HEADS * MLA_V)
    rope_cols = w_in[:, MLA_Q_LORA + MLA_KV_LORA:]
    rope_placed = jnp.where(is_rope[None, :], rope_cols[:, np.where(is_rope, src - MLA_NOPE, 0)], 0.0)
    win = jnp.concatenate([w_in[:, :MLA_Q_LORA + MLA_KV_LORA], rope_placed], axis=1)
    gain = lambda g: jnp.where(valid, g.astype(F32)[src], 0.0).reshape(1, LANES)
    partner = (np.arange(LANES) + LANES // 2) % LANES
    gain_partner = lambda g: jnp.where(is_rope, gain(g)[0, partner], 0.0).reshape(1, LANES)

    half = MLA_ROPE // 2
    inv_freq = ROPE_THETA ** (-jnp.arange(half, dtype=F32) / half)
    ang = jnp.arange(t).astype(F32)[:, None] * inv_freq[None, :]
    rest = LANES // 2 - half
    one, zero = jnp.ones((t, rest), F32), jnp.zeros((t, rest), F32)
    cos = jnp.concatenate([jnp.cos(ang), one, jnp.cos(ang), one], axis=1)
    sin = jnp.concatenate([-jnp.sin(ang), zero, jnp.sin(ang), zero], axis=1)
    q_mult = MLA_QK ** -0.5 * LOG2E
    tables = (gain(q_norm) * cos * q_mult, gain_partner(q_norm) * sin * q_mult,
              gain(k_norm) * cos, gain_partner(k_norm) * sin)

    lane2 = np.arange(2 * LANES)
    bd = (lane2[:, None] // LANES) == (lane2[None, :] // LANES)
    return (win.astype(BF16), wuq.astype(BF16), wuk.astype(BF16), wuv.astype(BF16), jnp.asarray(bd, BF16)) + tables


def _mla_mixer(x, mix_g, w_in, q_a_norm, w_uq, kv_a_norm, w_ukv, q_norm, k_norm, w_out):
    b, t, _ = x.shape
    m = b * t
    prepared = _mla_weights(w_in, w_uq, w_ukv, q_norm, k_norm, t)
    q, k, v = _mla_prep(x.reshape(m, D_MODEL), mix_g, prepared[0], q_a_norm.astype(F32).reshape(1, -1),
                        kv_a_norm.astype(F32).reshape(1, -1), *prepared[1:], t)
    o = _mla_attn(q.reshape(b, t, -1), k.reshape(b, t, -1), v.reshape(b, t, -1))
    return "proj", (o.reshape(m, D_MODEL), w_out)


def kernel(x, ffn1_norm, ffn1_w_gu, ffn1_w_down, mix_norm, ffn2_norm, ffn2_w_gu, ffn2_w_down,
           hg_lb_logits, hg_w_in, hg_g_norm, hg_w_out,
           na_w_in, na_q_norm, na_k_norm, na_rpb, na_w_out,
           mla_w_in, mla_q_a_norm, mla_w_uq, mla_kv_a_norm, mla_w_ukv, mla_q_norm, mla_k_norm, mla_w_out):
    b, t, d = x.shape
    m = b * t
    gam = jnp.cumsum(jax.nn.softmax(hg_lb_logits.astype(F32), axis=0), axis=0)
    lb_all = gam - gam[0:1]
    row = lambda g: g.astype(F32).reshape(1, -1)
    bf = lambda w: w.astype(BF16)
    ia = ib = ic = 0
    for layer in range(DEPTH):
        x = _ffn(x.reshape(m, d), row(ffn1_norm[layer]), bf(ffn1_w_gu[layer]),
                 bf(ffn1_w_down[layer])).reshape(b, t, d)
        g = row(mix_norm[layer])
        kind = layer % N_MIXERS
        if kind == 0:
            mixer = _hgrn_mixer(x, g, bf(hg_w_in[ia]), hg_g_norm[ia].astype(F32), bf(hg_w_out[ia]), lb_all[layer])
            ia += 1
        elif kind == 1:
            mixer = _na_mixer(x, g, bf(na_w_in[ib]), na_q_norm[ib], na_k_norm[ib], na_rpb[ib], bf(na_w_out[ib]))
            ib += 1
        else:
            mixer = _mla_mixer(x, g, mla_w_in[ic], mla_q_a_norm[ic], mla_w_uq[ic], mla_kv_a_norm[ic],
                               mla_w_ukv[ic], mla_q_norm[ic], mla_k_norm[ic], bf(mla_w_out[ic]))
            ic += 1
        x = _ffn(x.reshape(m, d), row(ffn2_norm[layer]), bf(ffn2_w_gu[layer]),
                 bf(ffn2_w_down[layer]), *mixer).reshape(b, t, d)
    return x
```
